```python
import math
import jax, jax.numpy as jnp
from jax import lax
import numpy as np

D_MODEL = 1024
BATCH = 4
SEQ = 4096
DEPTH = 2

N_MIXERS = 2
DA_HEADS = 8
DA_HEAD_DIM = D_MODEL // DA_HEADS // 2
DA_V_DIM = 2 * DA_HEAD_DIM
ROPE_THETA = 10000.0
Q_BLOCK = 128
CONV_WIDTH = 31
D_FF_DENSE = 2816
N_EXPERTS = 8
TOP_K = 2
D_FF_EXPERT = 3584
MOE_BLOCK = 128
EPS = 1e-6
N_EVEN = (DEPTH + 1) // 2
N_ODD = DEPTH // 2

kernel_name = "hybrid_diffattn_conformer_moe_adaln"


def rms_norm(x, g):
    xf = x.astype(jnp.float32)
    y = xf * lax.rsqrt(jnp.mean(xf * xf, axis=-1, keepdims=True) + EPS)
    return (y * g.astype(jnp.float32)).astype(x.dtype)


def layer_norm(x, g, b):
    xf = x.astype(jnp.float32)
    mu = jnp.mean(xf, axis=-1, keepdims=True)
    var = jnp.mean(jnp.square(xf - mu), axis=-1, keepdims=True)
    y = (xf - mu) * lax.rsqrt(var + EPS)
    return (y * g.astype(jnp.float32) + b.astype(jnp.float32)).astype(x.dtype)


def modulate(h, shift, scale):
    return h * (1 + scale[:, None, :]) + shift[:, None, :]


def rope(x, positions):
    dh = x.shape[-1]
    half = dh // 2
    inv_freq = ROPE_THETA ** (-jnp.arange(half, dtype=jnp.float32) / half)
    ang = positions.astype(jnp.float32)[..., None] * inv_freq
    cos = jnp.cos(ang)[:, :, None, :]
    sin = jnp.sin(ang)[:, :, None, :]
    xf = x.astype(jnp.float32)
    x1, x2 = xf[..., :half], xf[..., half:]
    out = jnp.concatenate([x1 * cos - x2 * sin, x2 * cos + x1 * sin], axis=-1)
    return out.astype(x.dtype)


def diff_attention(h, positions, w_qkv, w_o, lam_q1, lam_k1, lam_q2, lam_k2,
                   subln_g, lambda_init):
    B, T, D = h.shape
    qkv = h @ w_qkv
    q, k, v = jnp.split(qkv, 3, axis=-1)
    q = q.reshape(B, T, 2 * DA_HEADS, DA_HEAD_DIM)
    k = k.reshape(B, T, 2 * DA_HEADS, DA_HEAD_DIM)
    v = v.reshape(B, T, DA_HEADS, DA_V_DIM)
    q = rope(q, positions) * (DA_HEAD_DIM ** -0.5)
    k = rope(k, positions)
    lam = (jnp.exp(jnp.sum(lam_q1.astype(jnp.float32) * lam_k1.astype(jnp.float32)))
           - jnp.exp(jnp.sum(lam_q2.astype(jnp.float32) * lam_k2.astype(jnp.float32)))
           + lambda_init)
    nb = T // Q_BLOCK
    qb = q.reshape(B, nb, Q_BLOCK, 2 * DA_HEADS, DA_HEAD_DIM).transpose(1, 0, 3, 2, 4)
    kt = k.transpose(0, 2, 1, 3)
    vt = v.transpose(0, 2, 1, 3)
    key_pos = jnp.arange(T)

    def block(args):
        q_blk, start = args
        s = jnp.einsum('bhqd,bhkd->bhqk', q_blk, kt).astype(jnp.float32)
        q_pos = start + jnp.arange(Q_BLOCK)
        causal = key_pos[None, :] <= q_pos[:, None]
        s = jnp.where(causal, s, -jnp.inf)
        p = jax.nn.softmax(s, axis=-1).reshape(B, DA_HEADS, 2, Q_BLOCK, T)
        a = p[:, :, 0] - lam * p[:, :, 1]
        return jnp.einsum('bhqk,bhkd->bhqd', a.astype(vt.dtype), vt)

    o = lax.map(block, (qb, jnp.arange(nb) * Q_BLOCK))
    o = o.transpose(1, 0, 3, 2, 4).reshape(B, T, DA_HEADS, DA_V_DIM)
    o = rms_norm(o, subln_g) * (1 - lambda_init)
    return o.reshape(B, T, DA_HEADS * DA_V_DIM) @ w_o


def conformer_conv(h, w_pw1, b_pw1, w_dw, b_dw, ln_g, ln_b, w_pw2, b_pw2):
    D = h.shape[-1]
    u = h @ w_pw1 + b_pw1
    a, g = jnp.split(u, 2, axis=-1)
    u = a * jax.nn.sigmoid(g)
    u = lax.conv_general_dilated(
        u, w_dw[:, None, :], window_strides=(1,),
        padding=[(CONV_WIDTH - 1, 0)],
        dimension_numbers=('NWC', 'WIO', 'NWC'),
        feature_group_count=D) + b_dw
    u = jax.nn.silu(layer_norm(u, ln_g, ln_b))
    return u @ w_pw2 + b_pw2


def dense_swiglu(h, w_gate, w_up, w_down):
    return (jax.nn.silu(h @ w_gate) * (h @ w_up)) @ w_down


def moe_swiglu(h, w_router, w_gate, w_up, w_down):
    B, T, D = h.shape
    xt = h.reshape(-1, D)
    N = xt.shape[0]
    A = N * TOP_K
    logits = (xt @ w_router).astype(jnp.float32)
    top_val, top_idx = lax.top_k(logits, TOP_K)
    gates = jax.nn.softmax(top_val, axis=-1)
    flat_e = top_idx.reshape(-1)
    order = jnp.argsort(flat_e)
    tok = order // TOP_K
    sorted_e = flat_e[order]
    sizes = jnp.bincount(flat_e, length=N_EXPERTS)
    padded = (sizes + MOE_BLOCK - 1) // MOE_BLOCK * MOE_BLOCK
    pad_end = jnp.cumsum(padded)
    pad_start = pad_end - padded
    grp_start = jnp.cumsum(sizes) - sizes
    dest = pad_start[sorted_e] + (jnp.arange(A) - grp_start[sorted_e])
    P = A + N_EXPERTS * MOE_BLOCK
    nb = P // MOE_BLOCK
    block_e = jnp.minimum(
        jnp.searchsorted(pad_end, jnp.arange(nb) * MOE_BLOCK, side='right'),
        N_EXPERTS - 1)
    x_pad = jnp.zeros((P, D), xt.dtype).at[dest].set(xt[tok])

    def expert_block(args):
        xb, e = args
        return (jax.nn.silu(xb @ w_gate[e]) * (xb @ w_up[e])) @ w_down[e]

    y_pad = lax.map(expert_block, (x_pad.reshape(nb, MOE_BLOCK, D), block_e)).reshape(P, D)
    g_sorted = gates.reshape(-1)[order].astype(xt.dtype)
    y = y_pad[dest] * g_sorted[:, None]
    out = jnp.zeros_like(xt).at[tok].add(y)
    return out.reshape(B, T, D)


def setup_inputs(seed: int = 0) -> dict:
    key = jax.random.key(seed)
    ks = iter(jax.random.split(key, 40))
    D = D_MODEL
    f32 = jnp.float32

    def nrm(shape, fan_in, mult=1.0):
        return jax.random.normal(next(ks), shape, f32) * (mult * fan_in ** -0.5)

    def gain(shape):
        return 1.0 + 0.01 * jax.random.normal(next(ks), shape, f32)

    def bias(shape):
        return 0.01 * jax.random.normal(next(ks), shape, f32)

    x = jax.random.normal(next(ks), (BATCH, SEQ, D), f32)
    c = jax.random.normal(next(ks), (BATCH, D), f32)
    offset = jax.random.randint(next(ks), (BATCH, 1), 0, SEQ, dtype=jnp.int32)
    positions = jnp.arange(SEQ, dtype=jnp.int32)[None, :] + offset
    return {
        "x": x,
        "c": c,
        "positions": positions,
        "w_ada": nrm((DEPTH, D, 6 * D), D, 0.5),
        "b_ada": bias((DEPTH, 6 * D)),
        "norm_mix_g": gain((DEPTH, D)),
        "norm_ffn_g": gain((DEPTH, D)),
        "attn_w_qkv": nrm((N_EVEN, D, 3 * D), D),
        "attn_w_o": nrm((N_EVEN, D, D), D),
        "lam_q1": 0.1 * jax.random.normal(next(ks), (N_EVEN, DA_HEAD_DIM), f32),
        "lam_k1": 0.1 * jax.random.normal(next(ks), (N_EVEN, DA_HEAD_DIM), f32),
        "lam_q2": 0.1 * jax.random.normal(next(ks), (N_EVEN, DA_HEAD_DIM), f32),
        "lam_k2": 0.1 * jax.random.normal(next(ks), (N_EVEN, DA_HEAD_DIM), f32),
        "attn_subln_g": gain((N_EVEN, DA_V_DIM)),
        "conv_w_pw1": nrm((N_ODD, D, 2 * D), D),
        "conv_b_pw1": bias((N_ODD, 2 * D)),
        "conv_w_dw": nrm((N_ODD, CONV_WIDTH, D), CONV_WIDTH),
        "conv_b_dw": bias((N_ODD, D)),
        "conv_ln_g": gain((N_ODD, D)),
        "conv_ln_b": bias((N_ODD, D)),
        "conv_w_pw2": nrm((N_ODD, D, D), D),
        "conv_b_pw2": bias((N_ODD, D)),
        "ffn_w_gate": nrm((N_EVEN, D, D_FF_DENSE), D),
        "ffn_w_up": nrm((N_EVEN, D, D_FF_DENSE), D),
        "ffn_w_down": nrm((N_EVEN, D_FF_DENSE, D), D_FF_DENSE),
        "moe_w_router": nrm((N_ODD, D, N_EXPERTS), D),
        "moe_w_gate": nrm((N_ODD, N_EXPERTS, D, D_FF_EXPERT), D),
        "moe_w_up": nrm((N_ODD, N_EXPERTS, D, D_FF_EXPERT), D),
        "moe_w_down": nrm((N_ODD, N_EXPERTS, D_FF_EXPERT, D), D_FF_EXPERT),
        "final_g": gain((D,)),
    }


def reference(x, c, positions, w_ada, b_ada, norm_mix_g, norm_ffn_g,
              attn_w_qkv, attn_w_o, lam_q1, lam_k1, lam_q2, lam_k2, attn_subln_g,
              conv_w_pw1, conv_b_pw1, conv_w_dw, conv_b_dw, conv_ln_g, conv_ln_b,
              conv_w_pw2, conv_b_pw2, ffn_w_gate, ffn_w_up, ffn_w_down,
              moe_w_router, moe_w_gate, moe_w_up, moe_w_down, final_g):
    c_act = jax.nn.silu(c)
    for i in range(DEPTH):
        j = i // 2
        mod = c_act @ w_ada[i] + b_ada[i]
        sh_m, sc_m, g_m, sh_f, sc_f, g_f = jnp.split(mod, 6, axis=-1)
        h = modulate(rms_norm(x, norm_mix_g[i]), sh_m, sc_m)
        if i % N_MIXERS == 0:
            lambda_init = 0.8 - 0.6 * math.exp(-0.3 * i)
            mix = diff_attention(h, positions, attn_w_qkv[j], attn_w_o[j],
                                 lam_q1[j], lam_k1[j], lam_q2[j], lam_k2[j],
                                 attn_subln_g[j], lambda_init)
        else:
            mix = conformer_conv(h, conv_w_pw1[j], conv_b_pw1[j], conv_w_dw[j],
                                 conv_b_dw[j], conv_ln_g[j], conv_ln_b[j],
                                 conv_w_pw2[j], conv_b_pw2[j])
        x = x + g_m[:, None, :] * mix
        h = modulate(rms_norm(x, norm_ffn_g[i]), sh_f, sc_f)
        if i % 2 == 0:
            ff = dense_swiglu(h, ffn_w_gate[j], ffn_w_up[j], ffn_w_down[j])
        else:
            ff = moe_swiglu(h, moe_w_router[j], moe_w_gate[j], moe_w_up[j], moe_w_down[j])
        x = x + g_f[:, None, :] * ff
    return rms_norm(x, final_g)
```

```python
import functools
import math

import numpy as np
import jax
import jax.numpy as jnp
from jax import lax
from jax.experimental import pallas as pl
from jax.experimental.pallas import tpu as pltpu

EPS = 1e-6
ROPE_THETA = 10000.0
HEAD_DIM = 64
V_DIM = 128
HALF = HEAD_DIM // 2
LANES = 128
GROUP = 4 * HEAD_DIM
CONV_WIDTH = 31
CONV_HALO = 32
TOP_K = 2
NEG = -1e30
VMEM_LIMIT = 56 * 1024 * 1024

BF16 = jnp.bfloat16
F32 = jnp.float32


def _cparams(sem):
    return pltpu.CompilerParams(dimension_semantics=sem, vmem_limit_bytes=VMEM_LIMIT)


def _dot(a, b):
    return jnp.dot(a, b, preferred_element_type=F32)


def _dot_nt(a, b):
    return lax.dot_general(a, b, (((1,), (1,)), ((), ())), preferred_element_type=F32)


def _norm_mod(x, g, shift, scale):
    y = x * lax.rsqrt(jnp.mean(x * x, axis=-1, keepdims=True) + EPS)
    return (y * g) * (1.0 + scale) + shift


def _sigmoid(x):
    return 1.0 / (1.0 + jnp.exp(-x))


def _ada_kernel(c_ref, w_ref, b_ref, o_ref):
    c = c_ref[...]
    ca = (c * _sigmoid(c)).astype(BF16)
    o_ref[0] = _dot(ca, w_ref[0].astype(BF16)) + b_ref[0]


def _ada(c8, w_ada, b_ada):
    depth, d, d6 = w_ada.shape
    tn = 1536
    return pl.pallas_call(
        _ada_kernel,
        out_shape=jax.ShapeDtypeStruct((depth, 8, d6), F32),
        grid=(depth, d6 // tn),
        in_specs=[
            pl.BlockSpec((8, d), lambda l, j: (0, 0)),
            pl.BlockSpec((1, d, tn), lambda l, j: (l, 0, j)),
            pl.BlockSpec((1, 1, tn), lambda l, j: (l, 0, j)),
        ],
        out_specs=pl.BlockSpec((1, 8, tn), lambda l, j: (l, 0, j)),
        compiler_params=_cparams(("arbitrary", "arbitrary")),
        name="ada_mod",
    )(c8, w_ada, b_ada.reshape(depth, 1, d6))


def _rope_table_kernel(pos_ref, freq_ref, cos_ref, sin_ref):
    ang = pos_ref[...].astype(F32) * freq_ref[...]
    cos_ref[...] = jnp.cos(ang)
    sin_ref[...] = jnp.sin(ang)


def _rope_table(pos_col, freq_row, tm):
    n = pos_col.shape[0]
    return pl.pallas_call(
        _rope_table_kernel,
        out_shape=(jax.ShapeDtypeStruct((n, LANES), F32), jax.ShapeDtypeStruct((n, LANES), F32)),
        grid=(n // tm,),
        in_specs=[pl.BlockSpec((tm, 1), lambda i: (i, 0)), pl.BlockSpec((1, LANES), lambda i: (0, 0))],
        out_specs=(pl.BlockSpec((tm, LANES), lambda i: (i, 0)), pl.BlockSpec((tm, LANES), lambda i: (i, 0))),
        compiler_params=_cparams(("arbitrary",)),
        name="rope_table",
    )(pos_col, freq_row)


def _qkv_kernel(x_ref, mod_ref, g_ref, cos_ref, sin_ref, wq_ref, wk_ref, wvt_ref,
                q_ref, k_ref, vt_ref, *, tk):
    m = mod_ref[0]
    h = _norm_mod(x_ref[...], g_ref[...], m[0:1], m[1:2]).astype(BF16)
    c = cos_ref[...]
    s = sin_ref[...]
    d = x_ref.shape[1]
    for w_ref, o_ref, mult in ((wq_ref, q_ref, HEAD_DIM ** -0.5), (wk_ref, k_ref, None)):
        for gi in range(d // GROUP):
            y = _dot(h, w_ref[:, gi * GROUP:(gi + 1) * GROUP])
            a = y[:, :LANES]
            b = y[:, LANES:]
            oa = a * c - b * s
            ob = b * c + a * s
            if mult is not None:
                oa = oa * mult
                ob = ob * mult
            o_ref[:, gi * GROUP:gi * GROUP + LANES] = oa.astype(BF16)
            o_ref[:, gi * GROUP + LANES:(gi + 1) * GROUP] = ob.astype(BF16)
    vt = _dot_nt(wvt_ref[...], h)
    for ci in range(vt.shape[1] // tk):
        vt_ref[0, ci] = vt[:, ci * tk:(ci + 1) * tk].astype(BF16)


def _qkv(x2, mod, g, cos, sin, wq, wk, wvt, bsz, tm, tk):
    n, d = x2.shape
    t = n // bsz
    tiles_per_b = t // tm
    row = lambda i: (i, 0)
    const = lambda i: (0, 0)
    return pl.pallas_call(
        functools.partial(_qkv_kernel, tk=tk),
        out_shape=(jax.ShapeDtypeStruct((n, d), BF16), jax.ShapeDtypeStruct((n, d), BF16),
                   jax.ShapeDtypeStruct((bsz, t // tk, d, tk), BF16)),
        grid=(n // tm,),
        in_specs=[
            pl.BlockSpec((tm, d), row),
            pl.BlockSpec((1, 8, d), lambda i: (i // tiles_per_b, 0, 0)),
            pl.BlockSpec((1, d), const),
            pl.BlockSpec((tm, LANES), row),
            pl.BlockSpec((tm, LANES), row),
            pl.BlockSpec((d, d), const),
            pl.BlockSpec((d, d), const),
            pl.BlockSpec((d, d), const),
        ],
        out_specs=(pl.BlockSpec((tm, d), row), pl.BlockSpec((tm, d), row),
                   pl.BlockSpec((1, tm // tk, d, tk),
                                lambda i: (i // tiles_per_b, i % tiles_per_b, 0, 0))),
        compiler_params=_cparams(("arbitrary",)),
        name="qkv_rope",
    )(x2, mod, g, cos, sin, wq, wk, wvt)


def _attn_kernel(q_ref, k_ref, vt_ref, lam_ref, g_ref, o_ref, acc1, acc2, *, tk, lambda_init):
    tq = q_ref.shape[0]
    p = pl.program_id(1)
    qi = pl.program_id(2)
    lane = lax.broadcasted_iota(jnp.int32, (1, GROUP), 1)
    head = (lane % LANES) // HALF
    r1 = 2 * (p % 2)
    qf = q_ref[...].astype(F32)
    qm1 = jnp.where(head == r1, qf, 0.0).astype(BF16)
    qm2 = jnp.where(head == r1 + 1, qf, 0.0).astype(BF16)
    acc1[...] = jnp.zeros_like(acc1)
    acc2[...] = jnp.zeros_like(acc2)

    def one_head(s, m, l, acc, vc):
        m_new = jnp.maximum(m, jnp.max(s, axis=0, keepdims=True))
        alpha = jnp.exp(m - m_new)
        pr = jnp.exp(s - m_new)
        l_new = alpha * l + jnp.sum(pr, axis=0, keepdims=True)
        acc[...] = alpha * acc[...] + _dot(vc, pr.astype(BF16))
        return m_new, l_new

    def chunk(kj, carry, masked):
        m1, l1, m2, l2 = carry
        start = pl.multiple_of(kj * tk, tk)
        kc = k_ref[pl.ds(start, tk), :]
        vc = vt_ref[0, kj]
        s1 = _dot_nt(kc, qm1)
        s2 = _dot_nt(kc, qm2)
        if masked:
            kidx = kj * tk + lax.broadcasted_iota(jnp.int32, (tk, tq), 0)
            qidx = qi * tq + lax.broadcasted_iota(jnp.int32, (tk, tq), 1)
            vis = kidx <= qidx
            s1 = jnp.where(vis, s1, NEG)
            s2 = jnp.where(vis, s2, NEG)
        m1, l1 = one_head(s1, m1, l1, acc1, vc)
        m2, l2 = one_head(s2, m2, l2, acc2, vc)
        return m1, l1, m2, l2

    init = (jnp.full((1, tq), NEG, F32), jnp.zeros((1, tq), F32),
            jnp.full((1, tq), NEG, F32), jnp.zeros((1, tq), F32))
    per_q = tq // tk
    carry = lax.fori_loop(0, qi * per_q, lambda kj, cr: chunk(kj, cr, False), init)
    for dj in range(per_q):
        carry = chunk(qi * per_q + dj, carry, True)
    _, l1, _, l2 = carry

    lp = lam_ref[...]
    lam = (jnp.exp(jnp.sum(lp[0:1] * lp[1:2], axis=-1, keepdims=True))
           - jnp.exp(jnp.sum(lp[2:3] * lp[3:4], axis=-1, keepdims=True)) + lambda_init)
    o = acc1[...] / l1 - lam * (acc2[...] / l2)
    o = o * lax.rsqrt(jnp.mean(o * o, axis=0, keepdims=True) + EPS)
    o = (o * g_ref[...]) * (1.0 - lambda_init)
    o_ref[...] = o.T.astype(BF16)


def _attention(q, k, vt, lam_rows, g_col, bsz, tq, tk, lambda_init):
    n, d = q.shape
    t = n // bsz
    nq = t // tq
    npairs = d // V_DIM
    return pl.pallas_call(
        functools.partial(_attn_kernel, tk=tk, lambda_init=lambda_init),
        out_shape=jax.ShapeDtypeStruct((n, d), BF16),
        grid=(bsz, npairs, nq),
        in_specs=[
            pl.BlockSpec((tq, GROUP), lambda b, p, i: (b * nq + i, p // 2)),
            pl.BlockSpec((t, GROUP), lambda b, p, i: (b, p // 2)),
            pl.BlockSpec((1, t // tk, V_DIM, tk), lambda b, p, i: (b, 0, p, 0)),
            pl.BlockSpec((4, HEAD_DIM), lambda b, p, i: (0, 0)),
            pl.BlockSpec((V_DIM, 1), lambda b, p, i: (0, 0)),
        ],
        out_specs=pl.BlockSpec((tq, V_DIM), lambda b, p, i: (b * nq + i, p)),
        scratch_shapes=[pltpu.VMEM((V_DIM, tq), F32), pltpu.VMEM((V_DIM, tq), F32)],
        compiler_params=_cparams(("arbitrary", "arbitrary", "arbitrary")),
        name="diff_attn",
    )(q, k, vt, lam_rows, g_col)


def _proj_res_kernel(a_ref, w_ref, x_ref, mod_ref, o_ref, *, gate_row):
    gate = mod_ref[0][gate_row:gate_row + 1]
    o_ref[...] = x_ref[...] + gate * _dot(a_ref[...], w_ref[...])


def _proj_res(a, w, x2, mod, bsz, tm, gate_row):
    n, d = x2.shape
    tiles_per_b = n // bsz // tm
    row = lambda i: (i, 0)
    return pl.pallas_call(
        functools.partial(_proj_res_kernel, gate_row=gate_row),
        out_shape=jax.ShapeDtypeStruct((n, d), F32),
        grid=(n // tm,),
        in_specs=[
            pl.BlockSpec((tm, a.shape[1]), row),
            pl.BlockSpec(w.shape, lambda i: (0, 0)),
            pl.BlockSpec((tm, d), row),
            pl.BlockSpec((1, 8, d), lambda i: (i // tiles_per_b, 0, 0)),
        ],
        out_specs=pl.BlockSpec((tm, d), row),
        compiler_params=_cparams(("arbitrary",)),
        name="proj_residual",
    )(a, w, x2, mod)


def _ffn_kernel(x_ref, mod_ref, g_ref, wg_ref, wu_ref, wd_ref, o_ref, *, tf):
    m = mod_ref[0]
    x = x_ref[...]
    h = _norm_mod(x, g_ref[...], m[3:4], m[4:5]).astype(BF16)
    f = wg_ref.shape[1]
    acc = None
    for ci in range(f // tf):
        sl = slice(ci * tf, (ci + 1) * tf)
        gt = _dot(h, wg_ref[:, sl])
        up = _dot(h, wu_ref[:, sl])
        a = (gt * _sigmoid(gt) * up).astype(BF16)
        part = _dot(a, wd_ref[sl, :])
        acc = part if acc is None else acc + part
    o_ref[...] = x + m[5:6] * acc


def _ffn(x2, mod, g, wg, wu, wd, bsz, tm, tf):
    n, d = x2.shape
    f = wg.shape[1]
    tiles_per_b = n // bsz // tm
    row = lambda i: (i, 0)
    const = lambda i: (0, 0)
    return pl.pallas_call(
        functools.partial(_ffn_kernel, tf=tf),
        out_shape=jax.ShapeDtypeStruct((n, d), F32),
        grid=(n // tm,),
        in_specs=[
            pl.BlockSpec((tm, d), row),
            pl.BlockSpec((1, 8, d), lambda i: (i // tiles_per_b, 0, 0)),
            pl.BlockSpec((1, d), const),
            pl.BlockSpec((d, f), const),
            pl.BlockSpec((d, f), const),
            pl.BlockSpec((f, d), const),
        ],
        out_specs=pl.BlockSpec((tm, d), row),
        compiler_params=_cparams(("arbitrary",)),
        name="dense_swiglu",
    )(x2, mod, g, wg, wu, wd)


def _pw1_kernel(x_ref, mod_ref, g_ref, w_ref, b_ref, o_ref, *, tn):
    m = mod_ref[0]
    h = _norm_mod(x_ref[...], g_ref[...], m[0:1], m[1:2]).astype(BF16)
    d = x_ref.shape[1]
    for ci in range(d // tn):
        a = _dot(h, w_ref[:, ci * tn:(ci + 1) * tn]) + b_ref[:, ci * tn:(ci + 1) * tn]
        gt = _dot(h, w_ref[:, d + ci * tn:d + (ci + 1) * tn]) + b_ref[:, d + ci * tn:d + (ci + 1) * tn]
        o_ref[:, ci * tn:(ci + 1) * tn] = a * _sigmoid(gt)


def _pw1(x2, mod, g, w, b, bsz, tm):
    n, d = x2.shape
    tiles_per_b = n // bsz // tm
    row = lambda i: (i, 0)
    const = lambda i: (0, 0)
    return pl.pallas_call(
        functools.partial(_pw1_kernel, tn=256),
        out_shape=jax.ShapeDtypeStruct((n, d), F32),
        grid=(n // tm,),
        in_specs=[
            pl.BlockSpec((tm, d), row),
            pl.BlockSpec((1, 8, d), lambda i: (i // tiles_per_b, 0, 0)),
            pl.BlockSpec((1, d), const),
            pl.BlockSpec((d, 2 * d), const),
            pl.BlockSpec((1, 2 * d), const),
        ],
        out_specs=pl.BlockSpec((tm, d), row),
        compiler_params=_cparams(("arbitrary",)),
        name="conv_pw1_glu",
    )(x2, mod, g, w, b)


def _dwconv_kernel(u_ref, halo_ref, w_ref, b_ref, o_ref, ext, *, tiles_per_b, rb):
    tm = u_ref.shape[0]
    first = (pl.program_id(0) % tiles_per_b) == 0
    ext[0:CONV_HALO, :] = jnp.where(first, 0.0, halo_ref[...])
    ext[CONV_HALO:, :] = u_ref[...]
    off = CONV_HALO - (CONV_WIDTH - 1)
    for r in range(tm // rb):
        acc = jnp.broadcast_to(b_ref[...], (rb, LANES))
        for kk in range(CONV_WIDTH):
            acc = acc + w_ref[kk:kk + 1, :] * ext[r * rb + off + kk:r * rb + off + kk + rb, :]
        o_ref[r * rb:(r + 1) * rb, :] = acc


def _dwconv(u, w, b, bsz, tm):
    n, d = u.shape
    tiles_per_b = n // bsz // tm
    hb = tm // CONV_HALO
    return pl.pallas_call(
        functools.partial(_dwconv_kernel, tiles_per_b=tiles_per_b, rb=64),
        out_shape=jax.ShapeDtypeStruct((n, d), F32),
        grid=(n // tm, d // LANES),
        in_specs=[
            pl.BlockSpec((tm, LANES), lambda i, j: (i, j)),
            pl.BlockSpec((CONV_HALO, LANES), lambda i, j: (jnp.maximum(i * hb - 1, 0), j)),
            pl.BlockSpec((CONV_WIDTH, LANES), lambda i, j: (0, j)),
            pl.BlockSpec((1, LANES), lambda i, j: (0, j)),
        ],
        out_specs=pl.BlockSpec((tm, LANES), lambda i, j: (i, j)),
        scratch_shapes=[pltpu.VMEM((tm + CONV_HALO, LANES), F32)],
        compiler_params=_cparams(("arbitrary", "arbitrary")),
        name="conv_depthwise",
    )(u, u, w, b)


def _pw2_kernel(v_ref, lg_ref, lb_ref, w_ref, b_ref, x_ref, mod_ref, o_ref):
    v = v_ref[...]
    mu = jnp.mean(v, axis=-1, keepdims=True)
    vc = v - mu
    var = jnp.mean(vc * vc, axis=-1, keepdims=True)
    y = vc * lax.rsqrt(var + EPS) * lg_ref[...] + lb_ref[...]
    y = (y * _sigmoid(y)).astype(BF16)
    gate = mod_ref[0][2:3]
    o_ref[...] = x_ref[...] + gate * (_dot(y, w_ref[...]) + b_ref[...])


def _pw2(v, ln_g, ln_b, w, b, x2, mod, bsz, tm):
    n, d = x2.shape
    tiles_per_b = n // bsz // tm
    row = lambda i: (i, 0)
    const = lambda i: (0, 0)
    return pl.pallas_call(
        _pw2_kernel,
        out_shape=jax.ShapeDtypeStruct((n, d), F32),
        grid=(n // tm,),
        in_specs=[
            pl.BlockSpec((tm, d), row),
            pl.BlockSpec((1, d), const),
            pl.BlockSpec((1, d), const),
            pl.BlockSpec((d, d), const),
            pl.BlockSpec((1, d), const),
            pl.BlockSpec((tm, d), row),
            pl.BlockSpec((1, 8, d), lambda i: (i // tiles_per_b, 0, 0)),
        ],
        out_specs=pl.BlockSpec((tm, d), row),
        compiler_params=_cparams(("arbitrary",)),
        name="conv_ln_pw2_residual",
    )(v, ln_g, ln_b, w, b, x2, mod)


def _router_kernel(x_ref, mod_ref, g_ref, wr_ref, h_ref, idx_ref, gate_ref):
    m = mod_ref[0]
    h = _norm_mod(x_ref[...], g_ref[...], m[3:4], m[4:5])
    h_ref[...] = h
    logits = lax.dot_general(wr_ref[...], h, (((1,), (1,)), ((), ())),
                             preferred_element_type=F32, precision=lax.Precision.HIGHEST)
    ne = logits.shape[0]
    eidx = lax.broadcasted_iota(jnp.int32, logits.shape, 0)
    v1 = jnp.max(logits, axis=0, keepdims=True)
    i1 = jnp.min(jnp.where(logits == v1, eidx, ne), axis=0, keepdims=True)
    rest = jnp.where(eidx == i1, -jnp.inf, logits)
    v2 = jnp.max(rest, axis=0, keepdims=True)
    i2 = jnp.min(jnp.where(rest == v2, eidx, ne), axis=0, keepdims=True)
    e2 = jnp.exp(v2 - v1)
    den = 1.0 + e2
    idx_ref[0:1, :] = i1
    idx_ref[1:2, :] = i2
    gate_ref[0:1, :] = 1.0 / den
    gate_ref[1:2, :] = e2 / den


def _router(x2, mod, g, wr_t, bsz, tm):
    n, d = x2.shape
    ne = wr_t.shape[0]
    tiles_per_b = n // bsz // tm
    row = lambda i: (i, 0)
    return pl.pallas_call(
        _router_kernel,
        out_shape=(jax.ShapeDtypeStruct((n, d), F32), jax.ShapeDtypeStruct((TOP_K, n), jnp.int32),
                   jax.ShapeDtypeStruct((TOP_K, n), F32)),
        grid=(n // tm,),
        in_specs=[
            pl.BlockSpec((tm, d), row),
            pl.BlockSpec((1, 8, d), lambda i: (i // tiles_per_b, 0, 0)),
            pl.BlockSpec((1, d), lambda i: (0, 0)),
            pl.BlockSpec((ne, d), lambda i: (0, 0)),
        ],
        out_specs=(pl.BlockSpec((tm, d), row), pl.BlockSpec((TOP_K, tm), lambda i: (0, i)),
                   pl.BlockSpec((TOP_K, tm), lambda i: (0, i))),
        compiler_params=_cparams(("arbitrary",)),
        name="moe_router",
    )(x2, mod, g, wr_t)


def _row_copy(src_hbm, dst, sem, src_row, dst_row):
    return pltpu.make_async_copy(src_hbm.at[pl.ds(src_row, 1)], dst.at[pl.ds(dst_row, 1)], sem)


def _moe_kernel(be_ref, na_ref, src_ref, h_hbm, gate_ref, wg_ref, wu_ref, wd_ref, o_ref,
                xbuf, xb16, acc, sem):
    tm = xbuf.shape[0]
    blk = pl.program_id(0)
    f = pl.program_id(1)
    nf = pl.num_programs(1)
    active = blk < na_ref[0]

    @pl.when(jnp.logical_and(active, f == 0))
    def _():
        base = blk * tm

        def issue(i, carry):
            _row_copy(h_hbm, xbuf, sem, src_ref[base + i], i).start()
            return carry

        lax.fori_loop(0, tm, issue, 0)

        def drain(i, carry):
            _row_copy(h_hbm, xbuf, sem, 0, i).wait()
            return carry

        lax.fori_loop(0, tm, drain, 0)
        xb16[...] = xbuf[...].astype(BF16)
        acc[...] = jnp.zeros_like(acc)

    @pl.when(active)
    def _():
        xb = xb16[...]
        gt = _dot(xb, wg_ref[0])
        up = _dot(xb, wu_ref[0])
        a = (gt * _sigmoid(gt) * up).astype(BF16)
        acc[...] += _dot(a, wd_ref[0])

    @pl.when(f == nf - 1)
    def _():
        o_ref[...] = jnp.where(active, acc[...] * gate_ref[...], 0.0)


def _moe(block_e, n_active, src_tok, h, gate_slot, wg, wu, wd, tm, tf):
    n, d = h.shape
    p = src_tok.shape[0]
    ne, _, fe = wg.shape
    nf = fe // tf
    nblk = p // tm

    def wsel(blk, f, be, na, src):
        return jnp.where(blk < na[0], f, nf - 1)

    grid_spec = pltpu.PrefetchScalarGridSpec(
        num_scalar_prefetch=3,
        grid=(nblk, nf),
        in_specs=[
            pl.BlockSpec(memory_space=pl.ANY),
            pl.BlockSpec((tm, 1), lambda blk, f, be, na, src: (blk, 0)),
            pl.BlockSpec((1, d, tf), lambda blk, f, be, na, src: (be[blk], 0, wsel(blk, f, be, na, src))),
            pl.BlockSpec((1, d, tf), lambda blk, f, be, na, src: (be[blk], 0, wsel(blk, f, be, na, src))),
            pl.BlockSpec((1, tf, d), lambda blk, f, be, na, src: (be[blk], wsel(blk, f, be, na, src), 0)),
        ],
        out_specs=pl.BlockSpec((tm, d), lambda blk, f, be, na, src: (blk, 0)),
        scratch_shapes=[pltpu.VMEM((tm, d), F32), pltpu.VMEM((tm, d), BF16), pltpu.VMEM((tm, d), F32),
                        pltpu.SemaphoreType.DMA],
    )
    return pl.pallas_call(
        _moe_kernel,
        out_shape=jax.ShapeDtypeStruct((p, d), F32),
        grid_spec=grid_spec,
        compiler_params=_cparams(("arbitrary", "arbitrary")),
        name="moe_experts",
    )(block_e, n_active, src_tok, h, gate_slot, wg, wu, wd)


def _combine_kernel(pos_ref, y_hbm, x_ref, mod_ref, g_ref, o_ref, buf0, buf1, sem):
    tm = x_ref.shape[0]
    n = pl.num_programs(0) * tm
    base = pl.program_id(0) * tm

    def issue(i, carry):
        _row_copy(y_hbm, buf0, sem, pos_ref[base + i], i).start()
        _row_copy(y_hbm, buf1, sem, pos_ref[n + base + i], i).start()
        return carry

    lax.fori_loop(0, tm, issue, 0)

    def drain(i, carry):
        _row_copy(y_hbm, buf0, sem, 0, i).wait()
        _row_copy(y_hbm, buf1, sem, 0, i).wait()
        return carry

    lax.fori_loop(0, tm, drain, 0)
    gate = mod_ref[0][5:6]
    x = x_ref[...] + gate * (buf0[...] + buf1[...])
    o_ref[...] = x * lax.rsqrt(jnp.mean(x * x, axis=-1, keepdims=True) + EPS) * g_ref[...]


def _combine(pos_flat, y_pad, x2, mod, final_g, bsz, tm):
    n, d = x2.shape
    tiles_per_b = n // bsz // tm
    grid_spec = pltpu.PrefetchScalarGridSpec(
        num_scalar_prefetch=1,
        grid=(n // tm,),
        in_specs=[
            pl.BlockSpec(memory_space=pl.ANY),
            pl.BlockSpec((tm, d), lambda i, pos: (i, 0)),
            pl.BlockSpec((1, 8, d), lambda i, pos: (i // tiles_per_b, 0, 0)),
            pl.BlockSpec((1, d), lambda i, pos: (0, 0)),
        ],
        out_specs=pl.BlockSpec((tm, d), lambda i, pos: (i, 0)),
        scratch_shapes=[pltpu.VMEM((tm, d), F32), pltpu.VMEM((tm, d), F32), pltpu.SemaphoreType.DMA],
    )
    return pl.pallas_call(
        _combine_kernel,
        out_shape=jax.ShapeDtypeStruct((n, d), F32),
        grid_spec=grid_spec,
        compiler_params=_cparams(("arbitrary",)),
        name="moe_combine_final_norm",
    )(pos_flat, y_pad, x2, mod, final_g)


def _routing(idx, gates, ne, tm):
    n = idx.shape[1]
    a_tot = n * TOP_K
    flat_e = idx.T.reshape(-1)
    flat_g = gates.T.reshape(-1)
    onehot = (flat_e[:, None] == jnp.arange(ne, dtype=jnp.int32)[None, :]).astype(jnp.int32)
    csum = jnp.cumsum(onehot, axis=0)
    rank = jnp.sum((csum - onehot) * onehot, axis=1)
    sizes = csum[-1]
    padded = (sizes + tm - 1) // tm * tm
    pad_end = jnp.cumsum(padded)
    pad_start = pad_end - padded
    dest = (jnp.sum(pad_start[None, :] * onehot, axis=1) + rank).astype(jnp.int32)
    p = a_tot + ne * tm
    nblk = p // tm
    src_tok = jnp.zeros((p,), jnp.int32).at[dest].set(jnp.arange(a_tot, dtype=jnp.int32) // TOP_K)
    gate_slot = jnp.zeros((p,), F32).at[dest].set(flat_g)
    n_active = (pad_end[-1] // tm).astype(jnp.int32)
    blk_start = jnp.arange(nblk, dtype=jnp.int32) * tm
    block_e = jnp.sum((blk_start[:, None] >= pad_end[None, :]).astype(jnp.int32), axis=1)
    last_e = jnp.sum((jnp.maximum(pad_end[-1] - 1, 0) >= pad_end).astype(jnp.int32))
    block_e = jnp.where(jnp.arange(nblk) < n_active, block_e, last_e).astype(jnp.int32)
    pos_flat = dest.reshape(n, TOP_K).T.reshape(-1)
    return block_e, n_active.reshape(1), src_tok, gate_slot.reshape(p, 1), pos_flat


def _rope_perm(d):
    perm = np.zeros((d,), np.int32)
    for h in range(d // HEAD_DIM):
        gi, r = divmod(h, 4)
        for dd in range(HEAD_DIM):
            new = gi * GROUP + (dd // HALF) * LANES + r * HALF + dd % HALF
            perm[new] = h * HEAD_DIM + dd
    return perm


def _mod_rows(mod_l, bsz, d):
    m = mod_l[:bsz].reshape(bsz, 6, d)
    return jnp.pad(m, ((0, 0), (0, 2), (0, 0)))


@jax.jit
def kernel(x, c, positions, w_ada, b_ada, norm_mix_g, norm_ffn_g, attn_w_qkv, attn_w_o, lam_q1, lam_k1,
           lam_q2, lam_k2, attn_subln_g, conv_w_pw1, conv_b_pw1, conv_w_dw, conv_b_dw, conv_ln_g,
           conv_ln_b, conv_w_pw2, conv_b_pw2, ffn_w_gate, ffn_w_up, ffn_w_down, moe_w_router,
           moe_w_gate, moe_w_up, moe_w_down, final_g):
    bsz, t, d = x.shape
    n = bsz * t
    depth = w_ada.shape[0]
    assert depth == 2 and d % GROUP == 0 and bsz <= 8
    tm = min(512, t)
    tk = min(256, t)
    x2 = x.reshape(n, d)

    c8 = jnp.pad(c, ((0, 8 - bsz), (0, 0)))
    mod = _ada(c8, w_ada, b_ada)
    mod0 = _mod_rows(mod[0], bsz, d)
    mod1 = _mod_rows(mod[1], bsz, d)

    inv_freq = ROPE_THETA ** (-jnp.arange(HALF, dtype=F32) / HALF)
    cos, sin = _rope_table(positions.reshape(n, 1), jnp.tile(inv_freq, LANES // HALF).reshape(1, LANES), tm)
    perm = _rope_perm(d)
    w_qkv = attn_w_qkv[0]
    wq = w_qkv[:, perm].astype(BF16)
    wk = w_qkv[:, d + perm].astype(BF16)
    wvt = w_qkv[:, 2 * d:].T.astype(BF16)
    q, k, vt = _qkv(x2, mod0, norm_mix_g[0:1], cos, sin, wq, wk, wvt, bsz, tm, tk)
    lam_rows = jnp.concatenate([lam_q1[0:1], lam_k1[0:1], lam_q2[0:1], lam_k2[0:1]], axis=0)
    lambda_init = 0.8 - 0.6 * math.exp(-0.3 * 0)
    o = _attention(q, k, vt, lam_rows, attn_subln_g[0].reshape(V_DIM, 1), bsz, tm, tk, lambda_init)
    x2 = _proj_res(o, attn_w_o[0].astype(BF16), x2, mod0, bsz, tm, gate_row=2)
    x2 = _ffn(x2, mod0, norm_ffn_g[0:1], ffn_w_gate[0].astype(BF16), ffn_w_up[0].astype(BF16),
              ffn_w_down[0].astype(BF16), bsz, tm, tf=256)

    u = _pw1(x2, mod1, norm_mix_g[1:2], conv_w_pw1[0].astype(BF16), conv_b_pw1[0:1], bsz, tm)
    v = _dwconv(u, conv_w_dw[0], conv_b_dw[0:1], bsz, tm)
    x2 = _pw2(v, conv_ln_g[0:1], conv_ln_b[0:1], conv_w_pw2[0].astype(BF16), conv_b_pw2[0:1], x2, mod1,
              bsz, tm)
    h, idx, gates = _router(x2, mod1, norm_ffn_g[1:2], moe_w_router[0].T, bsz, tm)
    ne = moe_w_router.shape[-1]
    fe = moe_w_gate.shape[-1]
    block_e, n_active, src_tok, gate_slot, pos_flat = _routing(idx, gates, ne, tm)
    tf_e = 896 if fe % 896 == 0 else fe
    y_pad = _moe(block_e, n_active, src_tok, h, gate_slot, moe_w_gate[0].astype(BF16),
                 moe_w_up[0].astype(BF16), moe_w_down[0].astype(BF16), tm, tf_e)
    out = _combine(pos_flat, y_pad, x2, mod1, final_g.reshape(1, d), bsz, tm)
    return out.reshape(bsz, t, d)
```

```python
import functools
import math

import numpy as np
import jax
import jax.numpy as jnp
from jax import lax
from jax.experimental import pallas as pl
from jax.experimental.pallas import tpu as pltpu

EPS = 1e-6
ROPE_THETA = 10000.0
HEAD_DIM = 64
V_DIM = 128
V_ROWS = V_DIM + 16
HALF = HEAD_DIM // 2
LANES = 128
GROUP = 4 * HEAD_DIM
CONV_WIDTH = 31
CONV_HALO = 32
TOP_K = 2
NEG = -1e30
VMEM_LIMIT = 56 * 1024 * 1024

BF16 = jnp.bfloat16
F32 = jnp.float32


def _cparams(sem):
    return pltpu.CompilerParams(dimension_semantics=sem, vmem_limit_bytes=VMEM_LIMIT)


def _dot(a, b):
    return jnp.dot(a, b, preferred_element_type=F32)


def _dot_nt(a, b):
    return lax.dot_general(a, b, (((1,), (1,)), ((), ())), preferred_element_type=F32)


def _norm_mod(x, g, shift, scale):
    y = x * lax.rsqrt(jnp.mean(x * x, axis=-1, keepdims=True) + EPS)
    return (y * g) * (1.0 + scale) + shift


def _sigmoid(x):
    return 1.0 / (1.0 + jnp.exp(-x))


def _ada_kernel(c_ref, w_ref, b_ref, o_ref):
    c = c_ref[...]
    ca = (c * _sigmoid(c)).astype(BF16)
    o_ref[0] = _dot(ca, w_ref[0].astype(BF16)) + b_ref[0]


def _ada(c8, w_ada, b_ada):
    depth, d, d6 = w_ada.shape
    tn = 1536
    return pl.pallas_call(
        _ada_kernel,
        out_shape=jax.ShapeDtypeStruct((depth, 8, d6), F32),
        grid=(depth, d6 // tn),
        in_specs=[
            pl.BlockSpec((8, d), lambda l, j: (0, 0)),
            pl.BlockSpec((1, d, tn), lambda l, j: (l, 0, j)),
            pl.BlockSpec((1, 1, tn), lambda l, j: (l, 0, j)),
        ],
        out_specs=pl.BlockSpec((1, 8, tn), lambda l, j: (l, 0, j)),
        compiler_params=_cparams(("arbitrary", "arbitrary")),
        name="ada_mod",
    )(c8, w_ada, b_ada.reshape(depth, 1, d6))


def _rope_table_kernel(pos_ref, freq_ref, cos_ref, sin_ref):
    ang = pos_ref[...].astype(F32) * freq_ref[...]
    cos_ref[...] = jnp.cos(ang)
    sin_ref[...] = jnp.sin(ang)


def _rope_table(pos_col, freq_row, tm):
    n = pos_col.shape[0]
    return pl.pallas_call(
        _rope_table_kernel,
        out_shape=(jax.ShapeDtypeStruct((n, LANES), F32), jax.ShapeDtypeStruct((n, LANES), F32)),
        grid=(n // tm,),
        in_specs=[pl.BlockSpec((tm, 1), lambda i: (i, 0)), pl.BlockSpec((1, LANES), lambda i: (0, 0))],
        out_specs=(pl.BlockSpec((tm, LANES), lambda i: (i, 0)), pl.BlockSpec((tm, LANES), lambda i: (i, 0))),
        compiler_params=_cparams(("arbitrary",)),
        name="rope_table",
    )(pos_col, freq_row)


def _qkv_kernel(x_ref, mod_ref, g_ref, cos_ref, sin_ref, pos_ref, freq_ref, wqt_ref, wk_ref, wvt_ref,
                qt_ref, k_ref, vt_ref, *, tk):
    m = mod_ref[0]
    h = _norm_mod(x_ref[...], g_ref[...], m[0:1], m[1:2]).astype(BF16)
    d = x_ref.shape[1]
    tm = x_ref.shape[0]
    c = cos_ref[...]
    s = sin_ref[...]
    for gi in range(d // GROUP):
        y = _dot(h, wk_ref[:, gi * GROUP:(gi + 1) * GROUP])
        a = y[:, :LANES]
        b = y[:, LANES:]
        k_ref[:, gi * GROUP:gi * GROUP + LANES] = (a * c - b * s).astype(BF16)
        k_ref[:, gi * GROUP + LANES:(gi + 1) * GROUP] = (b * c + a * s).astype(BF16)
    ang = freq_ref[...] * pos_ref[0].astype(F32)
    reps = LANES // HALF
    ct = jnp.concatenate([jnp.cos(ang)] * reps, axis=0)
    st = jnp.concatenate([jnp.sin(ang)] * reps, axis=0)
    mult = HEAD_DIM ** -0.5 * math.log2(math.e)
    for gi in range(d // GROUP):
        y = _dot_nt(wqt_ref[gi * GROUP:(gi + 1) * GROUP, :], h)
        a = y[:LANES]
        b = y[LANES:]
        qt_ref[0, gi * GROUP:gi * GROUP + LANES, :] = ((a * ct - b * st) * mult).astype(BF16)
        qt_ref[0, gi * GROUP + LANES:(gi + 1) * GROUP, :] = ((b * ct + a * st) * mult).astype(BF16)
    ones = jnp.ones((V_ROWS - V_DIM, tk), BF16)
    for hh in range(d // V_DIM):
        vt = _dot_nt(wvt_ref[hh * V_DIM:(hh + 1) * V_DIM, :], h)
        for ci in range(tm // tk):
            vt_ref[0, ci, hh, 0:V_DIM, :] = vt[:, ci * tk:(ci + 1) * tk].astype(BF16)
            vt_ref[0, ci, hh, V_DIM:V_ROWS, :] = ones


def _qkv(x2, mod, g, cos, sin, pos_row, freq_col, wqt, wk, wvt, bsz, tm, tk):
    n, d = x2.shape
    t = n // bsz
    tiles_per_b = t // tm
    nh = d // V_DIM
    row = lambda i: (i, 0)
    const = lambda i: (0, 0)
    return pl.pallas_call(
        functools.partial(_qkv_kernel, tk=tk),
        out_shape=(jax.ShapeDtypeStruct((bsz, d, t), BF16), jax.ShapeDtypeStruct((n, d), BF16),
                   jax.ShapeDtypeStruct((bsz, t // tk, nh, V_ROWS, tk), BF16)),
        grid=(n // tm,),
        in_specs=[
            pl.BlockSpec((tm, d), row),
            pl.BlockSpec((1, 8, d), lambda i: (i // tiles_per_b, 0, 0)),
            pl.BlockSpec((1, d), const),
            pl.BlockSpec((tm, LANES), row),
            pl.BlockSpec((tm, LANES), row),
            pl.BlockSpec((1, 1, tm), lambda i: (i, 0, 0)),
            pl.BlockSpec((HALF, 1), const),
            pl.BlockSpec((d, d), const),
            pl.BlockSpec((d, d), const),
            pl.BlockSpec((d, d), const),
        ],
        out_specs=(pl.BlockSpec((1, d, tm), lambda i: (i // tiles_per_b, 0, i % tiles_per_b)),
                   pl.BlockSpec((tm, d), row),
                   pl.BlockSpec((1, tm // tk, nh, V_ROWS, tk),
                                lambda i: (i // tiles_per_b, i % tiles_per_b, 0, 0, 0))),
        compiler_params=_cparams(("arbitrary",)),
        name="qkv_rope",
    )(x2, mod, g, cos, sin, pos_row, freq_col, wqt, wk, wvt)


def _attn_kernel(qt_ref, k_ref, vt_ref, lam_ref, g_ref, o_ref, s_a, s_b, acc, *, tk, lambda_init):
    tq = qt_ref.shape[2]
    p = pl.program_id(1)
    qi = pl.program_id(2)
    n_full = qi * (tq // tk)
    row = lax.broadcasted_iota(jnp.int32, (GROUP, 1), 0)
    head = (row % LANES) // HALF
    r1 = 2 * (p % 2)
    qf = qt_ref[0].astype(F32)
    qm = (jnp.where(head == r1, qf, 0.0).astype(BF16), jnp.where(head == r1 + 1, qf, 0.0).astype(BF16))

    def scores(kj, s_ref, masked):
        kc = k_ref[pl.ds(pl.multiple_of(kj * tk, tk), tk), :]
        for hd in range(2):
            s = _dot(kc, qm[hd])
            if masked:
                kidx = kj * tk + lax.broadcasted_iota(jnp.int32, (tk, tq), 0)
                qidx = qi * tq + lax.broadcasted_iota(jnp.int32, (tk, tq), 1)
                s = jnp.where(kidx <= qidx, s, NEG)
            s_ref[hd] = s

    def softmax_pv(kj, s_ref, ms, first):
        vc = vt_ref[0, kj, 0]
        out = []
        for hd in range(2):
            s = s_ref[hd]
            mc = jnp.max(s, axis=0, keepdims=True)
            m_new = mc if first else jnp.maximum(ms[hd], mc)
            pv = _dot(vc, jnp.exp2(s - m_new).astype(BF16))
            if first:
                acc[hd] = pv
            else:
                acc[hd] = jnp.exp2(ms[hd] - m_new) * acc[hd] + pv
            out.append(m_new)
        return tuple(out)

    assert tq == 2 * tk
    scores(n_full, s_a, True)
    scores(n_full + 1, s_b, True)
    ms = softmax_pv(n_full, s_a, None, True)
    scores(0, s_a, False)
    ms = softmax_pv(n_full + 1, s_b, ms, False)

    def body(jj, ms):
        j = 2 * jj
        scores(j + 1, s_b, False)
        ms = softmax_pv(j, s_a, ms, False)
        scores(j + 2, s_a, False)
        return softmax_pv(j + 1, s_b, ms, False)

    ms = lax.fori_loop(0, n_full // 2 - 1, body, ms)

    @pl.when(n_full > 0)
    def _():
        scores(n_full - 1, s_b, False)
        ms2 = softmax_pv(n_full - 2, s_a, ms, False)
        softmax_pv(n_full - 1, s_b, ms2, False)

    lp = lam_ref[...]
    lam = (jnp.exp(jnp.sum(lp[0:1] * lp[1:2], axis=-1, keepdims=True))
           - jnp.exp(jnp.sum(lp[2:3] * lp[3:4], axis=-1, keepdims=True)) + lambda_init)
    a1 = acc[0]
    a2 = acc[1]
    o = a1[:V_DIM] / a1[V_DIM:V_DIM + 1] - lam * (a2[:V_DIM] / a2[V_DIM:V_DIM + 1])
    o = o * lax.rsqrt(jnp.mean(o * o, axis=0, keepdims=True) + EPS)
    o = (o * g_ref[...]) * (1.0 - lambda_init)
    o_ref[...] = o.T.astype(BF16)


def _attention(qt, k, vt, lam_rows, g_col, tq, tk, lambda_init):
    bsz, d, t = qt.shape
    n = bsz * t
    nq = t // tq
    npairs = d // V_DIM
    return pl.pallas_call(
        functools.partial(_attn_kernel, tk=tk, lambda_init=lambda_init),
        out_shape=jax.ShapeDtypeStruct((n, d), BF16),
        grid=(bsz, npairs, nq),
        in_specs=[
            pl.BlockSpec((1, GROUP, tq), lambda b, p, i: (b, p // 2, i)),
            pl.BlockSpec((t, GROUP), lambda b, p, i: (b, p // 2)),
            pl.BlockSpec((1, t // tk, 1, V_ROWS, tk), lambda b, p, i: (b, 0, p, 0, 0)),
            pl.BlockSpec((4, HEAD_DIM), lambda b, p, i: (0, 0)),
            pl.BlockSpec((V_DIM, 1), lambda b, p, i: (0, 0)),
        ],
        out_specs=pl.BlockSpec((tq, V_DIM), lambda b, p, i: (b * nq + i, p)),
        scratch_shapes=[pltpu.VMEM((2, tk, tq), F32), pltpu.VMEM((2, tk, tq), F32),
                        pltpu.VMEM((2, V_ROWS, tq), F32)],
        compiler_params=_cparams(("arbitrary", "arbitrary", "arbitrary")),
        name="diff_attn",
    )(qt, k, vt, lam_rows, g_col)


def _proj_res_kernel(a_ref, w_ref, x_ref, mod_ref, o_ref, *, gate_row):
    gate = mod_ref[0][gate_row:gate_row + 1]
    o_ref[...] = x_ref[...] + gate * _dot(a_ref[...], w_ref[...])


def _proj_res(a, w, x2, mod, bsz, tm, gate_row):
    n, d = x2.shape
    tiles_per_b = n // bsz // tm
    row = lambda i: (i, 0)
    return pl.pallas_call(
        functools.partial(_proj_res_kernel, gate_row=gate_row),
        out_shape=jax.ShapeDtypeStruct((n, d), F32),
        grid=(n // tm,),
        in_specs=[
            pl.BlockSpec((tm, a.shape[1]), row),
            pl.BlockSpec(w.shape, lambda i: (0, 0)),
            pl.BlockSpec((tm, d), row),
            pl.BlockSpec((1, 8, d), lambda i: (i // tiles_per_b, 0, 0)),
        ],
        out_specs=pl.BlockSpec((tm, d), row),
        compiler_params=_cparams(("arbitrary",)),
        name="proj_residual",
    )(a, w, x2, mod)


def _ffn_kernel(x_ref, mod_ref, g_ref, wg_ref, wu_ref, wd_ref, o_ref, *, tf):
    m = mod_ref[0]
    x = x_ref[...]
    h = _norm_mod(x, g_ref[...], m[3:4], m[4:5]).astype(BF16)
    f = wg_ref.shape[1]
    acc = None
    for ci in range(f // tf):
        sl = slice(ci * tf, (ci + 1) * tf)
        gt = _dot(h, wg_ref[:, sl])
        up = _dot(h, wu_ref[:, sl])
        a = (gt * _sigmoid(gt) * up).astype(BF16)
        part = _dot(a, wd_ref[sl, :])
        acc = part if acc is None else acc + part
    o_ref[...] = x + m[5:6] * acc


def _ffn(x2, mod, g, wg, wu, wd, bsz, tm, tf):
    n, d = x2.shape
    f = wg.shape[1]
    tiles_per_b = n // bsz // tm
    row = lambda i: (i, 0)
    const = lambda i: (0, 0)
    return pl.pallas_call(
        functools.partial(_ffn_kernel, tf=tf),
        out_shape=jax.ShapeDtypeStruct((n, d), F32),
        grid=(n // tm,),
        in_specs=[
            pl.BlockSpec((tm, d), row),
            pl.BlockSpec((1, 8, d), lambda i: (i // tiles_per_b, 0, 0)),
            pl.BlockSpec((1, d), const),
            pl.BlockSpec((d, f), const),
            pl.BlockSpec((d, f), const),
            pl.BlockSpec((f, d), const),
        ],
        out_specs=pl.BlockSpec((tm, d), row),
        compiler_params=_cparams(("arbitrary",)),
        name="dense_swiglu",
    )(x2, mod, g, wg, wu, wd)


def _pw1_kernel(x_ref, mod_ref, g_ref, w_ref, b_ref, o_ref, *, tn):
    m = mod_ref[0]
    h = _norm_mod(x_ref[...], g_ref[...], m[0:1], m[1:2]).astype(BF16)
    d = x_ref.shape[1]
    for ci in range(d // tn):
        a = _dot(h, w_ref[:, ci * tn:(ci + 1) * tn]) + b_ref[:, ci * tn:(ci + 1) * tn]
        gt = _dot(h, w_ref[:, d + ci * tn:d + (ci + 1) * tn]) + b_ref[:, d + ci * tn:d + (ci + 1) * tn]
        o_ref[:, ci * tn:(ci + 1) * tn] = a * _sigmoid(gt)


def _pw1(x2, mod, g, w, b, bsz, tm):
    n, d = x2.shape
    tiles_per_b = n // bsz // tm
    row = lambda i: (i, 0)
    const = lambda i: (0, 0)
    return pl.pallas_call(
        functools.partial(_pw1_kernel, tn=256),
        out_shape=jax.ShapeDtypeStruct((n, d), F32),
        grid=(n // tm,),
        in_specs=[
            pl.BlockSpec((tm, d), row),
            pl.BlockSpec((1, 8, d), lambda i: (i // tiles_per_b, 0, 0)),
            pl.BlockSpec((1, d), const),
            pl.BlockSpec((d, 2 * d), const),
            pl.BlockSpec((1, 2 * d), const),
        ],
        out_specs=pl.BlockSpec((tm, d), row),
        compiler_params=_cparams(("arbitrary",)),
        name="conv_pw1_glu",
    )(x2, mod, g, w, b)


def _dwconv_kernel(u_ref, halo_ref, w_ref, b_ref, o_ref, ext, *, tiles_per_b, rb):
    tm = u_ref.shape[0]
    first = (pl.program_id(0) % tiles_per_b) == 0
    ext[0:CONV_HALO, :] = jnp.where(first, 0.0, halo_ref[...])
    ext[CONV_HALO:, :] = u_ref[...]
    off = CONV_HALO - (CONV_WIDTH - 1)
    for r in range(tm // rb):
        acc = jnp.broadcast_to(b_ref[...], (rb, LANES))
        for kk in range(CONV_WIDTH):
            acc = acc + w_ref[kk:kk + 1, :] * ext[r * rb + off + kk:r * rb + off + kk + rb, :]
        o_ref[r * rb:(r + 1) * rb, :] = acc


def _dwconv(u, w, b, bsz, tm):
    n, d = u.shape
    tiles_per_b = n // bsz // tm
    hb = tm // CONV_HALO
    return pl.pallas_call(
        functools.partial(_dwconv_kernel, tiles_per_b=tiles_per_b, rb=64),
        out_shape=jax.ShapeDtypeStruct((n, d), F32),
        grid=(n // tm, d // LANES),
        in_specs=[
            pl.BlockSpec((tm, LANES), lambda i, j: (i, j)),
            pl.BlockSpec((CONV_HALO, LANES), lambda i, j: (jnp.maximum(i * hb - 1, 0), j)),
            pl.BlockSpec((CONV_WIDTH, LANES), lambda i, j: (0, j)),
            pl.BlockSpec((1, LANES), lambda i, j: (0, j)),
        ],
        out_specs=pl.BlockSpec((tm, LANES), lambda i, j: (i, j)),
        scratch_shapes=[pltpu.VMEM((tm + CONV_HALO, LANES), F32)],
        compiler_params=_cparams(("arbitrary", "arbitrary")),
        name="conv_depthwise",
    )(u, u, w, b)


def _pw2_kernel(v_ref, lg_ref, lb_ref, w_ref, b_ref, x_ref, mod_ref, o_ref):
    v = v_ref[...]
    mu = jnp.mean(v, axis=-1, keepdims=True)
    vc = v - mu
    var = jnp.mean(vc * vc, axis=-1, keepdims=True)
    y = vc * lax.rsqrt(var + EPS) * lg_ref[...] + lb_ref[...]
    y = (y * _sigmoid(y)).astype(BF16)
    gate = mod_ref[0][2:3]
    o_ref[...] = x_ref[...] + gate * (_dot(y, w_ref[...]) + b_ref[...])


def _pw2(v, ln_g, ln_b, w, b, x2, mod, bsz, tm):
    n, d = x2.shape
    tiles_per_b = n // bsz // tm
    row = lambda i: (i, 0)
    const = lambda i: (0, 0)
    return pl.pallas_call(
        _pw2_kernel,
        out_shape=jax.ShapeDtypeStruct((n, d), F32),
        grid=(n // tm,),
        in_specs=[
            pl.BlockSpec((tm, d), row),
            pl.BlockSpec((1, d), const),
            pl.BlockSpec((1, d), const),
            pl.BlockSpec((d, d), const),
            pl.BlockSpec((1, d), const),
            pl.BlockSpec((tm, d), row),
            pl.BlockSpec((1, 8, d), lambda i: (i // tiles_per_b, 0, 0)),
        ],
        out_specs=pl.BlockSpec((tm, d), row),
        compiler_params=_cparams(("arbitrary",)),
        name="conv_ln_pw2_residual",
    )(v, ln_g, ln_b, w, b, x2, mod)


def _router_kernel(x_ref, mod_ref, g_ref, wr_ref, h_ref, idx_ref, gate_ref):
    m = mod_ref[0]
    h = _norm_mod(x_ref[...], g_ref[...], m[3:4], m[4:5])
    h_ref[...] = h
    logits = lax.dot_general(wr_ref[...], h, (((1,), (1,)), ((), ())),
                             preferred_element_type=F32, precision=lax.Precision.HIGHEST)
    ne = logits.shape[0]
    eidx = lax.broadcasted_iota(jnp.int32, logits.shape, 0)
    v1 = jnp.max(logits, axis=0, keepdims=True)
    i1 = jnp.min(jnp.where(logits == v1, eidx, ne), axis=0, keepdims=True)
    rest = jnp.where(eidx == i1, -jnp.inf, logits)
    v2 = jnp.max(rest, axis=0, keepdims=True)
    i2 = jnp.min(jnp.where(rest == v2, eidx, ne), axis=0, keepdims=True)
    e2 = jnp.exp(v2 - v1)
    den = 1.0 + e2
    idx_ref[0:1, :] = i1
    idx_ref[1:2, :] = i2
    gate_ref[0:1, :] = 1.0 / den
    gate_ref[1:2, :] = e2 / den


def _router(x2, mod, g, wr_t, bsz, tm):
    n, d = x2.shape
    ne = wr_t.shape[0]
    tiles_per_b = n // bsz // tm
    row = lambda i: (i, 0)
    return pl.pallas_call(
        _router_kernel,
        out_shape=(jax.ShapeDtypeStruct((n, d), F32), jax.ShapeDtypeStruct((TOP_K, n), jnp.int32),
                   jax.ShapeDtypeStruct((TOP_K, n), F32)),
        grid=(n // tm,),
        in_specs=[
            pl.BlockSpec((tm, d), row),
            pl.BlockSpec((1, 8, d), lambda i: (i // tiles_per_b, 0, 0)),
            pl.BlockSpec((1, d), lambda i: (0, 0)),
            pl.BlockSpec((ne, d), lambda i: (0, 0)),
        ],
        out_specs=(pl.BlockSpec((tm, d), row), pl.BlockSpec((TOP_K, tm), lambda i: (0, i)),
                   pl.BlockSpec((TOP_K, tm), lambda i: (0, i))),
        compiler_params=_cparams(("arbitrary",)),
        name="moe_router",
    )(x2, mod, g, wr_t)


def _row_copy(src_hbm, dst, sem, src_row, dst_row):
    return pltpu.make_async_copy(src_hbm.at[pl.ds(src_row, 1)], dst.at[pl.ds(dst_row, 1)], sem)


def _moe_kernel(be_ref, na_ref, src_ref, h_hbm, wg_ref, wu_ref, wd_ref, o_ref,
                xbuf, xb16, acc, sem):
    tm = xbuf.shape[0]
    blk = pl.program_id(0)
    f = pl.program_id(1)
    nf = pl.num_programs(1)
    active = blk < na_ref[0]

    @pl.when(jnp.logical_and(active, f == 0))
    def _():
        base = blk * tm

        def issue(i, carry):
            _row_copy(h_hbm, xbuf, sem, src_ref[base + i], i).start()
            return carry

        lax.fori_loop(0, tm, issue, 0)

        def drain(i, carry):
            _row_copy(h_hbm, xbuf, sem, 0, i).wait()
            return carry

        lax.fori_loop(0, tm, drain, 0)
        xb16[...] = xbuf[...].astype(BF16)
        acc[...] = jnp.zeros_like(acc)

    @pl.when(active)
    def _():
        xb = xb16[...]
        gt = _dot(xb, wg_ref[0])
        up = _dot(xb, wu_ref[0])
        a = (gt * _sigmoid(gt) * up).astype(BF16)
        acc[...] += _dot(a, wd_ref[0])

    @pl.when(f == nf - 1)
    def _():
        o_ref[...] = jnp.where(active, acc[...], 0.0)


def _moe(block_e, n_active, src_tok, h, wg, wu, wd, tm, tf):
    n, d = h.shape
    p = src_tok.shape[0]
    ne, _, fe = wg.shape
    nf = fe // tf
    nblk = p // tm

    def wsel(blk, f, be, na, src):
        return jnp.where(blk < na[0], f, nf - 1)

    grid_spec = pltpu.PrefetchScalarGridSpec(
        num_scalar_prefetch=3,
        grid=(nblk, nf),
        in_specs=[
            pl.BlockSpec(memory_space=pl.ANY),
            pl.BlockSpec((1, d, tf), lambda blk, f, be, na, src: (be[blk], 0, wsel(blk, f, be, na, src))),
            pl.BlockSpec((1, d, tf), lambda blk, f, be, na, src: (be[blk], 0, wsel(blk, f, be, na, src))),
            pl.BlockSpec((1, tf, d), lambda blk, f, be, na, src: (be[blk], wsel(blk, f, be, na, src), 0)),
        ],
        out_specs=pl.BlockSpec((tm, d), lambda blk, f, be, na, src: (blk, 0)),
        scratch_shapes=[pltpu.VMEM((tm, d), F32), pltpu.VMEM((tm, d), BF16), pltpu.VMEM((tm, d), F32),
                        pltpu.SemaphoreType.DMA],
    )
    return pl.pallas_call(
        _moe_kernel,
        out_shape=jax.ShapeDtypeStruct((p, d), F32),
        grid_spec=grid_spec,
        compiler_params=_cparams(("arbitrary", "arbitrary")),
        name="moe_experts",
    )(block_e, n_active, src_tok, h, wg, wu, wd)


def _combine_kernel(pos_ref, y_hbm, x_ref, gates_ref, mod_ref, g_ref, o_ref, buf0, buf1, sem):
    tm = x_ref.shape[0]
    n = pl.num_programs(0) * tm
    base = pl.program_id(0) * tm

    def issue(i, carry):
        _row_copy(y_hbm, buf0, sem, pos_ref[base + i], i).start()
        _row_copy(y_hbm, buf1, sem, pos_ref[n + base + i], i).start()
        return carry

    lax.fori_loop(0, tm, issue, 0)

    def drain(i, carry):
        _row_copy(y_hbm, buf0, sem, 0, i).wait()
        _row_copy(y_hbm, buf1, sem, 0, i).wait()
        return carry

    lax.fori_loop(0, tm, drain, 0)
    gate = mod_ref[0][5:6]
    gt = gates_ref[...]
    x = x_ref[...] + gate * (gt[:, 0:1] * buf0[...] + gt[:, 1:2] * buf1[...])
    o_ref[...] = x * lax.rsqrt(jnp.mean(x * x, axis=-1, keepdims=True) + EPS) * g_ref[...]


def _combine(pos_flat, y_pad, x2, gates_t, mod, final_g, bsz, tm):
    n, d = x2.shape
    tiles_per_b = n // bsz // tm
    grid_spec = pltpu.PrefetchScalarGridSpec(
        num_scalar_prefetch=1,
        grid=(n // tm,),
        in_specs=[
            pl.BlockSpec(memory_space=pl.ANY),
            pl.BlockSpec((tm, d), lambda i, pos: (i, 0)),
            pl.BlockSpec((tm, TOP_K), lambda i, pos: (i, 0)),
            pl.BlockSpec((1, 8, d), lambda i, pos: (i // tiles_per_b, 0, 0)),
            pl.BlockSpec((1, d), lambda i, pos: (0, 0)),
        ],
        out_specs=pl.BlockSpec((tm, d), lambda i, pos: (i, 0)),
        scratch_shapes=[pltpu.VMEM((tm, d), F32), pltpu.VMEM((tm, d), F32), pltpu.SemaphoreType.DMA],
    )
    return pl.pallas_call(
        _combine_kernel,
        out_shape=jax.ShapeDtypeStruct((n, d), F32),
        grid_spec=grid_spec,
        compiler_params=_cparams(("arbitrary",)),
        name="moe_combine_final_norm",
    )(pos_flat, y_pad, x2, gates_t, mod, final_g)


def _routing(idx, ne, tm):
    n = idx.shape[1]
    a_tot = n * TOP_K
    flat_e = idx.T.reshape(-1)
    onehot = (flat_e[:, None] == jnp.arange(ne, dtype=jnp.int32)[None, :]).astype(jnp.int32)
    csum = jnp.cumsum(onehot, axis=0)
    rank = jnp.sum((csum - onehot) * onehot, axis=1)
    sizes = csum[-1]
    padded = (sizes + tm - 1) // tm * tm
    pad_end = jnp.cumsum(padded)
    pad_start = pad_end - padded
    grp_start = jnp.cumsum(sizes) - sizes
    dest = (jnp.sum(pad_start[None, :] * onehot, axis=1) + rank).astype(jnp.int32)
    p = a_tot + ne * tm
    nblk = p // tm
    n_active = (pad_end[-1] // tm).astype(jnp.int32)
    blk_start = jnp.arange(nblk, dtype=jnp.int32) * tm
    block_e = jnp.sum((blk_start[:, None] >= pad_end[None, :]).astype(jnp.int32), axis=1)
    last_e = jnp.sum((jnp.maximum(pad_end[-1] - 1, 0) >= pad_end).astype(jnp.int32))
    block_e = jnp.where(jnp.arange(nblk) < n_active, block_e, last_e).astype(jnp.int32)
    order = jnp.argsort(flat_e).astype(jnp.int32)
    slot = jnp.arange(p, dtype=jnp.int32)
    slot_e = jnp.repeat(block_e, tm)
    within = slot - pad_start[slot_e]
    valid = within < sizes[slot_e]
    src_a = order[jnp.clip(grp_start[slot_e] + within, 0, a_tot - 1)]
    src_tok = jnp.where(valid, src_a // TOP_K, 0).astype(jnp.int32)
    pos_flat = dest.reshape(n, TOP_K).T.reshape(-1)
    return block_e, n_active.reshape(1), src_tok, pos_flat


def _rope_perm(d):
    perm = np.zeros((d,), np.int32)
    for h in range(d // HEAD_DIM):
        gi, r = divmod(h, 4)
        for dd in range(HEAD_DIM):
            new = gi * GROUP + (dd // HALF) * LANES + r * HALF + dd % HALF
            perm[new] = h * HEAD_DIM + dd
    return perm


def _mod_rows(mod_l, bsz, d):
    m = mod_l[:bsz].reshape(bsz, 6, d)
    return jnp.pad(m, ((0, 0), (0, 2), (0, 0)))


@jax.jit
def kernel(x, c, positions, w_ada, b_ada, norm_mix_g, norm_ffn_g, attn_w_qkv, attn_w_o, lam_q1, lam_k1,
           lam_q2, lam_k2, attn_subln_g, conv_w_pw1, conv_b_pw1, conv_w_dw, conv_b_dw, conv_ln_g,
           conv_ln_b, conv_w_pw2, conv_b_pw2, ffn_w_gate, ffn_w_up, ffn_w_down, moe_w_router,
           moe_w_gate, moe_w_up, moe_w_down, final_g):
    bsz, t, d = x.shape
    n = bsz * t
    depth = w_ada.shape[0]
    assert depth == 2 and d % GROUP == 0 and bsz <= 8
    tm = min(512, t)
    tk = min(256, t)
    x2 = x.reshape(n, d)

    c8 = jnp.pad(c, ((0, 8 - bsz), (0, 0)))
    mod = _ada(c8, w_ada, b_ada)
    mod0 = _mod_rows(mod[0], bsz, d)
    mod1 = _mod_rows(mod[1], bsz, d)

    inv_freq = ROPE_THETA ** (-jnp.arange(HALF, dtype=F32) / HALF)
    cos, sin = _rope_table(positions.reshape(n, 1), jnp.tile(inv_freq, LANES // HALF).reshape(1, LANES), tm)
    perm = _rope_perm(d)
    w_qkv = attn_w_qkv[0]
    wqt = w_qkv[:, perm].T.astype(BF16)
    wk = w_qkv[:, d + perm].astype(BF16)
    wvt = w_qkv[:, 2 * d:].T.astype(BF16)
    qt, k, vt = _qkv(x2, mod0, norm_mix_g[0:1], cos, sin, positions.reshape(n // tm, 1, tm),
                     inv_freq.reshape(HALF, 1), wqt, wk, wvt, bsz, tm, tk)
    lam_rows = jnp.concatenate([lam_q1[0:1], lam_k1[0:1], lam_q2[0:1], lam_k2[0:1]], axis=0)
    lambda_init = 0.8 - 0.6 * math.exp(-0.3 * 0)
    o = _attention(qt, k, vt, lam_rows, attn_subln_g[0].reshape(V_DIM, 1), tm, tk, lambda_init)
    x2 = _proj_res(o, attn_w_o[0].astype(BF16), x2, mod0, bsz, tm, gate_row=2)
    x2 = _ffn(x2, mod0, norm_ffn_g[0:1], ffn_w_gate[0].astype(BF16), ffn_w_up[0].astype(BF16),
              ffn_w_down[0].astype(BF16), bsz, tm, tf=256)

    u = _pw1(x2, mod1, norm_mix_g[1:2], conv_w_pw1[0].astype(BF16), conv_b_pw1[0:1], bsz, tm)
    v = _dwconv(u, conv_w_dw[0], conv_b_dw[0:1], bsz, tm)
    x2 = _pw2(v, conv_ln_g[0:1], conv_ln_b[0:1], conv_w_pw2[0].astype(BF16), conv_b_pw2[0:1], x2, mod1,
              bsz, tm)
    h, idx, gates = _router(x2, mod1, norm_ffn_g[1:2], moe_w_router[0].T, bsz, tm)
    ne = moe_w_router.shape[-1]
    fe = moe_w_gate.shape[-1]
    block_e, n_active, src_tok, pos_flat = _routing(idx, ne, tm)
    tf_e = 896 if fe % 896 == 0 else fe
    y_pad = _moe(block_e, n_active, src_tok, h, moe_w_gate[0].astype(BF16),
                 moe_w_up[0].astype(BF16), moe_w_down[0].astype(BF16), tm, tf_e)
    out = _combine(pos_flat, y_pad, x2, gates.T, mod1, final_g.reshape(1, d), bsz, tm)
    return out.reshape(bsz, t, d)
```

```python
import functools
import math

import numpy as np
import jax
import jax.numpy as jnp
from jax import lax
from jax.experimental import pallas as pl
from jax.experimental.pallas import tpu as pltpu

EPS = 1e-6
ROPE_THETA = 10000.0
HEAD_DIM = 64
V_DIM = 128
V_ROWS = V_DIM + 16
HALF = HEAD_DIM // 2
LANES = 128
GROUP = 4 * HEAD_DIM
CONV_WIDTH = 31
CONV_HALO = 32
TOP_K = 2
NEG = -1e30
VMEM_LIMIT = 56 * 1024 * 1024

BF16 = jnp.bfloat16
F32 = jnp.float32


def _cparams(sem):
    return pltpu.CompilerParams(dimension_semantics=sem, vmem_limit_bytes=VMEM_LIMIT)


def _dot(a, b):
    return jnp.dot(a, b, preferred_element_type=F32)


def _dot_nt(a, b):
    return lax.dot_general(a, b, (((1,), (1,)), ((), ())), preferred_element_type=F32)


def _norm_mod(x, g, shift, scale):
    y = x * lax.rsqrt(jnp.mean(x * x, axis=-1, keepdims=True) + EPS)
    return (y * g) * (1.0 + scale) + shift


def _sigmoid(x):
    return 1.0 / (1.0 + jnp.exp(-x))


def _ada_kernel(c_ref, w_ref, b_ref, o_ref):
    c = c_ref[...]
    ca = (c * _sigmoid(c)).astype(BF16)
    o_ref[0] = _dot(ca, w_ref[0].astype(BF16)) + b_ref[0]


def _ada(c8, w_ada, b_ada):
    depth, d, d6 = w_ada.shape
    tn = 1536
    return pl.pallas_call(
        _ada_kernel,
        out_shape=jax.ShapeDtypeStruct((depth, 8, d6), F32),
        grid=(depth, d6 // tn),
        in_specs=[
            pl.BlockSpec((8, d), lambda l, j: (0, 0)),
            pl.BlockSpec((1, d, tn), lambda l, j: (l, 0, j)),
            pl.BlockSpec((1, 1, tn), lambda l, j: (l, 0, j)),
        ],
        out_specs=pl.BlockSpec((1, 8, tn), lambda l, j: (l, 0, j)),
        compiler_params=_cparams(("arbitrary", "arbitrary")),
        name="ada_mod",
    )(c8, w_ada, b_ada.reshape(depth, 1, d6))


def _rope_table_kernel(pos_ref, freq_ref, cos_ref, sin_ref):
    ang = pos_ref[...].astype(F32) * freq_ref[...]
    cos_ref[...] = jnp.cos(ang)
    sin_ref[...] = jnp.sin(ang)


def _rope_table(pos_col, freq_row, tm):
    n = pos_col.shape[0]
    return pl.pallas_call(
        _rope_table_kernel,
        out_shape=(jax.ShapeDtypeStruct((n, LANES), F32), jax.ShapeDtypeStruct((n, LANES), F32)),
        grid=(n // tm,),
        in_specs=[pl.BlockSpec((tm, 1), lambda i: (i, 0)), pl.BlockSpec((1, LANES), lambda i: (0, 0))],
        out_specs=(pl.BlockSpec((tm, LANES), lambda i: (i, 0)), pl.BlockSpec((tm, LANES), lambda i: (i, 0))),
        compiler_params=_cparams(("arbitrary",)),
        name="rope_table",
    )(pos_col, freq_row)


def _qkv_kernel(x_ref, mod_ref, g_ref, cos_ref, sin_ref, pos_ref, freq_ref, wqt_ref, wk_ref, wvt_ref,
                qt_ref, k_ref, vt_ref, *, tk):
    m = mod_ref[0]
    h = _norm_mod(x_ref[...], g_ref[...], m[0:1], m[1:2]).astype(BF16)
    d = x_ref.shape[1]
    tm = x_ref.shape[0]
    c = cos_ref[...]
    s = sin_ref[...]
    for gi in range(d // GROUP):
        y = _dot(h, wk_ref[:, gi * GROUP:(gi + 1) * GROUP])
        a = y[:, :LANES]
        b = y[:, LANES:]
        k_ref[:, gi * GROUP:gi * GROUP + LANES] = (a * c - b * s).astype(BF16)
        k_ref[:, gi * GROUP + LANES:(gi + 1) * GROUP] = (b * c + a * s).astype(BF16)
    ang = freq_ref[...] * pos_ref[0].astype(F32)
    reps = LANES // HALF
    ct = jnp.concatenate([jnp.cos(ang)] * reps, axis=0)
    st = jnp.concatenate([jnp.sin(ang)] * reps, axis=0)
    mult = HEAD_DIM ** -0.5 * math.log2(math.e)
    for gi in range(d // GROUP):
        y = _dot_nt(wqt_ref[gi * GROUP:(gi + 1) * GROUP, :], h)
        a = y[:LANES]
        b = y[LANES:]
        qt_ref[0, gi * GROUP:gi * GROUP + LANES, :] = ((a * ct - b * st) * mult).astype(BF16)
        qt_ref[0, gi * GROUP + LANES:(gi + 1) * GROUP, :] = ((b * ct + a * st) * mult).astype(BF16)
    ones = jnp.ones((V_ROWS - V_DIM, tk), BF16)
    for gi in range(d // GROUP):
        vt = _dot_nt(wvt_ref[gi * GROUP:(gi + 1) * GROUP, :], h)
        for hh in range(GROUP // V_DIM):
            for ci in range(tm // tk):
                vt_ref[0, ci, 2 * gi + hh, 0:V_DIM, :] = (
                    vt[hh * V_DIM:(hh + 1) * V_DIM, ci * tk:(ci + 1) * tk].astype(BF16))
                vt_ref[0, ci, 2 * gi + hh, V_DIM:V_ROWS, :] = ones


def _qkv(x2, mod, g, cos, sin, pos_row, freq_col, wqt, wk, wvt, bsz, tm, tk):
    n, d = x2.shape
    t = n // bsz
    tiles_per_b = t // tm
    nh = d // V_DIM
    row = lambda i: (i, 0)
    const = lambda i: (0, 0)
    return pl.pallas_call(
        functools.partial(_qkv_kernel, tk=tk),
        out_shape=(jax.ShapeDtypeStruct((bsz, d, t), BF16), jax.ShapeDtypeStruct((n, d), BF16),
                   jax.ShapeDtypeStruct((bsz, t // tk, nh, V_ROWS, tk), BF16)),
        grid=(n // tm,),
        in_specs=[
            pl.BlockSpec((tm, d), row),
            pl.BlockSpec((1, 8, d), lambda i: (i // tiles_per_b, 0, 0)),
            pl.BlockSpec((1, d), const),
            pl.BlockSpec((tm, LANES), row),
            pl.BlockSpec((tm, LANES), row),
            pl.BlockSpec((1, 1, tm), lambda i: (i, 0, 0)),
            pl.BlockSpec((HALF, 1), const),
            pl.BlockSpec((d, d), const),
            pl.BlockSpec((d, d), const),
            pl.BlockSpec((d, d), const),
        ],
        out_specs=(pl.BlockSpec((1, d, tm), lambda i: (i // tiles_per_b, 0, i % tiles_per_b)),
                   pl.BlockSpec((tm, d), row),
                   pl.BlockSpec((1, tm // tk, nh, V_ROWS, tk),
                                lambda i: (i // tiles_per_b, i % tiles_per_b, 0, 0, 0))),
        compiler_params=_cparams(("arbitrary",)),
        name="qkv_rope",
    )(x2, mod, g, cos, sin, pos_row, freq_col, wqt, wk, wvt)


def _attn_kernel(qt_ref, k_ref, vt_ref, lam_ref, g_ref, o_ref, s_a, s_b, mc_a, mc_b, acc, *, tk,
                 lambda_init):
    tq = qt_ref.shape[2]
    p = pl.program_id(1)
    qi = pl.program_id(2)
    n_full = qi * (tq // tk)
    row = lax.broadcasted_iota(jnp.int32, (GROUP, 1), 0)
    head = (row % LANES) // HALF
    r1 = 2 * (p % 2)
    qf = qt_ref[0].astype(F32)
    qm = (jnp.where(head == r1, qf, 0.0).astype(BF16), jnp.where(head == r1 + 1, qf, 0.0).astype(BF16))

    def scores(kj, s_ref, mc_ref, masked):
        kc = k_ref[pl.ds(pl.multiple_of(kj * tk, tk), tk), :]
        for hd in range(2):
            s = _dot(kc, qm[hd])
            if masked:
                kidx = kj * tk + lax.broadcasted_iota(jnp.int32, (tk, tq), 0)
                qidx = qi * tq + lax.broadcasted_iota(jnp.int32, (tk, tq), 1)
                s = jnp.where(kidx <= qidx, s, NEG)
            s_ref[hd] = s
            mc_ref[hd] = jnp.max(s, axis=0, keepdims=True)

    def softmax_pv(kj, s_ref, mc_ref, ms, first):
        vc = vt_ref[0, kj, 0]
        out = []
        for hd in range(2):
            s = s_ref[hd]
            mc = mc_ref[hd]
            m_new = mc if first else jnp.maximum(ms[hd], mc)
            pv = _dot(vc, jnp.exp2(s - m_new).astype(BF16))
            if first:
                acc[hd] = pv
            else:
                acc[hd] = jnp.exp2(ms[hd] - m_new) * acc[hd] + pv
            out.append(m_new)
        return tuple(out)

    assert tq == 2 * tk
    scores(n_full, s_a, mc_a, True)
    scores(n_full + 1, s_b, mc_b, True)
    ms = softmax_pv(n_full, s_a, mc_a, None, True)
    scores(0, s_a, mc_a, False)
    ms = softmax_pv(n_full + 1, s_b, mc_b, ms, False)

    def body(jj, ms):
        j = 2 * jj
        scores(j + 1, s_b, mc_b, False)
        ms = softmax_pv(j, s_a, mc_a, ms, False)
        scores(j + 2, s_a, mc_a, False)
        return softmax_pv(j + 1, s_b, mc_b, ms, False)

    ms = lax.fori_loop(0, n_full // 2 - 1, body, ms)

    @pl.when(n_full > 0)
    def _():
        scores(n_full - 1, s_b, mc_b, False)
        ms2 = softmax_pv(n_full - 2, s_a, mc_a, ms, False)
        softmax_pv(n_full - 1, s_b, mc_b, ms2, False)

    lp = lam_ref[...]
    lam = (jnp.exp(jnp.sum(lp[0:1] * lp[1:2], axis=-1, keepdims=True))
           - jnp.exp(jnp.sum(lp[2:3] * lp[3:4], axis=-1, keepdims=True)) + lambda_init)
    a1 = acc[0]
    a2 = acc[1]
    o = a1[:V_DIM] / a1[V_DIM:V_DIM + 1] - lam * (a2[:V_DIM] / a2[V_DIM:V_DIM + 1])
    o = o * lax.rsqrt(jnp.mean(o * o, axis=0, keepdims=True) + EPS)
    o = (o * g_ref[...]) * (1.0 - lambda_init)
    o_ref[...] = o.T.astype(BF16)


def _attention(qt, k, vt, lam_rows, g_col, tq, tk, lambda_init):
    bsz, d, t = qt.shape
    n = bsz * t
    nq = t // tq
    npairs = d // V_DIM
    return pl.pallas_call(
        functools.partial(_attn_kernel, tk=tk, lambda_init=lambda_init),
        out_shape=jax.ShapeDtypeStruct((n, d), BF16),
        grid=(bsz, npairs, nq),
        in_specs=[
            pl.BlockSpec((1, GROUP, tq), lambda b, p, i: (b, p // 2, i)),
            pl.BlockSpec((t, GROUP), lambda b, p, i: (b, p // 2)),
            pl.BlockSpec((1, t // tk, 1, V_ROWS, tk), lambda b, p, i: (b, 0, p, 0, 0)),
            pl.BlockSpec((4, HEAD_DIM), lambda b, p, i: (0, 0)),
            pl.BlockSpec((V_DIM, 1), lambda b, p, i: (0, 0)),
        ],
        out_specs=pl.BlockSpec((tq, V_DIM), lambda b, p, i: (b * nq + i, p)),
        scratch_shapes=[pltpu.VMEM((2, tk, tq), F32), pltpu.VMEM((2, tk, tq), F32),
                        pltpu.VMEM((2, 1, tq), F32), pltpu.VMEM((2, 1, tq), F32),
                        pltpu.VMEM((2, V_ROWS, tq), F32)],
        compiler_params=_cparams(("arbitrary", "arbitrary", "arbitrary")),
        name="diff_attn",
    )(qt, k, vt, lam_rows, g_col)


def _ffn_kernel(a_ref, wo_ref, x_ref, mod_ref, g_ref, wg_ref, wu_ref, wd_ref, o_ref, *, tf):
    m = mod_ref[0]
    x = x_ref[...] + m[2:3] * _dot(a_ref[...], wo_ref[...])
    h = _norm_mod(x, g_ref[...], m[3:4], m[4:5]).astype(BF16)
    f = wg_ref.shape[1]
    acc = None
    for ci in range(f // tf):
        sl = slice(ci * tf, (ci + 1) * tf)
        gt = _dot(h, wg_ref[:, sl])
        up = _dot(h, wu_ref[:, sl])
        a = (gt * _sigmoid(gt) * up).astype(BF16)
        part = _dot(a, wd_ref[sl, :])
        acc = part if acc is None else acc + part
    o_ref[...] = x + m[5:6] * acc


def _ffn(a, wo, x2, mod, g, wg, wu, wd, bsz, tm, tf):
    n, d = x2.shape
    f = wg.shape[1]
    tiles_per_b = n // bsz // tm
    row = lambda i: (i, 0)
    const = lambda i: (0, 0)
    return pl.pallas_call(
        functools.partial(_ffn_kernel, tf=tf),
        out_shape=jax.ShapeDtypeStruct((n, d), F32),
        grid=(n // tm,),
        in_specs=[
            pl.BlockSpec((tm, d), row),
            pl.BlockSpec((d, d), const),
            pl.BlockSpec((tm, d), row),
            pl.BlockSpec((1, 8, d), lambda i: (i // tiles_per_b, 0, 0)),
            pl.BlockSpec((1, d), const),
            pl.BlockSpec((d, f), const),
            pl.BlockSpec((d, f), const),
            pl.BlockSpec((f, d), const),
        ],
        out_specs=pl.BlockSpec((tm, d), row),
        compiler_params=_cparams(("arbitrary",)),
        name="oproj_dense_swiglu",
    )(a, wo, x2, mod, g, wg, wu, wd)


def _pw1_kernel(x_ref, mod_ref, g_ref, w_ref, b_ref, o_ref, *, tn):
    m = mod_ref[0]
    h = _norm_mod(x_ref[...], g_ref[...], m[0:1], m[1:2]).astype(BF16)
    d = x_ref.shape[1]
    for ci in range(d // tn):
        a = _dot(h, w_ref[:, ci * tn:(ci + 1) * tn]) + b_ref[:, ci * tn:(ci + 1) * tn]
        gt = _dot(h, w_ref[:, d + ci * tn:d + (ci + 1) * tn]) + b_ref[:, d + ci * tn:d + (ci + 1) * tn]
        o_ref[:, ci * tn:(ci + 1) * tn] = a * _sigmoid(gt)


def _pw1(x2, mod, g, w, b, bsz, tm):
    n, d = x2.shape
    tiles_per_b = n // bsz // tm
    row = lambda i: (i, 0)
    const = lambda i: (0, 0)
    return pl.pallas_call(
        functools.partial(_pw1_kernel, tn=256),
        out_shape=jax.ShapeDtypeStruct((n, d), F32),
        grid=(n // tm,),
        in_specs=[
            pl.BlockSpec((tm, d), row),
            pl.BlockSpec((1, 8, d), lambda i: (i // tiles_per_b, 0, 0)),
            pl.BlockSpec((1, d), const),
            pl.BlockSpec((d, 2 * d), const),
            pl.BlockSpec((1, 2 * d), const),
        ],
        out_specs=pl.BlockSpec((tm, d), row),
        compiler_params=_cparams(("arbitrary",)),
        name="conv_pw1_glu",
    )(x2, mod, g, w, b)


def _dwconv_kernel(u_ref, halo_ref, w_ref, b_ref, o_ref, ext, *, tiles_per_b, rb):
    tm = u_ref.shape[0]
    first = (pl.program_id(0) % tiles_per_b) == 0
    ext[0:CONV_HALO, :] = jnp.where(first, 0.0, halo_ref[...])
    ext[CONV_HALO:, :] = u_ref[...]
    off = CONV_HALO - (CONV_WIDTH - 1)
    for lc in range(u_ref.shape[1] // LANES):
        cols = slice(lc * LANES, (lc + 1) * LANES)
        for r in range(tm // rb):
            acc = jnp.broadcast_to(b_ref[:, cols], (rb, LANES))
            for kk in range(CONV_WIDTH):
                acc = acc + w_ref[kk:kk + 1, cols] * ext[r * rb + off + kk:r * rb + off + kk + rb, cols]
            o_ref[r * rb:(r + 1) * rb, cols] = acc


def _dwconv(u, w, b, bsz, tm):
    n, d = u.shape
    tiles_per_b = n // bsz // tm
    hb = tm // CONV_HALO
    tc = 2 * LANES
    return pl.pallas_call(
        functools.partial(_dwconv_kernel, tiles_per_b=tiles_per_b, rb=64),
        out_shape=jax.ShapeDtypeStruct((n, d), F32),
        grid=(n // tm, d // tc),
        in_specs=[
            pl.BlockSpec((tm, tc), lambda i, j: (i, j)),
            pl.BlockSpec((CONV_HALO, tc), lambda i, j: (jnp.maximum(i * hb - 1, 0), j)),
            pl.BlockSpec((CONV_WIDTH, tc), lambda i, j: (0, j)),
            pl.BlockSpec((1, tc), lambda i, j: (0, j)),
        ],
        out_specs=pl.BlockSpec((tm, tc), lambda i, j: (i, j)),
        scratch_shapes=[pltpu.VMEM((tm + CONV_HALO, tc), F32)],
        compiler_params=_cparams(("arbitrary", "arbitrary")),
        name="conv_depthwise",
    )(u, u, w, b)


def _pw2_router_kernel(v_ref, lg_ref, lb_ref, w_ref, b_ref, x_ref, mod_ref, g_ref, wr_ref,
                       x_out_ref, h_ref, idx_ref, gate_ref):
    v = v_ref[...]
    mu = jnp.mean(v, axis=-1, keepdims=True)
    vc = v - mu
    var = jnp.mean(vc * vc, axis=-1, keepdims=True)
    y = vc * lax.rsqrt(var + EPS) * lg_ref[...] + lb_ref[...]
    y = (y * _sigmoid(y)).astype(BF16)
    m = mod_ref[0]
    x = x_ref[...] + m[2:3] * (_dot(y, w_ref[...]) + b_ref[...])
    x_out_ref[...] = x
    h = _norm_mod(x, g_ref[...], m[3:4], m[4:5])
    h_ref[...] = h
    logits = lax.dot_general(wr_ref[...], h, (((1,), (1,)), ((), ())),
                             preferred_element_type=F32, precision=lax.Precision.HIGHEST)
    ne = logits.shape[0]
    eidx = lax.broadcasted_iota(jnp.int32, logits.shape, 0)
    v1 = jnp.max(logits, axis=0, keepdims=True)
    i1 = jnp.min(jnp.where(logits == v1, eidx, ne), axis=0, keepdims=True)
    rest = jnp.where(eidx == i1, -jnp.inf, logits)
    v2 = jnp.max(rest, axis=0, keepdims=True)
    i2 = jnp.min(jnp.where(rest == v2, eidx, ne), axis=0, keepdims=True)
    e2 = jnp.exp(v2 - v1)
    den = 1.0 + e2
    idx_ref[0:1, :] = i1
    idx_ref[1:2, :] = i2
    gate_ref[0:1, :] = 1.0 / den
    gate_ref[1:2, :] = e2 / den


def _pw2_router(v, ln_g, ln_b, w, b, x2, mod, g, wr_t, bsz, tm):
    n, d = x2.shape
    ne = wr_t.shape[0]
    tiles_per_b = n // bsz // tm
    row = lambda i: (i, 0)
    const = lambda i: (0, 0)
    return pl.pallas_call(
        _pw2_router_kernel,
        out_shape=(jax.ShapeDtypeStruct((n, d), F32), jax.ShapeDtypeStruct((n, d), F32),
                   jax.ShapeDtypeStruct((TOP_K, n), jnp.int32), jax.ShapeDtypeStruct((TOP_K, n), F32)),
        grid=(n // tm,),
        in_specs=[
            pl.BlockSpec((tm, d), row),
            pl.BlockSpec((1, d), const),
            pl.BlockSpec((1, d), const),
            pl.BlockSpec((d, d), const),
            pl.BlockSpec((1, d), const),
            pl.BlockSpec((tm, d), row),
            pl.BlockSpec((1, 8, d), lambda i: (i // tiles_per_b, 0, 0)),
            pl.BlockSpec((1, d), const),
            pl.BlockSpec((ne, d), const),
        ],
        out_specs=(pl.BlockSpec((tm, d), row), pl.BlockSpec((tm, d), row),
                   pl.BlockSpec((TOP_K, tm), lambda i: (0, i)), pl.BlockSpec((TOP_K, tm), lambda i: (0, i))),
        compiler_params=_cparams(("arbitrary",)),
        name="conv_ln_pw2_router",
    )(v, ln_g, ln_b, w, b, x2, mod, g, wr_t)


def _moe_kernel(be_ref, na_ref, src_ref, dst_ref, h_hbm, wg_ref, wu_ref, wd_ref, y_hbm,
                xbuf, xb16, acc, gsem, ssem, *, n_assign, n_spare_blocks):
    tm = xb16.shape[0]
    nf = pl.num_programs(1)
    nblk = pl.num_programs(0)
    rpb = tm // LANES
    blk = pl.program_id(0)
    f = pl.program_id(1)
    na = na_ref[0]
    active = blk < na
    slot = blk % 2
    other = 1 - slot

    def gather_row(trow, col, buf, r):
        return pltpu.make_async_copy(h_hbm.at[pl.ds(src_ref[trow, col], 1)],
                                     xbuf.at[buf, pl.ds(r, 1)], gsem.at[buf])

    def scatter_row(trow, col, buf, r):
        return pltpu.make_async_copy(acc.at[buf, pl.ds(r, 1)], y_hbm.at[pl.ds(dst_ref[trow, col], 1)],
                                     ssem.at[buf])

    def wait_gather(buf):
        pltpu.make_async_copy(h_hbm.at[pl.ds(0, tm)], xbuf.at[buf], gsem.at[buf]).wait()

    def wait_scatter(buf):
        pltpu.make_async_copy(acc.at[buf], y_hbm.at[pl.ds(0, tm)], ssem.at[buf]).wait()

    def scatter_block(b, buf):
        def body(i, carry):
            scatter_row(b * rpb + i // LANES, i % LANES, buf, i).start()
            return carry
        lax.fori_loop(0, tm, body, 0)
        wait_scatter(buf)

    @pl.when(jnp.logical_and(blk == 0, f == 0))
    def _():
        def body(i, carry):
            gather_row(i // LANES, i % LANES, 0, i).start()
            return carry
        lax.fori_loop(0, tm, body, 0)
        acc[1] = jnp.zeros(acc.shape[1:], F32)
        spare_fill = [pltpu.make_async_copy(acc.at[1], y_hbm.at[pl.ds(n_assign + j * tm, tm)], ssem.at[1])
                      for j in range(n_spare_blocks)]
        for cp in spare_fill:
            cp.start()
        for cp in spare_fill:
            cp.wait()

    @pl.when(jnp.logical_and(active, f == 0))
    def _():
        wait_gather(slot)
        xb16[...] = xbuf[slot].astype(BF16)

        @pl.when(blk >= 1)
        def _():
            wait_scatter(slot)
        acc[slot] = jnp.zeros(acc.shape[1:], F32)

    @pl.when(active)
    def _():
        quarter = tm // nf
        qrows = quarter // LANES
        gbase = (blk + 1) * rpb + f * qrows
        sbase = jnp.where(blk == 0, nblk, blk - 1) * rpb + f * qrows
        for i in range(quarter):
            gather_row(gbase + i // LANES, i % LANES, other, f * quarter + i).start()
            scatter_row(sbase + i // LANES, i % LANES, other, f * quarter + i).start()
        xb = xb16[...]
        gt = _dot(xb, wg_ref[0])
        up = _dot(xb, wu_ref[0])
        acc[slot] += _dot((gt * _sigmoid(gt) * up).astype(BF16), wd_ref[0])

    @pl.when(jnp.logical_and(f == 0, blk == na))
    def _():
        wait_gather(slot)
        wait_scatter(slot)
        scatter_block(blk - 1, other)

    @pl.when(jnp.logical_and(jnp.logical_and(f == nf - 1, blk == nblk - 1), active))
    def _():
        wait_gather(other)
        wait_scatter(other)
        scatter_block(blk, slot)


def _moe(block_e, n_active, src_tab, dst_tab, h, wg, wu, wd, n_out, tm, tf):
    n, d = h.shape
    ne, _, fe = wg.shape
    nf = fe // tf
    nblk = dst_tab.shape[0] * LANES // tm - 1
    assert tm % (nf * LANES) == 0

    def wsel(blk, f, be, na, src, dst):
        return jnp.where(blk < na[0], f, nf - 1)

    grid_spec = pltpu.PrefetchScalarGridSpec(
        num_scalar_prefetch=4,
        grid=(nblk, nf),
        in_specs=[
            pl.BlockSpec(memory_space=pl.ANY),
            pl.BlockSpec((1, d, tf), lambda blk, f, be, na, src, dst: (be[blk], 0, wsel(blk, f, be, na, src, dst))),
            pl.BlockSpec((1, d, tf), lambda blk, f, be, na, src, dst: (be[blk], 0, wsel(blk, f, be, na, src, dst))),
            pl.BlockSpec((1, tf, d), lambda blk, f, be, na, src, dst: (be[blk], wsel(blk, f, be, na, src, dst), 0)),
        ],
        out_specs=pl.BlockSpec(memory_space=pl.ANY),
        scratch_shapes=[pltpu.VMEM((2, tm, d), F32), pltpu.VMEM((tm, d), BF16), pltpu.VMEM((2, tm, d), F32),
                        pltpu.SemaphoreType.DMA((2,)), pltpu.SemaphoreType.DMA((2,))],
    )
    return pl.pallas_call(
        functools.partial(_moe_kernel, n_assign=n * TOP_K, n_spare_blocks=ne),
        out_shape=jax.ShapeDtypeStruct((n_out, d), F32),
        grid_spec=grid_spec,
        compiler_params=_cparams(("arbitrary", "arbitrary")),
        name="moe_experts",
    )(block_e, n_active, src_tab, dst_tab, h, wg, wu, wd)


def _combine_kernel(y0_ref, y1_ref, x_ref, gates_ref, mod_ref, g_ref, o_ref):
    gate = mod_ref[0][5:6]
    gt = gates_ref[...]
    x = x_ref[...] + gate * (gt[:, 0:1] * y0_ref[...] + gt[:, 1:2] * y1_ref[...])
    o_ref[...] = x * lax.rsqrt(jnp.mean(x * x, axis=-1, keepdims=True) + EPS) * g_ref[...]


def _combine(y, x2, gates_t, mod, final_g, bsz, tm):
    n, d = x2.shape
    tiles_per_b = n // bsz // tm
    nt = n // tm
    row = lambda i: (i, 0)
    return pl.pallas_call(
        _combine_kernel,
        out_shape=jax.ShapeDtypeStruct((n, d), F32),
        grid=(nt,),
        in_specs=[
            pl.BlockSpec((tm, d), row),
            pl.BlockSpec((tm, d), lambda i: (nt + i, 0)),
            pl.BlockSpec((tm, d), row),
            pl.BlockSpec((tm, TOP_K), row),
            pl.BlockSpec((1, 8, d), lambda i: (i // tiles_per_b, 0, 0)),
            pl.BlockSpec((1, d), lambda i: (0, 0)),
        ],
        out_specs=pl.BlockSpec((tm, d), row),
        compiler_params=_cparams(("arbitrary",)),
        name="moe_combine_final_norm",
    )(y, y, x2, gates_t, mod, final_g)


def _routing(idx, ne, tm):
    n = idx.shape[1]
    a_tot = n * TOP_K
    flat_e = idx.T.reshape(-1)
    sizes = jnp.sum((flat_e[:, None] == jnp.arange(ne, dtype=jnp.int32)[None, :]).astype(jnp.int32), axis=0)
    padded = (sizes + tm - 1) // tm * tm
    pad_end = jnp.cumsum(padded)
    pad_start = pad_end - padded
    grp_start = jnp.cumsum(sizes) - sizes
    p = a_tot + ne * tm
    nblk = p // tm
    n_active = (pad_end[-1] // tm).astype(jnp.int32)
    blk_start = jnp.arange(nblk, dtype=jnp.int32) * tm
    block_e = jnp.sum((blk_start[:, None] >= pad_end[None, :]).astype(jnp.int32), axis=1)
    last_e = jnp.sum((jnp.maximum(pad_end[-1] - 1, 0) >= pad_end).astype(jnp.int32))
    block_e = jnp.where(jnp.arange(nblk) < n_active, block_e, last_e).astype(jnp.int32)
    order = jnp.argsort(flat_e).astype(jnp.int32)
    slot = jnp.arange(p, dtype=jnp.int32)
    slot_e = jnp.repeat(block_e, tm)
    within = slot - pad_start[slot_e]
    valid = within < sizes[slot_e]
    src_a = order[jnp.clip(grp_start[slot_e] + within, 0, a_tot - 1)]
    src_tok = jnp.where(valid, src_a // TOP_K, 0).astype(jnp.int32)
    spare = a_tot + jnp.cumsum(jnp.logical_not(valid).astype(jnp.int32)) - 1
    dst_row = jnp.where(valid, (src_a % TOP_K) * n + src_a // TOP_K, spare).astype(jnp.int32)
    src_tab = jnp.pad(src_tok, (0, tm)).reshape(-1, LANES)
    dst_tab = jnp.concatenate([dst_row, p + jnp.arange(tm, dtype=jnp.int32)]).reshape(-1, LANES)
    return block_e, n_active.reshape(1), src_tab, dst_tab, p + tm


def _rope_perm(d):
    perm = np.zeros((d,), np.int32)
    for h in range(d // HEAD_DIM):
        gi, r = divmod(h, 4)
        for dd in range(HEAD_DIM):
            new = gi * GROUP + (dd // HALF) * LANES + r * HALF + dd % HALF
            perm[new] = h * HEAD_DIM + dd
    return perm


def _mod_rows(mod_l, bsz, d):
    m = mod_l[:bsz].reshape(bsz, 6, d)
    return jnp.pad(m, ((0, 0), (0, 2), (0, 0)))


@jax.jit
def kernel(x, c, positions, w_ada, b_ada, norm_mix_g, norm_ffn_g, attn_w_qkv, attn_w_o, lam_q1, lam_k1,
           lam_q2, lam_k2, attn_subln_g, conv_w_pw1, conv_b_pw1, conv_w_dw, conv_b_dw, conv_ln_g,
           conv_ln_b, conv_w_pw2, conv_b_pw2, ffn_w_gate, ffn_w_up, ffn_w_down, moe_w_router,
           moe_w_gate, moe_w_up, moe_w_down, final_g):
    bsz, t, d = x.shape
    n = bsz * t
    depth = w_ada.shape[0]
    assert depth == 2 and d % GROUP == 0 and bsz <= 8
    tm = min(512, t)
    tk = min(256, t)
    x2 = x.reshape(n, d)

    c8 = jnp.pad(c, ((0, 8 - bsz), (0, 0)))
    mod = _ada(c8, w_ada, b_ada)
    mod0 = _mod_rows(mod[0], bsz, d)
    mod1 = _mod_rows(mod[1], bsz, d)

    inv_freq = ROPE_THETA ** (-jnp.arange(HALF, dtype=F32) / HALF)
    cos, sin = _rope_table(positions.reshape(n, 1), jnp.tile(inv_freq, LANES // HALF).reshape(1, LANES), tm)
    perm = _rope_perm(d)
    w_qkv = attn_w_qkv[0]
    wqt = w_qkv[:, perm].T.astype(BF16)
    wk = w_qkv[:, d + perm].astype(BF16)
    wvt = w_qkv[:, 2 * d:].T.astype(BF16)
    qt, k, vt = _qkv(x2, mod0, norm_mix_g[0:1], cos, sin, positions.reshape(n // tm, 1, tm),
                     inv_freq.reshape(HALF, 1), wqt, wk, wvt, bsz, tm, tk)
    lam_rows = jnp.concatenate([lam_q1[0:1], lam_k1[0:1], lam_q2[0:1], lam_k2[0:1]], axis=0)
    lambda_init = 0.8 - 0.6 * math.exp(-0.3 * 0)
    o = _attention(qt, k, vt, lam_rows, attn_subln_g[0].reshape(V_DIM, 1), tm, tk, lambda_init)
    x2 = _ffn(o, attn_w_o[0].astype(BF16), x2, mod0, norm_ffn_g[0:1], ffn_w_gate[0].astype(BF16),
              ffn_w_up[0].astype(BF16), ffn_w_down[0].astype(BF16), bsz, tm, tf=256)

    u = _pw1(x2, mod1, norm_mix_g[1:2], conv_w_pw1[0].astype(BF16), conv_b_pw1[0:1], bsz, tm)
    v = _dwconv(u, conv_w_dw[0], conv_b_dw[0:1], bsz, tm)
    x2, h, idx, gates = _pw2_router(v, conv_ln_g[0:1], conv_ln_b[0:1], conv_w_pw2[0].astype(BF16),
                                    conv_b_pw2[0:1], x2, mod1, norm_ffn_g[1:2], moe_w_router[0].T, bsz, tm)
    ne = moe_w_router.shape[-1]
    fe = moe_w_gate.shape[-1]
    block_e, n_active, src_tab, dst_tab, n_out = _routing(idx, ne, tm)
    tf_e = 896 if fe % 896 == 0 else fe
    y = _moe(block_e, n_active, src_tab, dst_tab, h, moe_w_gate[0].astype(BF16),
             moe_w_up[0].astype(BF16), moe_w_down[0].astype(BF16), n_out, tm, tf_e)
    out = _combine(y, x2, gates.T, mod1, final_g.reshape(1, d), bsz, tm)
    return out.reshape(bsz, t, d)
```

```python
import functools
import math

import numpy as np
import jax
import jax.numpy as jnp
from jax import lax
from jax.experimental import pallas as pl
from jax.experimental.pallas import tpu as pltpu

EPS = 1e-6
ROPE_THETA = 10000.0
HEAD_DIM = 64
V_DIM = 128
V_ROWS = V_DIM + 16
HALF = HEAD_DIM // 2
LANES = 128
GROUP = 4 * HEAD_DIM
CONV_WIDTH = 31
CONV_HALO = 32
TOP_K = 2
NEG = -1e30
VMEM_LIMIT = 56 * 1024 * 1024

BF16 = jnp.bfloat16
F32 = jnp.float32


def _cparams(sem):
    return pltpu.CompilerParams(dimension_semantics=sem, vmem_limit_bytes=VMEM_LIMIT)


def _dot(a, b):
    return jnp.dot(a, b, preferred_element_type=F32)


def _dot_nt(a, b):
    return lax.dot_general(a, b, (((1,), (1,)), ((), ())), preferred_element_type=F32)


def _norm_mod(x, g, shift, scale):
    y = x * lax.rsqrt(jnp.mean(x * x, axis=-1, keepdims=True) + EPS)
    return (y * g) * (1.0 + scale) + shift


def _sigmoid(x):
    return 1.0 / (1.0 + jnp.exp(-x))


def _ada_kernel(c_ref, w_ref, b_ref, o_ref):
    c = c_ref[...]
    ca = (c * _sigmoid(c)).astype(BF16)
    o_ref[0] = _dot(ca, w_ref[0].astype(BF16)) + b_ref[0]


def _ada(c8, w_ada, b_ada):
    depth, d, d6 = w_ada.shape
    tn = 1536
    return pl.pallas_call(
        _ada_kernel,
        out_shape=jax.ShapeDtypeStruct((depth, 8, d6), F32),
        grid=(depth, d6 // tn),
        in_specs=[
            pl.BlockSpec((8, d), lambda l, j: (0, 0)),
            pl.BlockSpec((1, d, tn), lambda l, j: (l, 0, j)),
            pl.BlockSpec((1, 1, tn), lambda l, j: (l, 0, j)),
        ],
        out_specs=pl.BlockSpec((1, 8, tn), lambda l, j: (l, 0, j)),
        compiler_params=_cparams(("arbitrary", "arbitrary")),
        name="ada_mod",
    )(c8, w_ada, b_ada.reshape(depth, 1, d6))


def _qkv_kernel(x_ref, mod_ref, g_ref, pos_ref, freq_ref, wqt_ref, wk_ref, wvt_ref,
                qt_ref, k_ref, vt_ref, *, tk):
    m = mod_ref[0]
    h = _norm_mod(x_ref[...], g_ref[...], m[0:1], m[1:2]).astype(BF16)
    d = x_ref.shape[1]
    tm = x_ref.shape[0]
    ang = freq_ref[...] * pos_ref[0].astype(F32)
    reps = LANES // HALF
    ct = jnp.concatenate([jnp.cos(ang)] * reps, axis=0)
    st = jnp.concatenate([jnp.sin(ang)] * reps, axis=0)
    c = ct.T
    s = st.T
    for gi in range(d // GROUP):
        y = _dot(h, wk_ref[:, gi * GROUP:(gi + 1) * GROUP])
        a = y[:, :LANES]
        b = y[:, LANES:]
        k_ref[:, gi * GROUP:gi * GROUP + LANES] = (a * c - b * s).astype(BF16)
        k_ref[:, gi * GROUP + LANES:(gi + 1) * GROUP] = (b * c + a * s).astype(BF16)
    mult = HEAD_DIM ** -0.5 * math.log2(math.e)
    for gi in range(d // GROUP):
        y = _dot_nt(wqt_ref[gi * GROUP:(gi + 1) * GROUP, :], h)
        a = y[:LANES]
        b = y[LANES:]
        qt_ref[0, gi * GROUP:gi * GROUP + LANES, :] = ((a * ct - b * st) * mult).astype(BF16)
        qt_ref[0, gi * GROUP + LANES:(gi + 1) * GROUP, :] = ((b * ct + a * st) * mult).astype(BF16)
    ones = jnp.ones((V_ROWS - V_DIM, tk), BF16)
    for gi in range(d // GROUP):
        vt = _dot_nt(wvt_ref[gi * GROUP:(gi + 1) * GROUP, :], h)
        for hh in range(GROUP // V_DIM):
            for ci in range(tm // tk):
                vt_ref[0, ci, 2 * gi + hh, 0:V_DIM, :] = (
                    vt[hh * V_DIM:(hh + 1) * V_DIM, ci * tk:(ci + 1) * tk].astype(BF16))
                vt_ref[0, ci, 2 * gi + hh, V_DIM:V_ROWS, :] = ones


def _qkv(x2, mod, g, pos_row, freq_col, wqt, wk, wvt, bsz, tm, tk):
    n, d = x2.shape
    t = n // bsz
    tiles_per_b = t // tm
    nh = d // V_DIM
    row = lambda i: (i, 0)
    const = lambda i: (0, 0)
    return pl.pallas_call(
        functools.partial(_qkv_kernel, tk=tk),
        out_shape=(jax.ShapeDtypeStruct((bsz, d, t), BF16), jax.ShapeDtypeStruct((n, d), BF16),
                   jax.ShapeDtypeStruct((bsz, t // tk, nh, V_ROWS, tk), BF16)),
        grid=(n // tm,),
        in_specs=[
            pl.BlockSpec((tm, d), row),
            pl.BlockSpec((1, 8, d), lambda i: (i // tiles_per_b, 0, 0)),
            pl.BlockSpec((1, d), const),
            pl.BlockSpec((1, 1, tm), lambda i: (i, 0, 0)),
            pl.BlockSpec((HALF, 1), const),
            pl.BlockSpec((d, d), const),
            pl.BlockSpec((d, d), const),
            pl.BlockSpec((d, d), const),
        ],
        out_specs=(pl.BlockSpec((1, d, tm), lambda i: (i // tiles_per_b, 0, i % tiles_per_b)),
                   pl.BlockSpec((tm, d), row),
                   pl.BlockSpec((1, tm // tk, nh, V_ROWS, tk),
                                lambda i: (i // tiles_per_b, i % tiles_per_b, 0, 0, 0))),
        compiler_params=_cparams(("arbitrary",)),
        name="qkv_rope",
    )(x2, mod, g, pos_row, freq_col, wqt, wk, wvt)


def _attn_kernel(qt_ref, k_ref, vt_ref, lam_ref, g_ref, o_ref, s_a, s_b, mc_a, mc_b, acc, *, tk,
                 lambda_init):
    tq = qt_ref.shape[2]
    nh = acc.shape[0]
    qi = pl.program_id(2)
    n_full = qi * (tq // tk)
    row = lax.broadcasted_iota(jnp.int32, (GROUP, 1), 0)
    head = (row % LANES) // HALF
    qf = qt_ref[0].astype(F32)
    qm = [jnp.where(head == hd, qf, 0.0).astype(BF16) for hd in range(nh)]

    def scores(kj, s_ref, mc_ref, masked, lo=0):
        kc = k_ref[pl.ds(pl.multiple_of(kj * tk, tk), tk), :]
        for hd in range(nh):
            s = _dot(kc, qm[hd][:, lo:])
            if masked:
                kidx = kj * tk + lax.broadcasted_iota(jnp.int32, s.shape, 0)
                qidx = qi * tq + lo + lax.broadcasted_iota(jnp.int32, s.shape, 1)
                s = jnp.where(kidx <= qidx, s, NEG)
            s_ref[hd, :, lo:] = s
            mc_ref[hd, :, lo:] = jnp.max(s, axis=0, keepdims=True)

    def softmax_pv(kj, s_ref, mc_ref, ms, first, lo=0):
        out = []
        for hd in range(nh):
            vc = vt_ref[0, kj, hd // 2]
            s = s_ref[hd, :, lo:]
            mc = mc_ref[hd, :, lo:]
            m_new = mc if first else jnp.maximum(ms[hd][:, lo:], mc)
            pv = _dot(vc, jnp.exp2(s - m_new).astype(BF16))
            if first:
                acc[hd] = pv
            else:
                acc[hd, :, lo:] = jnp.exp2(ms[hd][:, lo:] - m_new) * acc[hd, :, lo:] + pv
            out.append(m_new if lo == 0 else jnp.concatenate([ms[hd][:, :lo], m_new], axis=1))
        return tuple(out)

    assert tq == 2 * tk
    scores(n_full, s_a, mc_a, True)
    scores(n_full + 1, s_b, mc_b, True, lo=tk)
    ms = softmax_pv(n_full, s_a, mc_a, None, True)
    scores(0, s_a, mc_a, False)
    ms = softmax_pv(n_full + 1, s_b, mc_b, ms, False, lo=tk)

    def body(jj, ms):
        j = 2 * jj
        scores(j + 1, s_b, mc_b, False)
        ms = softmax_pv(j, s_a, mc_a, ms, False)
        scores(j + 2, s_a, mc_a, False)
        return softmax_pv(j + 1, s_b, mc_b, ms, False)

    ms = lax.fori_loop(0, n_full // 2 - 1, body, ms)

    @pl.when(n_full > 0)
    def _():
        scores(n_full - 1, s_b, mc_b, False)
        ms2 = softmax_pv(n_full - 2, s_a, mc_a, ms, False)
        softmax_pv(n_full - 1, s_b, mc_b, ms2, False)

    lp = lam_ref[...]
    lam = (jnp.exp(jnp.sum(lp[0:1] * lp[1:2], axis=-1, keepdims=True))
           - jnp.exp(jnp.sum(lp[2:3] * lp[3:4], axis=-1, keepdims=True)) + lambda_init)
    for pr in range(nh // 2):
        a1 = acc[2 * pr]
        a2 = acc[2 * pr + 1]
        o = a1[:V_DIM] / a1[V_DIM:V_DIM + 1] - lam * (a2[:V_DIM] / a2[V_DIM:V_DIM + 1])
        o = o * lax.rsqrt(jnp.mean(o * o, axis=0, keepdims=True) + EPS)
        o = (o * g_ref[...]) * (1.0 - lambda_init)
        o_ref[:, pr * V_DIM:(pr + 1) * V_DIM] = o.T.astype(BF16)


def _attention(qt, k, vt, lam_rows, g_col, tq, tk, lambda_init):
    bsz, d, t = qt.shape
    n = bsz * t
    nq = t // tq
    nh = GROUP // HEAD_DIM
    return pl.pallas_call(
        functools.partial(_attn_kernel, tk=tk, lambda_init=lambda_init),
        out_shape=jax.ShapeDtypeStruct((n, d), BF16),
        grid=(bsz, d // GROUP, nq),
        in_specs=[
            pl.BlockSpec((1, GROUP, tq), lambda b, g, i: (b, g, i)),
            pl.BlockSpec((t, GROUP), lambda b, g, i: (b, g)),
            pl.BlockSpec((1, t // tk, nh // 2, V_ROWS, tk), lambda b, g, i: (b, 0, g, 0, 0)),
            pl.BlockSpec((4, HEAD_DIM), lambda b, g, i: (0, 0)),
            pl.BlockSpec((V_DIM, 1), lambda b, g, i: (0, 0)),
        ],
        out_specs=pl.BlockSpec((tq, GROUP), lambda b, g, i: (b * nq + i, g)),
        scratch_shapes=[pltpu.VMEM((nh, tk, tq), F32), pltpu.VMEM((nh, tk, tq), F32),
                        pltpu.VMEM((nh, 1, tq), F32), pltpu.VMEM((nh, 1, tq), F32),
                        pltpu.VMEM((nh, V_ROWS, tq), F32)],
        compiler_params=_cparams(("arbitrary", "arbitrary", "arbitrary")),
        name="diff_attn",
    )(qt, k, vt, lam_rows, g_col)


def _ffn_kernel(a_ref, wo_ref, x_ref, mod_ref, g_ref, wg_ref, wu_ref, wd_ref, o_ref, *, tf):
    m = mod_ref[0]
    x = x_ref[...] + m[2:3] * _dot(a_ref[...], wo_ref[...])
    h = _norm_mod(x, g_ref[...], m[3:4], m[4:5]).astype(BF16)
    f = wg_ref.shape[1]
    acc = None
    for ci in range(f // tf):
        sl = slice(ci * tf, (ci + 1) * tf)
        gt = _dot(h, wg_ref[:, sl])
        up = _dot(h, wu_ref[:, sl])
        a = (gt * _sigmoid(gt) * up).astype(BF16)
        part = _dot(a, wd_ref[sl, :])
        acc = part if acc is None else acc + part
    o_ref[...] = x + m[5:6] * acc


def _ffn(a, wo, x2, mod, g, wg, wu, wd, bsz, tm, tf):
    n, d = x2.shape
    f = wg.shape[1]
    tiles_per_b = n // bsz // tm
    row = lambda i: (i, 0)
    const = lambda i: (0, 0)
    return pl.pallas_call(
        functools.partial(_ffn_kernel, tf=tf),
        out_shape=jax.ShapeDtypeStruct((n, d), F32),
        grid=(n // tm,),
        in_specs=[
            pl.BlockSpec((tm, d), row),
            pl.BlockSpec((d, d), const),
            pl.BlockSpec((tm, d), row),
            pl.BlockSpec((1, 8, d), lambda i: (i // tiles_per_b, 0, 0)),
            pl.BlockSpec((1, d), const),
            pl.BlockSpec((d, f), const),
            pl.BlockSpec((d, f), const),
            pl.BlockSpec((f, d), const),
        ],
        out_specs=pl.BlockSpec((tm, d), row),
        compiler_params=_cparams(("arbitrary",)),
        name="oproj_dense_swiglu",
    )(a, wo, x2, mod, g, wg, wu, wd)


def _pw1_kernel(x_ref, mod_ref, g_ref, w_ref, b_ref, o_ref, *, tn):
    m = mod_ref[0]
    h = _norm_mod(x_ref[...], g_ref[...], m[0:1], m[1:2]).astype(BF16)
    d = x_ref.shape[1]
    for ci in range(d // tn):
        a = _dot(h, w_ref[:, ci * tn:(ci + 1) * tn]) + b_ref[:, ci * tn:(ci + 1) * tn]
        gt = _dot(h, w_ref[:, d + ci * tn:d + (ci + 1) * tn]) + b_ref[:, d + ci * tn:d + (ci + 1) * tn]
        o_ref[:, ci * tn:(ci + 1) * tn] = a * _sigmoid(gt)


def _pw1(x2, mod, g, w, b, bsz, tm):
    n, d = x2.shape
    tiles_per_b = n // bsz // tm
    row = lambda i: (i, 0)
    const = lambda i: (0, 0)
    return pl.pallas_call(
        functools.partial(_pw1_kernel, tn=256),
        out_shape=jax.ShapeDtypeStruct((n, d), F32),
        grid=(n // tm,),
        in_specs=[
            pl.BlockSpec((tm, d), row),
            pl.BlockSpec((1, 8, d), lambda i: (i // tiles_per_b, 0, 0)),
            pl.BlockSpec((1, d), const),
            pl.BlockSpec((d, 2 * d), const),
            pl.BlockSpec((1, 2 * d), const),
        ],
        out_specs=pl.BlockSpec((tm, d), row),
        compiler_params=_cparams(("arbitrary",)),
        name="conv_pw1_glu",
    )(x2, mod, g, w, b)


def _dwconv_kernel(u_ref, halo_ref, w_ref, b_ref, o_ref, ext, *, tiles_per_b, rb):
    tm = u_ref.shape[0]
    first = (pl.program_id(0) % tiles_per_b) == 0
    ext[0:CONV_HALO, :] = jnp.where(first, 0.0, halo_ref[...])
    ext[CONV_HALO:, :] = u_ref[...]
    off = CONV_HALO - (CONV_WIDTH - 1)
    for lc in range(u_ref.shape[1] // LANES):
        cols = slice(lc * LANES, (lc + 1) * LANES)
        for r in range(tm // rb):
            acc = jnp.broadcast_to(b_ref[:, cols], (rb, LANES))
            for kk in range(CONV_WIDTH):
                acc = acc + w_ref[kk:kk + 1, cols] * ext[r * rb + off + kk:r * rb + off + kk + rb, cols]
            o_ref[r * rb:(r + 1) * rb, cols] = acc


def _dwconv(u, w, b, bsz, tm):
    n, d = u.shape
    tiles_per_b = n // bsz // tm
    hb = tm // CONV_HALO
    tc = LANES
    return pl.pallas_call(
        functools.partial(_dwconv_kernel, tiles_per_b=tiles_per_b, rb=64),
        out_shape=jax.ShapeDtypeStruct((n, d), F32),
        grid=(n // tm, d // tc),
        in_specs=[
            pl.BlockSpec((tm, tc), lambda i, j: (i, j)),
            pl.BlockSpec((CONV_HALO, tc), lambda i, j: (jnp.maximum(i * hb - 1, 0), j)),
            pl.BlockSpec((CONV_WIDTH, tc), lambda i, j: (0, j)),
            pl.BlockSpec((1, tc), lambda i, j: (0, j)),
        ],
        out_specs=pl.BlockSpec((tm, tc), lambda i, j: (i, j)),
        scratch_shapes=[pltpu.VMEM((tm + CONV_HALO, tc), F32)],
        compiler_params=_cparams(("arbitrary", "arbitrary")),
        name="conv_depthwise",
    )(u, u, w, b)


def _pw2_router_kernel(v_ref, lg_ref, lb_ref, w_ref, b_ref, x_ref, mod_ref, g_ref, wr_ref,
                       x_out_ref, h_ref, idx_ref, gate_ref):
    v = v_ref[...]
    mu = jnp.mean(v, axis=-1, keepdims=True)
    vc = v - mu
    var = jnp.mean(vc * vc, axis=-1, keepdims=True)
    y = vc * lax.rsqrt(var + EPS) * lg_ref[...] + lb_ref[...]
    y = (y * _sigmoid(y)).astype(BF16)
    m = mod_ref[0]
    x = x_ref[...] + m[2:3] * (_dot(y, w_ref[...]) + b_ref[...])
    x_out_ref[...] = x
    h = _norm_mod(x, g_ref[...], m[3:4], m[4:5])
    h_ref[...] = h
    logits = lax.dot_general(wr_ref[...], h, (((1,), (1,)), ((), ())),
                             preferred_element_type=F32, precision=lax.Precision.HIGHEST)
    ne = logits.shape[0]
    eidx = lax.broadcasted_iota(jnp.int32, logits.shape, 0)
    v1 = jnp.max(logits, axis=0, keepdims=True)
    i1 = jnp.min(jnp.where(logits == v1, eidx, ne), axis=0, keepdims=True)
    rest = jnp.where(eidx == i1, -jnp.inf, logits)
    v2 = jnp.max(rest, axis=0, keepdims=True)
    i2 = jnp.min(jnp.where(rest == v2, eidx, ne), axis=0, keepdims=True)
    e2 = jnp.exp(v2 - v1)
    den = 1.0 + e2
    idx_ref[0:1, :] = i1
    idx_ref[1:2, :] = i2
    gate_ref[0:1, :] = 1.0 / den
    gate_ref[1:2, :] = e2 / den


def _pw2_router(v, ln_g, ln_b, w, b, x2, mod, g, wr_t, bsz, tm):
    n, d = x2.shape
    ne = wr_t.shape[0]
    tiles_per_b = n // bsz // tm
    row = lambda i: (i, 0)
    const = lambda i: (0, 0)
    return pl.pallas_call(
        _pw2_router_kernel,
        out_shape=(jax.ShapeDtypeStruct((n, d), F32), jax.ShapeDtypeStruct((n, d), F32),
                   jax.ShapeDtypeStruct((TOP_K, n), jnp.int32), jax.ShapeDtypeStruct((TOP_K, n), F32)),
        grid=(n // tm,),
        in_specs=[
            pl.BlockSpec((tm, d), row),
            pl.BlockSpec((1, d), const),
            pl.BlockSpec((1, d), const),
            pl.BlockSpec((d, d), const),
            pl.BlockSpec((1, d), const),
            pl.BlockSpec((tm, d), row),
            pl.BlockSpec((1, 8, d), lambda i: (i // tiles_per_b, 0, 0)),
            pl.BlockSpec((1, d), const),
            pl.BlockSpec((ne, d), const),
        ],
        out_specs=(pl.BlockSpec((tm, d), row), pl.BlockSpec((tm, d), row),
                   pl.BlockSpec((TOP_K, tm), lambda i: (0, i)), pl.BlockSpec((TOP_K, tm), lambda i: (0, i))),
        compiler_params=_cparams(("arbitrary",)),
        name="conv_ln_pw2_router",
    )(v, ln_g, ln_b, w, b, x2, mod, g, wr_t)


def _moe_kernel(be_ref, na_ref, src_ref, dst_ref, h_hbm, wg_ref, wu_ref, wd_ref, y_hbm,
                xbuf, xb16, acc, gsem, ssem, *, n_assign, n_spare_blocks):
    tm = xb16.shape[0]
    nf = pl.num_programs(1)
    nblk = pl.num_programs(0)
    rpb = tm // LANES
    blk = pl.program_id(0)
    f = pl.program_id(1)
    na = na_ref[0]
    active = blk < na
    slot = blk % 2
    other = 1 - slot

    def gather_row(trow, col, buf, r):
        return pltpu.make_async_copy(h_hbm.at[pl.ds(src_ref[trow, col], 1)],
                                     xbuf.at[buf, pl.ds(r, 1)], gsem.at[buf])

    def scatter_row(trow, col, buf, r):
        return pltpu.make_async_copy(acc.at[buf, pl.ds(r, 1)], y_hbm.at[pl.ds(dst_ref[trow, col], 1)],
                                     ssem.at[buf])

    def wait_gather(buf):
        pltpu.make_async_copy(h_hbm.at[pl.ds(0, tm)], xbuf.at[buf], gsem.at[buf]).wait()

    def wait_scatter(buf):
        pltpu.make_async_copy(acc.at[buf], y_hbm.at[pl.ds(0, tm)], ssem.at[buf]).wait()

    def scatter_block(b, buf):
        def body(i, carry):
            scatter_row(b * rpb + i // LANES, i % LANES, buf, i).start()
            return carry
        lax.fori_loop(0, tm, body, 0)
        wait_scatter(buf)

    @pl.when(jnp.logical_and(blk == 0, f == 0))
    def _():
        def body(i, carry):
            gather_row(i // LANES, i % LANES, 0, i).start()
            return carry
        lax.fori_loop(0, tm, body, 0)
        acc[1] = jnp.zeros(acc.shape[1:], F32)
        spare_fill = [pltpu.make_async_copy(acc.at[1], y_hbm.at[pl.ds(n_assign + j * tm, tm)], ssem.at[1])
                      for j in range(n_spare_blocks)]
        for cp in spare_fill:
            cp.start()
        for cp in spare_fill:
            cp.wait()

    @pl.when(jnp.logical_and(active, f == 0))
    def _():
        wait_gather(slot)
        xb16[...] = xbuf[slot].astype(BF16)

        @pl.when(blk >= 1)
        def _():
            wait_scatter(slot)
        acc[slot] = jnp.zeros(acc.shape[1:], F32)

    @pl.when(active)
    def _():
        quarter = tm // nf
        qrows = quarter // LANES
        gbase = (blk + 1) * rpb + f * qrows
        sbase = jnp.where(blk == 0, nblk, blk - 1) * rpb + f * qrows
        for i in range(quarter):
            gather_row(gbase + i // LANES, i % LANES, other, f * quarter + i).start()
            scatter_row(sbase + i // LANES, i % LANES, other, f * quarter + i).start()
        xb = xb16[...]
        gt = _dot(xb, wg_ref[0])
        up = _dot(xb, wu_ref[0])
        acc[slot] += _dot((gt * _sigmoid(gt) * up).astype(BF16), wd_ref[0])

    @pl.when(jnp.logical_and(f == 0, blk == na))
    def _():
        wait_gather(slot)
        wait_scatter(slot)
        scatter_block(blk - 1, other)

    @pl.when(jnp.logical_and(jnp.logical_and(f == nf - 1, blk == nblk - 1), active))
    def _():
        wait_gather(other)
        wait_scatter(other)
        scatter_block(blk, slot)


def _moe(block_e, n_active, src_tab, dst_tab, h, wg, wu, wd, n_out, tm, tf):
    n, d = h.shape
    ne, _, fe = wg.shape
    nf = fe // tf
    nblk = dst_tab.shape[0] * LANES // tm - 1
    assert tm % (nf * LANES) == 0

    def wsel(blk, f, be, na, src, dst):
        return jnp.where(blk < na[0], f, nf - 1)

    grid_spec = pltpu.PrefetchScalarGridSpec(
        num_scalar_prefetch=4,
        grid=(nblk, nf),
        in_specs=[
            pl.BlockSpec(memory_space=pl.ANY),
            pl.BlockSpec((1, d, tf), lambda blk, f, be, na, src, dst: (be[blk], 0, wsel(blk, f, be, na, src, dst))),
            pl.BlockSpec((1, d, tf), lambda blk, f, be, na, src, dst: (be[blk], 0, wsel(blk, f, be, na, src, dst))),
            pl.BlockSpec((1, tf, d), lambda blk, f, be, na, src, dst: (be[blk], wsel(blk, f, be, na, src, dst), 0)),
        ],
        out_specs=pl.BlockSpec(memory_space=pl.ANY),
        scratch_shapes=[pltpu.VMEM((2, tm, d), F32), pltpu.VMEM((tm, d), BF16), pltpu.VMEM((2, tm, d), F32),
                        pltpu.SemaphoreType.DMA((2,)), pltpu.SemaphoreType.DMA((2,))],
    )
    return pl.pallas_call(
        functools.partial(_moe_kernel, n_assign=n * TOP_K, n_spare_blocks=ne),
        out_shape=jax.ShapeDtypeStruct((n_out, d), F32),
        grid_spec=grid_spec,
        compiler_params=_cparams(("arbitrary", "arbitrary")),
        name="moe_experts",
    )(block_e, n_active, src_tab, dst_tab, h, wg, wu, wd)


def _combine_kernel(y0_ref, y1_ref, x_ref, gates_ref, mod_ref, g_ref, o_ref):
    gate = mod_ref[0][5:6]
    gt = gates_ref[...]
    x = x_ref[...] + gate * (gt[:, 0:1] * y0_ref[...] + gt[:, 1:2] * y1_ref[...])
    o_ref[...] = x * lax.rsqrt(jnp.mean(x * x, axis=-1, keepdims=True) + EPS) * g_ref[...]


def _combine(y, x2, gates_t, mod, final_g, bsz, tm):
    n, d = x2.shape
    tiles_per_b = n // bsz // tm
    nt = n // tm
    row = lambda i: (i, 0)
    return pl.pallas_call(
        _combine_kernel,
        out_shape=jax.ShapeDtypeStruct((n, d), F32),
        grid=(nt,),
        in_specs=[
            pl.BlockSpec((tm, d), row),
            pl.BlockSpec((tm, d), lambda i: (nt + i, 0)),
            pl.BlockSpec((tm, d), row),
            pl.BlockSpec((tm, TOP_K), row),
            pl.BlockSpec((1, 8, d), lambda i: (i // tiles_per_b, 0, 0)),
            pl.BlockSpec((1, d), lambda i: (0, 0)),
        ],
        out_specs=pl.BlockSpec((tm, d), row),
        compiler_params=_cparams(("arbitrary",)),
        name="moe_combine_final_norm",
    )(y, y, x2, gates_t, mod, final_g)


def _routing(idx, ne, tm):
    n = idx.shape[1]
    a_tot = n * TOP_K
    flat_e = idx.T.reshape(-1)
    sizes = jnp.sum((flat_e[:, None] == jnp.arange(ne, dtype=jnp.int32)[None, :]).astype(jnp.int32), axis=0)
    padded = (sizes + tm - 1) // tm * tm
    pad_end = jnp.cumsum(padded)
    pad_start = pad_end - padded
    grp_start = jnp.cumsum(sizes) - sizes
    p = a_tot + ne * tm
    nblk = p // tm
    n_active = (pad_end[-1] // tm).astype(jnp.int32)
    blk_start = jnp.arange(nblk, dtype=jnp.int32) * tm
    block_e = jnp.sum((blk_start[:, None] >= pad_end[None, :]).astype(jnp.int32), axis=1)
    last_e = jnp.sum((jnp.maximum(pad_end[-1] - 1, 0) >= pad_end).astype(jnp.int32))
    block_e = jnp.where(jnp.arange(nblk) < n_active, block_e, last_e).astype(jnp.int32)
    order = jnp.argsort(flat_e).astype(jnp.int32)
    slot = jnp.arange(p, dtype=jnp.int32)
    slot_e = jnp.repeat(block_e, tm)
    within = slot - pad_start[slot_e]
    valid = within < sizes[slot_e]
    src_a = order[jnp.clip(grp_start[slot_e] + within, 0, a_tot - 1)]
    src_tok = jnp.where(valid, src_a // TOP_K, 0).astype(jnp.int32)
    spare = a_tot + jnp.cumsum(jnp.logical_not(valid).astype(jnp.int32)) - 1
    dst_row = jnp.where(valid, (src_a % TOP_K) * n + src_a // TOP_K, spare).astype(jnp.int32)
    src_tab = jnp.pad(src_tok, (0, tm)).reshape(-1, LANES)
    dst_tab = jnp.concatenate([dst_row, p + jnp.arange(tm, dtype=jnp.int32)]).reshape(-1, LANES)
    return block_e, n_active.reshape(1), src_tab, dst_tab, p + tm


def _rope_perm(d):
    perm = np.zeros((d,), np.int32)
    for h in range(d // HEAD_DIM):
        gi, r = divmod(h, 4)
        for dd in range(HEAD_DIM):
            new = gi * GROUP + (dd // HALF) * LANES + r * HALF + dd % HALF
            perm[new] = h * HEAD_DIM + dd
    return perm


def _mod_rows(mod_l, bsz, d):
    m = mod_l[:bsz].reshape(bsz, 6, d)
    return jnp.pad(m, ((0, 0), (0, 2), (0, 0)))


@jax.jit
def kernel(x, c, positions, w_ada, b_ada, norm_mix_g, norm_ffn_g, attn_w_qkv, attn_w_o, lam_q1, lam_k1,
           lam_q2, lam_k2, attn_subln_g, conv_w_pw1, conv_b_pw1, conv_w_dw, conv_b_dw, conv_ln_g,
           conv_ln_b, conv_w_pw2, conv_b_pw2, ffn_w_gate, ffn_w_up, ffn_w_down, moe_w_router,
           moe_w_gate, moe_w_up, moe_w_down, final_g):
    bsz, t, d = x.shape
    n = bsz * t
    depth = w_ada.shape[0]
    assert depth == 2 and d % GROUP == 0 and bsz <= 8
    tm = min(512, t)
    tk = min(256, t)
    x2 = x.reshape(n, d)

    c8 = jnp.pad(c, ((0, 8 - bsz), (0, 0)))
    mod = _ada(c8, w_ada, b_ada)
    mod0 = _mod_rows(mod[0], bsz, d)
    mod1 = _mod_rows(mod[1], bsz, d)

    inv_freq = ROPE_THETA ** (-jnp.arange(HALF, dtype=F32) / HALF)
    perm = _rope_perm(d)
    w_qkv = attn_w_qkv[0]
    wqt = w_qkv[:, perm].T.astype(BF16)
    wk = w_qkv[:, d + perm].astype(BF16)
    wvt = w_qkv[:, 2 * d:].T.astype(BF16)
    qt, k, vt = _qkv(x2, mod0, norm_mix_g[0:1], positions.reshape(n // tm, 1, tm),
                     inv_freq.reshape(HALF, 1), wqt, wk, wvt, bsz, tm, tk)
    lam_rows = jnp.concatenate([lam_q1[0:1], lam_k1[0:1], lam_q2[0:1], lam_k2[0:1]], axis=0)
    lambda_init = 0.8 - 0.6 * math.exp(-0.3 * 0)
    o = _attention(qt, k, vt, lam_rows, attn_subln_g[0].reshape(V_DIM, 1), tm, tk, lambda_init)
    x2 = _ffn(o, attn_w_o[0].astype(BF16), x2, mod0, norm_ffn_g[0:1], ffn_w_gate[0].astype(BF16),
              ffn_w_up[0].astype(BF16), ffn_w_down[0].astype(BF16), bsz, tm, tf=256)

    u = _pw1(x2, mod1, norm_mix_g[1:2], conv_w_pw1[0].astype(BF16), conv_b_pw1[0:1], bsz, tm)
    v = _dwconv(u, conv_w_dw[0], conv_b_dw[0:1], bsz, min(2048, t))
    x2, h, idx, gates = _pw2_router(v, conv_ln_g[0:1], conv_ln_b[0:1], conv_w_pw2[0].astype(BF16),
                                    conv_b_pw2[0:1], x2, mod1, norm_ffn_g[1:2], moe_w_router[0].T, bsz, tm)
    ne = moe_w_router.shape[-1]
    fe = moe_w_gate.shape[-1]
    block_e, n_active, src_tab, dst_tab, n_out = _routing(idx, ne, tm)
    tf_e = 1792 if fe % 1792 == 0 else fe
    y = _moe(block_e, n_active, src_tab, dst_tab, h, moe_w_gate[0].astype(BF16),
             moe_w_up[0].astype(BF16), moe_w_down[0].astype(BF16), n_out, tm, tf_e)
    out = _combine(y, x2, gates.T, mod1, final_g.reshape(1, d), bsz, tm)
    return out.reshape(bsz, t, d)
```

```python
import functools
import math

import numpy as np
import jax
import jax.numpy as jnp
from jax import lax
from jax.experimental import pallas as pl
from jax.experimental.pallas import tpu as pltpu

EPS = 1e-6
ROPE_THETA = 10000.0
HEAD_DIM = 64
V_DIM = 128
V_ROWS = V_DIM + 16
HALF = HEAD_DIM // 2
LANES = 128
GROUP = 4 * HEAD_DIM
CONV_WIDTH = 31
CONV_HALO = 32
TOP_K = 2
NEG = -1e30
VMEM_LIMIT = 56 * 1024 * 1024

BF16 = jnp.bfloat16
F32 = jnp.float32


def _cparams(sem):
    return pltpu.CompilerParams(dimension_semantics=sem, vmem_limit_bytes=VMEM_LIMIT)


def _dot(a, b):
    return jnp.dot(a, b, preferred_element_type=F32)


def _dot_nt(a, b):
    return lax.dot_general(a, b, (((1,), (1,)), ((), ())), preferred_element_type=F32)


def _norm_mod(x, g, shift, scale):
    y = x * lax.rsqrt(jnp.mean(x * x, axis=-1, keepdims=True) + EPS)
    return (y * g) * (1.0 + scale) + shift


def _sigmoid(x):
    return 1.0 / (1.0 + jnp.exp(-x))


BF16_SUBLANES = 16


def _cast_plan(w2d, nsteps, step_of):
    rows, cols = w2d.shape
    share = 1
    while rows % (nsteps // share) or (rows // (nsteps // share)) % BF16_SUBLANES:
        share *= 2
        assert share <= nsteps, (rows, nsteps)
    rb = rows // (nsteps // share)
    imap = lambda *ids: (step_of(*ids) // share, 0)
    return (pl.BlockSpec((rb, cols), imap), pl.BlockSpec((rb, cols), imap),
            jax.ShapeDtypeStruct((rows, cols), BF16))


def _cast_blocks(refs):
    k = len(refs) // 2
    for src, dst in zip(refs[:k], refs[k:]):
        dst[...] = src[...].astype(BF16)


def _ada_kernel(c_ref, w_ref, b_ref, o_ref):
    c = c_ref[...]
    ca = (c * _sigmoid(c)).astype(BF16)
    o_ref[0] = _dot(ca, w_ref[0].astype(BF16)) + b_ref[0]


def _ada(c8, w_ada, b_ada):
    depth, d, d6 = w_ada.shape
    tn = 1536
    return pl.pallas_call(
        _ada_kernel,
        out_shape=jax.ShapeDtypeStruct((depth, 8, d6), F32),
        grid=(depth, d6 // tn),
        in_specs=[
            pl.BlockSpec((8, d), lambda l, j: (0, 0)),
            pl.BlockSpec((1, d, tn), lambda l, j: (l, 0, j)),
            pl.BlockSpec((1, 1, tn), lambda l, j: (l, 0, j)),
        ],
        out_specs=pl.BlockSpec((1, 8, tn), lambda l, j: (l, 0, j)),
        compiler_params=_cparams(("arbitrary", "arbitrary")),
        name="ada_mod",
    )(c8, w_ada, b_ada.reshape(depth, 1, d6))


def _qkv_kernel(x_ref, mod_ref, g_ref, pos_ref, freq_ref, wqt_ref, wk_ref, wvt_ref, *rest, tk, ncast):
    qt_ref, k_ref, vt_ref = rest[ncast:ncast + 3]
    _cast_blocks(rest[:ncast] + rest[ncast + 3:])
    m = mod_ref[0]
    h = _norm_mod(x_ref[...], g_ref[...], m[0:1], m[1:2]).astype(BF16)
    d = x_ref.shape[1]
    tm = x_ref.shape[0]
    ang = freq_ref[...] * pos_ref[0].astype(F32)
    reps = LANES // HALF
    ct = jnp.concatenate([jnp.cos(ang)] * reps, axis=0)
    st = jnp.concatenate([jnp.sin(ang)] * reps, axis=0)
    c = ct.T
    s = st.T
    for gi in range(d // GROUP):
        y = _dot(h, wk_ref[:, gi * GROUP:(gi + 1) * GROUP])
        a = y[:, :LANES]
        b = y[:, LANES:]
        k_ref[:, gi * GROUP:gi * GROUP + LANES] = (a * c - b * s).astype(BF16)
        k_ref[:, gi * GROUP + LANES:(gi + 1) * GROUP] = (b * c + a * s).astype(BF16)
    mult = HEAD_DIM ** -0.5 * math.log2(math.e)
    for gi in range(d // GROUP):
        y = _dot_nt(wqt_ref[gi * GROUP:(gi + 1) * GROUP, :], h)
        a = y[:LANES]
        b = y[LANES:]
        qt_ref[0, gi * GROUP:gi * GROUP + LANES, :] = ((a * ct - b * st) * mult).astype(BF16)
        qt_ref[0, gi * GROUP + LANES:(gi + 1) * GROUP, :] = ((b * ct + a * st) * mult).astype(BF16)
    ones = jnp.ones((V_ROWS - V_DIM, tk), BF16)
    for gi in range(d // GROUP):
        vt = _dot_nt(wvt_ref[gi * GROUP:(gi + 1) * GROUP, :], h)
        for hh in range(GROUP // V_DIM):
            for ci in range(tm // tk):
                vt_ref[0, ci, 2 * gi + hh, 0:V_DIM, :] = (
                    vt[hh * V_DIM:(hh + 1) * V_DIM, ci * tk:(ci + 1) * tk].astype(BF16))
                vt_ref[0, ci, 2 * gi + hh, V_DIM:V_ROWS, :] = ones


def _qkv(x2, mod, g, pos_row, freq_col, wqt, wk, wvt, bsz, tm, tk, cast_weights):
    n, d = x2.shape
    t = n // bsz
    tiles_per_b = t // tm
    nh = d // V_DIM
    row = lambda i: (i, 0)
    const = lambda i: (0, 0)
    plans = [_cast_plan(w, n // tm, lambda i: i) for w in cast_weights]
    outs = pl.pallas_call(
        functools.partial(_qkv_kernel, tk=tk, ncast=len(plans)),
        out_shape=(jax.ShapeDtypeStruct((bsz, d, t), BF16), jax.ShapeDtypeStruct((n, d), BF16),
                   jax.ShapeDtypeStruct((bsz, t // tk, nh, V_ROWS, tk), BF16)) + tuple(p[2] for p in plans),
        grid=(n // tm,),
        in_specs=[
            pl.BlockSpec((tm, d), row),
            pl.BlockSpec((1, 8, d), lambda i: (i // tiles_per_b, 0, 0)),
            pl.BlockSpec((1, d), const),
            pl.BlockSpec((1, 1, tm), lambda i: (i, 0, 0)),
            pl.BlockSpec((HALF, 1), const),
            pl.BlockSpec((d, d), const),
            pl.BlockSpec((d, d), const),
            pl.BlockSpec((d, d), const),
        ] + [p[0] for p in plans],
        out_specs=(pl.BlockSpec((1, d, tm), lambda i: (i // tiles_per_b, 0, i % tiles_per_b)),
                   pl.BlockSpec((tm, d), row),
                   pl.BlockSpec((1, tm // tk, nh, V_ROWS, tk),
                                lambda i: (i // tiles_per_b, i % tiles_per_b, 0, 0, 0)))
        + tuple(p[1] for p in plans),
        compiler_params=_cparams(("arbitrary",)),
        name="qkv_rope",
    )(x2, mod, g, pos_row, freq_col, wqt, wk, wvt, *cast_weights)
    return outs[0], outs[1], outs[2], outs[3:]


def _attn_kernel(qt_ref, k_ref, vt_ref, lam_ref, g_ref, *rest, tk, lambda_init, ncast):
    o_ref = rest[ncast]
    s_a, s_b, mc_a, mc_b, acc = rest[2 * ncast + 1:]
    _cast_blocks(rest[:ncast] + rest[ncast + 1:2 * ncast + 1])
    tq = qt_ref.shape[2]
    nh = acc.shape[0]
    qi = pl.program_id(2)
    n_full = qi * (tq // tk)
    row = lax.broadcasted_iota(jnp.int32, (GROUP, 1), 0)
    head = (row % LANES) // HALF
    qf = qt_ref[0].astype(F32)
    qm = [jnp.where(head == hd, qf, 0.0).astype(BF16) for hd in range(nh)]

    def scores(kj, s_ref, mc_ref, masked, lo=0):
        kc = k_ref[pl.ds(pl.multiple_of(kj * tk, tk), tk), :]
        for hd in range(nh):
            s = _dot(kc, qm[hd][:, lo:])
            if masked:
                kidx = kj * tk + lax.broadcasted_iota(jnp.int32, s.shape, 0)
                qidx = qi * tq + lo + lax.broadcasted_iota(jnp.int32, s.shape, 1)
                s = jnp.where(kidx <= qidx, s, NEG)
            s_ref[hd, :, lo:] = s
            mc_ref[hd, :, lo:] = jnp.max(s, axis=0, keepdims=True)

    def softmax_pv(kj, s_ref, mc_ref, ms, first, lo=0):
        out = []
        for hd in range(nh):
            vc = vt_ref[0, kj, hd // 2]
            s = s_ref[hd, :, lo:]
            mc = mc_ref[hd, :, lo:]
            m_new = mc if first else jnp.maximum(ms[hd][:, lo:], mc)
            pv = _dot(vc, jnp.exp2(s - m_new).astype(BF16))
            if first:
                acc[hd] = pv
            else:
                acc[hd, :, lo:] = jnp.exp2(ms[hd][:, lo:] - m_new) * acc[hd, :, lo:] + pv
            out.append(m_new if lo == 0 else jnp.concatenate([ms[hd][:, :lo], m_new], axis=1))
        return tuple(out)

    assert tq == 2 * tk
    scores(n_full, s_a, mc_a, True)
    scores(n_full + 1, s_b, mc_b, True, lo=tk)
    ms = softmax_pv(n_full, s_a, mc_a, None, True)
    scores(0, s_a, mc_a, False)
    ms = softmax_pv(n_full + 1, s_b, mc_b, ms, False, lo=tk)

    def body(jj, ms):
        j = 2 * jj
        scores(j + 1, s_b, mc_b, False)
        ms = softmax_pv(j, s_a, mc_a, ms, False)
        scores(j + 2, s_a, mc_a, False)
        return softmax_pv(j + 1, s_b, mc_b, ms, False)

    ms = lax.fori_loop(0, n_full // 2 - 1, body, ms)

    @pl.when(n_full > 0)
    def _():
        scores(n_full - 1, s_b, mc_b, False)
        ms2 = softmax_pv(n_full - 2, s_a, mc_a, ms, False)
        softmax_pv(n_full - 1, s_b, mc_b, ms2, False)

    lp = lam_ref[...]
    lam = (jnp.exp(jnp.sum(lp[0:1] * lp[1:2], axis=-1, keepdims=True))
           - jnp.exp(jnp.sum(lp[2:3] * lp[3:4], axis=-1, keepdims=True)) + lambda_init)
    for pr in range(nh // 2):
        a1 = acc[2 * pr]
        a2 = acc[2 * pr + 1]
        o = a1[:V_DIM] / a1[V_DIM:V_DIM + 1] - lam * (a2[:V_DIM] / a2[V_DIM:V_DIM + 1])
        o = o * lax.rsqrt(jnp.mean(o * o, axis=0, keepdims=True) + EPS)
        o = (o * g_ref[...]) * (1.0 - lambda_init)
        o_ref[:, pr * V_DIM:(pr + 1) * V_DIM] = o.T.astype(BF16)


def _attention(qt, k, vt, lam_rows, g_col, tq, tk, lambda_init, cast_weights):
    bsz, d, t = qt.shape
    n = bsz * t
    nq = t // tq
    nh = GROUP // HEAD_DIM
    ng = d // GROUP
    plans = [_cast_plan(w, bsz * ng * nq, lambda b, g, i: (b * ng + g) * nq + i) for w in cast_weights]
    outs = pl.pallas_call(
        functools.partial(_attn_kernel, tk=tk, lambda_init=lambda_init, ncast=len(plans)),
        out_shape=(jax.ShapeDtypeStruct((n, d), BF16),) + tuple(pln[2] for pln in plans),
        grid=(bsz, ng, nq),
        in_specs=[
            pl.BlockSpec((1, GROUP, tq), lambda b, g, i: (b, g, i)),
            pl.BlockSpec((t, GROUP), lambda b, g, i: (b, g)),
            pl.BlockSpec((1, t // tk, nh // 2, V_ROWS, tk), lambda b, g, i: (b, 0, g, 0, 0)),
            pl.BlockSpec((4, HEAD_DIM), lambda b, g, i: (0, 0)),
            pl.BlockSpec((V_DIM, 1), lambda b, g, i: (0, 0)),
        ] + [pln[0] for pln in plans],
        out_specs=(pl.BlockSpec((tq, GROUP), lambda b, g, i: (b * nq + i, g)),) + tuple(pln[1] for pln in plans),
        scratch_shapes=[pltpu.VMEM((nh, tk, tq), F32), pltpu.VMEM((nh, tk, tq), F32),
                        pltpu.VMEM((nh, 1, tq), F32), pltpu.VMEM((nh, 1, tq), F32),
                        pltpu.VMEM((nh, V_ROWS, tq), F32)],
        compiler_params=_cparams(("arbitrary", "arbitrary", "arbitrary")),
        name="diff_attn",
    )(qt, k, vt, lam_rows, g_col, *cast_weights)
    return outs[0], outs[1:]


def _ffn_kernel(a_ref, wo_ref, x_ref, mod_ref, g_ref, wg_ref, wu_ref, wd_ref, o_ref, *, tf):
    m = mod_ref[0]
    x = x_ref[...] + m[2:3] * _dot(a_ref[...], wo_ref[...])
    h = _norm_mod(x, g_ref[...], m[3:4], m[4:5]).astype(BF16)
    f = wg_ref.shape[1]
    acc = None
    for ci in range(f // tf):
        sl = slice(ci * tf, (ci + 1) * tf)
        gt = _dot(h, wg_ref[:, sl])
        up = _dot(h, wu_ref[:, sl])
        a = (gt * _sigmoid(gt) * up).astype(BF16)
        part = _dot(a, wd_ref[sl, :])
        acc = part if acc is None else acc + part
    o_ref[...] = x + m[5:6] * acc


def _ffn(a, wo, x2, mod, g, wg, wu, wd, bsz, tm, tf):
    n, d = x2.shape
    f = wg.shape[1]
    tiles_per_b = n // bsz // tm
    row = lambda i: (i, 0)
    const = lambda i: (0, 0)
    return pl.pallas_call(
        functools.partial(_ffn_kernel, tf=tf),
        out_shape=jax.ShapeDtypeStruct((n, d), F32),
        grid=(n // tm,),
        in_specs=[
            pl.BlockSpec((tm, d), row),
            pl.BlockSpec((d, d), const),
            pl.BlockSpec((tm, d), row),
            pl.BlockSpec((1, 8, d), lambda i: (i // tiles_per_b, 0, 0)),
            pl.BlockSpec((1, d), const),
            pl.BlockSpec((d, f), const),
            pl.BlockSpec((d, f), const),
            pl.BlockSpec((f, d), const),
        ],
        out_specs=pl.BlockSpec((tm, d), row),
        compiler_params=_cparams(("arbitrary",)),
        name="oproj_dense_swiglu",
    )(a, wo, x2, mod, g, wg, wu, wd)


def _pw1_kernel(x_ref, mod_ref, g_ref, w_ref, b_ref, o_ref, *, tn):
    m = mod_ref[0]
    h = _norm_mod(x_ref[...], g_ref[...], m[0:1], m[1:2]).astype(BF16)
    d = x_ref.shape[1]
    for ci in range(d // tn):
        a = _dot(h, w_ref[:, ci * tn:(ci + 1) * tn]) + b_ref[:, ci * tn:(ci + 1) * tn]
        gt = _dot(h, w_ref[:, d + ci * tn:d + (ci + 1) * tn]) + b_ref[:, d + ci * tn:d + (ci + 1) * tn]
        o_ref[:, ci * tn:(ci + 1) * tn] = a * _sigmoid(gt)


def _pw1(x2, mod, g, w, b, bsz, tm):
    n, d = x2.shape
    tiles_per_b = n // bsz // tm
    row = lambda i: (i, 0)
    const = lambda i: (0, 0)
    return pl.pallas_call(
        functools.partial(_pw1_kernel, tn=256),
        out_shape=jax.ShapeDtypeStruct((n, d), F32),
        grid=(n // tm,),
        in_specs=[
            pl.BlockSpec((tm, d), row),
            pl.BlockSpec((1, 8, d), lambda i: (i // tiles_per_b, 0, 0)),
            pl.BlockSpec((1, d), const),
            pl.BlockSpec((d, 2 * d), const),
            pl.BlockSpec((1, 2 * d), const),
        ],
        out_specs=pl.BlockSpec((tm, d), row),
        compiler_params=_cparams(("arbitrary",)),
        name="conv_pw1_glu",
    )(x2, mod, g, w, b)


def _dwconv_kernel(u_ref, halo_ref, w_ref, b_ref, o_ref, ext, *, tiles_per_b, rb):
    tm = u_ref.shape[0]
    first = (pl.program_id(0) % tiles_per_b) == 0
    ext[0:CONV_HALO, :] = jnp.where(first, 0.0, halo_ref[...])
    ext[CONV_HALO:, :] = u_ref[...]
    off = CONV_HALO - (CONV_WIDTH - 1)
    for lc in range(u_ref.shape[1] // LANES):
        cols = slice(lc * LANES, (lc + 1) * LANES)
        for r in range(tm // rb):
            acc = jnp.broadcast_to(b_ref[:, cols], (rb, LANES))
            for kk in range(CONV_WIDTH):
                acc = acc + w_ref[kk:kk + 1, cols] * ext[r * rb + off + kk:r * rb + off + kk + rb, cols]
            o_ref[r * rb:(r + 1) * rb, cols] = acc


def _dwconv(u, w, b, bsz, tm):
    n, d = u.shape
    tiles_per_b = n // bsz // tm
    hb = tm // CONV_HALO
    tc = LANES
    return pl.pallas_call(
        functools.partial(_dwconv_kernel, tiles_per_b=tiles_per_b, rb=64),
        out_shape=jax.ShapeDtypeStruct((n, d), F32),
        grid=(n // tm, d // tc),
        in_specs=[
            pl.BlockSpec((tm, tc), lambda i, j: (i, j)),
            pl.BlockSpec((CONV_HALO, tc), lambda i, j: (jnp.maximum(i * hb - 1, 0), j)),
            pl.BlockSpec((CONV_WIDTH, tc), lambda i, j: (0, j)),
            pl.BlockSpec((1, tc), lambda i, j: (0, j)),
        ],
        out_specs=pl.BlockSpec((tm, tc), lambda i, j: (i, j)),
        scratch_shapes=[pltpu.VMEM((tm + CONV_HALO, tc), F32)],
        compiler_params=_cparams(("arbitrary", "arbitrary")),
        name="conv_depthwise",
    )(u, u, w, b)


def _pw2_router_kernel(v_ref, lg_ref, lb_ref, w_ref, b_ref, x_ref, mod_ref, g_ref, wr_ref,
                       x_out_ref, h_ref, idx_ref, gate_ref):
    v = v_ref[...]
    mu = jnp.mean(v, axis=-1, keepdims=True)
    vc = v - mu
    var = jnp.mean(vc * vc, axis=-1, keepdims=True)
    y = vc * lax.rsqrt(var + EPS) * lg_ref[...] + lb_ref[...]
    y = (y * _sigmoid(y)).astype(BF16)
    m = mod_ref[0]
    x = x_ref[...] + m[2:3] * (_dot(y, w_ref[...]) + b_ref[...])
    x_out_ref[...] = x
    h = _norm_mod(x, g_ref[...], m[3:4], m[4:5])
    h_ref[...] = h
    logits = lax.dot_general(wr_ref[...], h, (((1,), (1,)), ((), ())),
                             preferred_element_type=F32, precision=lax.Precision.HIGHEST)
    ne = logits.shape[0]
    eidx = lax.broadcasted_iota(jnp.int32, logits.shape, 0)
    v1 = jnp.max(logits, axis=0, keepdims=True)
    i1 = jnp.min(jnp.where(logits == v1, eidx, ne), axis=0, keepdims=True)
    rest = jnp.where(eidx == i1, -jnp.inf, logits)
    v2 = jnp.max(rest, axis=0, keepdims=True)
    i2 = jnp.min(jnp.where(rest == v2, eidx, ne), axis=0, keepdims=True)
    e2 = jnp.exp(v2 - v1)
    den = 1.0 + e2
    idx_ref[0:1, :] = i1
    idx_ref[1:2, :] = i2
    gate_ref[0:1, :] = 1.0 / den
    gate_ref[1:2, :] = e2 / den


def _pw2_router(v, ln_g, ln_b, w, b, x2, mod, g, wr_t, bsz, tm):
    n, d = x2.shape
    ne = wr_t.shape[0]
    tiles_per_b = n // bsz // tm
    row = lambda i: (i, 0)
    const = lambda i: (0, 0)
    return pl.pallas_call(
        _pw2_router_kernel,
        out_shape=(jax.ShapeDtypeStruct((n, d), F32), jax.ShapeDtypeStruct((n, d), F32),
                   jax.ShapeDtypeStruct((TOP_K, n), jnp.int32), jax.ShapeDtypeStruct((TOP_K, n), F32)),
        grid=(n // tm,),
        in_specs=[
            pl.BlockSpec((tm, d), row),
            pl.BlockSpec((1, d), const),
            pl.BlockSpec((1, d), const),
            pl.BlockSpec((d, d), const),
            pl.BlockSpec((1, d), const),
            pl.BlockSpec((tm, d), row),
            pl.BlockSpec((1, 8, d), lambda i: (i // tiles_per_b, 0, 0)),
            pl.BlockSpec((1, d), const),
            pl.BlockSpec((ne, d), const),
        ],
        out_specs=(pl.BlockSpec((tm, d), row), pl.BlockSpec((tm, d), row),
                   pl.BlockSpec((TOP_K, tm), lambda i: (0, i)), pl.BlockSpec((TOP_K, tm), lambda i: (0, i))),
        compiler_params=_cparams(("arbitrary",)),
        name="conv_ln_pw2_router",
    )(v, ln_g, ln_b, w, b, x2, mod, g, wr_t)


def _moe_kernel(be_ref, na_ref, src_ref, dst_ref, h_hbm, wg_ref, wu_ref, wd_ref, y_hbm,
                xbuf, xb16, acc, gsem, ssem, *, n_assign, n_spare_blocks):
    tm = xb16.shape[0]
    nf = pl.num_programs(1)
    nblk = pl.num_programs(0)
    rpb = tm // LANES
    blk = pl.program_id(0)
    f = pl.program_id(1)
    na = na_ref[0]
    active = blk < na
    slot = blk % 2
    other = 1 - slot

    def gather_row(trow, col, buf, r):
        return pltpu.make_async_copy(h_hbm.at[pl.ds(src_ref[trow, col], 1)],
                                     xbuf.at[buf, pl.ds(r, 1)], gsem.at[buf])

    def scatter_row(trow, col, buf, r):
        return pltpu.make_async_copy(acc.at[buf, pl.ds(r, 1)], y_hbm.at[pl.ds(dst_ref[trow, col], 1)],
                                     ssem.at[buf])

    def wait_gather(buf):
        pltpu.make_async_copy(h_hbm.at[pl.ds(0, tm)], xbuf.at[buf], gsem.at[buf]).wait()

    def wait_scatter(buf):
        pltpu.make_async_copy(acc.at[buf], y_hbm.at[pl.ds(0, tm)], ssem.at[buf]).wait()

    def scatter_block(b, buf):
        def body(i, carry):
            scatter_row(b * rpb + i // LANES, i % LANES, buf, i).start()
            return carry
        lax.fori_loop(0, tm, body, 0)
        wait_scatter(buf)

    @pl.when(jnp.logical_and(blk == 0, f == 0))
    def _():
        def body(i, carry):
            gather_row(i // LANES, i % LANES, 0, i).start()
            return carry
        lax.fori_loop(0, tm, body, 0)
        acc[1] = jnp.zeros(acc.shape[1:], F32)
        spare_fill = [pltpu.make_async_copy(acc.at[1], y_hbm.at[pl.ds(n_assign + j * tm, tm)], ssem.at[1])
                      for j in range(n_spare_blocks)]
        for cp in spare_fill:
            cp.start()
        for cp in spare_fill:
            cp.wait()

    @pl.when(jnp.logical_and(active, f == 0))
    def _():
        wait_gather(slot)
        xb16[...] = xbuf[slot].astype(BF16)

        @pl.when(blk >= 1)
        def _():
            wait_scatter(slot)
        acc[slot] = jnp.zeros(acc.shape[1:], F32)

    @pl.when(active)
    def _():
        quarter = tm // nf
        qrows = quarter // LANES
        gbase = (blk + 1) * rpb + f * qrows
        sbase = jnp.where(blk == 0, nblk, blk - 1) * rpb + f * qrows
        for i in range(quarter):
            gather_row(gbase + i // LANES, i % LANES, other, f * quarter + i).start()
            scatter_row(sbase + i // LANES, i % LANES, other, f * quarter + i).start()
        xb = xb16[...]
        gt = _dot(xb, wg_ref[0])
        up = _dot(xb, wu_ref[0])
        acc[slot] += _dot((gt * _sigmoid(gt) * up).astype(BF16), wd_ref[0])

    @pl.when(jnp.logical_and(f == 0, blk == na))
    def _():
        wait_gather(slot)
        wait_scatter(slot)
        scatter_block(blk - 1, other)

    @pl.when(jnp.logical_and(jnp.logical_and(f == nf - 1, blk == nblk - 1), active))
    def _():
        wait_gather(other)
        wait_scatter(other)
        scatter_block(blk, slot)


def _moe(block_e, n_active, src_tab, dst_tab, h, wg, wu, wd, n_out, tm, tf):
    n, d = h.shape
    ne, _, fe = wg.shape
    nf = fe // tf
    nblk = dst_tab.shape[0] * LANES // tm - 1
    assert tm % (nf * LANES) == 0

    def wsel(blk, f, be, na, src, dst):
        return jnp.where(blk < na[0], f, nf - 1)

    grid_spec = pltpu.PrefetchScalarGridSpec(
        num_scalar_prefetch=4,
        grid=(nblk, nf),
        in_specs=[
            pl.BlockSpec(memory_space=pl.ANY),
            pl.BlockSpec((1, d, tf), lambda blk, f, be, na, src, dst: (be[blk], 0, wsel(blk, f, be, na, src, dst))),
            pl.BlockSpec((1, d, tf), lambda blk, f, be, na, src, dst: (be[blk], 0, wsel(blk, f, be, na, src, dst))),
            pl.BlockSpec((1, tf, d), lambda blk, f, be, na, src, dst: (be[blk], wsel(blk, f, be, na, src, dst), 0)),
        ],
        out_specs=pl.BlockSpec(memory_space=pl.ANY),
        scratch_shapes=[pltpu.VMEM((2, tm, d), F32), pltpu.VMEM((tm, d), BF16), pltpu.VMEM((2, tm, d), F32),
                        pltpu.SemaphoreType.DMA((2,)), pltpu.SemaphoreType.DMA((2,))],
    )
    return pl.pallas_call(
        functools.partial(_moe_kernel, n_assign=n * TOP_K, n_spare_blocks=ne),
        out_shape=jax.ShapeDtypeStruct((n_out, d), F32),
        grid_spec=grid_spec,
        compiler_params=_cparams(("arbitrary", "arbitrary")),
        name="moe_experts",
    )(block_e, n_active, src_tab, dst_tab, h, wg, wu, wd)


def _combine_kernel(y0_ref, y1_ref, x_ref, gates_ref, mod_ref, g_ref, o_ref):
    gate = mod_ref[0][5:6]
    gt = gates_ref[...]
    x = x_ref[...] + gate * (gt[:, 0:1] * y0_ref[...] + gt[:, 1:2] * y1_ref[...])
    o_ref[...] = x * lax.rsqrt(jnp.mean(x * x, axis=-1, keepdims=True) + EPS) * g_ref[...]


def _combine(y, x2, gates_t, mod, final_g, bsz, tm):
    n, d = x2.shape
    tiles_per_b = n // bsz // tm
    nt = n // tm
    row = lambda i: (i, 0)
    return pl.pallas_call(
        _combine_kernel,
        out_shape=jax.ShapeDtypeStruct((n, d), F32),
        grid=(nt,),
        in_specs=[
            pl.BlockSpec((tm, d), row),
            pl.BlockSpec((tm, d), lambda i: (nt + i, 0)),
            pl.BlockSpec((tm, d), row),
            pl.BlockSpec((tm, TOP_K), row),
            pl.BlockSpec((1, 8, d), lambda i: (i // tiles_per_b, 0, 0)),
            pl.BlockSpec((1, d), lambda i: (0, 0)),
        ],
        out_specs=pl.BlockSpec((tm, d), row),
        compiler_params=_cparams(("arbitrary",)),
        name="moe_combine_final_norm",
    )(y, y, x2, gates_t, mod, final_g)


def _routing(idx, ne, tm):
    n = idx.shape[1]
    a_tot = n * TOP_K
    flat_e = idx.T.reshape(-1)
    sizes = jnp.sum((flat_e[:, None] == jnp.arange(ne, dtype=jnp.int32)[None, :]).astype(jnp.int32), axis=0)
    padded = (sizes + tm - 1) // tm * tm
    pad_end = jnp.cumsum(padded)
    pad_start = pad_end - padded
    grp_start = jnp.cumsum(sizes) - sizes
    p = a_tot + ne * tm
    nblk = p // tm
    n_active = (pad_end[-1] // tm).astype(jnp.int32)
    blk_start = jnp.arange(nblk, dtype=jnp.int32) * tm
    block_e = jnp.sum((blk_start[:, None] >= pad_end[None, :]).astype(jnp.int32), axis=1)
    last_e = jnp.sum((jnp.maximum(pad_end[-1] - 1, 0) >= pad_end).astype(jnp.int32))
    block_e = jnp.where(jnp.arange(nblk) < n_active, block_e, last_e).astype(jnp.int32)
    order = jnp.argsort(flat_e).astype(jnp.int32)
    slot = jnp.arange(p, dtype=jnp.int32)
    slot_e = jnp.repeat(block_e, tm)
    within = slot - pad_start[slot_e]
    valid = within < sizes[slot_e]
    src_a = order[jnp.clip(grp_start[slot_e] + within, 0, a_tot - 1)]
    src_tok = jnp.where(valid, src_a // TOP_K, 0).astype(jnp.int32)
    spare = a_tot + jnp.cumsum(jnp.logical_not(valid).astype(jnp.int32)) - 1
    dst_row = jnp.where(valid, (src_a % TOP_K) * n + src_a // TOP_K, spare).astype(jnp.int32)
    src_tab = jnp.pad(src_tok, (0, tm)).reshape(-1, LANES)
    dst_tab = jnp.concatenate([dst_row, p + jnp.arange(tm, dtype=jnp.int32)]).reshape(-1, LANES)
    return block_e, n_active.reshape(1), src_tab, dst_tab, p + tm


def _rope_perm(d):
    perm = np.zeros((d,), np.int32)
    for h in range(d // HEAD_DIM):
        gi, r = divmod(h, 4)
        for dd in range(HEAD_DIM):
            new = gi * GROUP + (dd // HALF) * LANES + r * HALF + dd % HALF
            perm[new] = h * HEAD_DIM + dd
    return perm


def _mod_rows(mod_l, bsz, d):
    m = mod_l[:bsz].reshape(bsz, 6, d)
    return jnp.pad(m, ((0, 0), (0, 2), (0, 0)))


@jax.jit
def kernel(x, c, positions, w_ada, b_ada, norm_mix_g, norm_ffn_g, attn_w_qkv, attn_w_o, lam_q1, lam_k1,
           lam_q2, lam_k2, attn_subln_g, conv_w_pw1, conv_b_pw1, conv_w_dw, conv_b_dw, conv_ln_g,
           conv_ln_b, conv_w_pw2, conv_b_pw2, ffn_w_gate, ffn_w_up, ffn_w_down, moe_w_router,
           moe_w_gate, moe_w_up, moe_w_down, final_g):
    bsz, t, d = x.shape
    n = bsz * t
    depth = w_ada.shape[0]
    assert depth == 2 and d % GROUP == 0 and bsz <= 8
    tm = min(512, t)
    tk = min(256, t)
    x2 = x.reshape(n, d)

    c8 = jnp.pad(c, ((0, 8 - bsz), (0, 0)))
    mod = _ada(c8, w_ada, b_ada)
    mod0 = _mod_rows(mod[0], bsz, d)
    mod1 = _mod_rows(mod[1], bsz, d)

    inv_freq = ROPE_THETA ** (-jnp.arange(HALF, dtype=F32) / HALF)
    perm = _rope_perm(d)
    w_qkv = attn_w_qkv[0]
    wqt = w_qkv[:, perm].T.astype(BF16)
    wk = w_qkv[:, d + perm].astype(BF16)
    wvt = w_qkv[:, 2 * d:].T.astype(BF16)
    qt, k, vt, (ffn_wg, ffn_wu, ffn_wd, w_o, w_pw1, w_pw2) = _qkv(
        x2, mod0, norm_mix_g[0:1], positions.reshape(n // tm, 1, tm), inv_freq.reshape(HALF, 1),
        wqt, wk, wvt, bsz, tm, tk,
        [ffn_w_gate[0], ffn_w_up[0], ffn_w_down[0], attn_w_o[0], conv_w_pw1[0], conv_w_pw2[0]])
    lam_rows = jnp.concatenate([lam_q1[0:1], lam_k1[0:1], lam_q2[0:1], lam_k2[0:1]], axis=0)
    lambda_init = 0.8 - 0.6 * math.exp(-0.3 * 0)
    ne, _, fe = moe_w_gate.shape[1:]
    o, (moe_wg, moe_wu, moe_wd) = _attention(
        qt, k, vt, lam_rows, attn_subln_g[0].reshape(V_DIM, 1), tm, tk, lambda_init,
        [moe_w_gate[0].reshape(ne * d, fe), moe_w_up[0].reshape(ne * d, fe), moe_w_down[0].reshape(ne * fe, d)])
    x2 = _ffn(o, w_o, x2, mod0, norm_ffn_g[0:1], ffn_wg, ffn_wu, ffn_wd, bsz, tm, tf=256)

    u = _pw1(x2, mod1, norm_mix_g[1:2], w_pw1, conv_b_pw1[0:1], bsz, tm)
    v = _dwconv(u, conv_w_dw[0], conv_b_dw[0:1], bsz, min(2048, t))
    x2, h, idx, gates = _pw2_router(v, conv_ln_g[0:1], conv_ln_b[0:1], w_pw2,
                                    conv_b_pw2[0:1], x2, mod1, norm_ffn_g[1:2], moe_w_router[0].T, bsz, tm)
    block_e, n_active, src_tab, dst_tab, n_out = _routing(idx, ne, tm)
    tf_e = 1792 if fe % 1792 == 0 else fe
    y = _moe(block_e, n_active, src_tab, dst_tab, h, moe_wg.reshape(ne, d, fe), moe_wu.reshape(ne, d, fe),
             moe_wd.reshape(ne, fe, d), n_out, tm, tf_e)
    out = _combine(y, x2, gates.T, mod1, final_g.reshape(1, d), bsz, tm)
    return out.reshape(bsz, t, d)
```

```python
import functools
import math

import numpy as np
import jax
import jax.numpy as jnp
from jax import lax
from jax.experimental import pallas as pl
from jax.experimental.pallas import tpu as pltpu

EPS = 1e-6
ROPE_THETA = 10000.0
HEAD_DIM = 64
V_DIM = 128
V_ROWS = V_DIM + 16
HALF = HEAD_DIM // 2
LANES = 128
GROUP = 4 * HEAD_DIM
CONV_WIDTH = 31
CONV_HALO = 32
TOP_K = 2
NEG = -1e30
VMEM_LIMIT = 56 * 1024 * 1024

BF16 = jnp.bfloat16
F32 = jnp.float32


def _cparams(sem):
    return pltpu.CompilerParams(dimension_semantics=sem, vmem_limit_bytes=VMEM_LIMIT)


def _dot(a, b):
    return jnp.dot(a, b, preferred_element_type=F32)


def _dot_nt(a, b):
    return lax.dot_general(a, b, (((1,), (1,)), ((), ())), preferred_element_type=F32)


def _norm_mod(x, g, shift, scale):
    y = x * lax.rsqrt(jnp.mean(x * x, axis=-1, keepdims=True) + EPS)
    return (y * g) * (1.0 + scale) + shift


def _sigmoid(x):
    return 1.0 / (1.0 + jnp.exp(-x))


BF16_SUBLANES = 16


def _cast_plan(w2d, nsteps, step_of):
    rows, cols = w2d.shape
    share = 1
    while rows % (nsteps // share) or (rows // (nsteps // share)) % BF16_SUBLANES:
        share *= 2
        assert share <= nsteps, (rows, nsteps)
    rb = rows // (nsteps // share)
    imap = lambda *ids: (step_of(*ids) // share, 0)
    return (pl.BlockSpec((rb, cols), imap), pl.BlockSpec((rb, cols), imap),
            jax.ShapeDtypeStruct((rows, cols), BF16))


def _cast_blocks(refs):
    k = len(refs) // 2
    for src, dst in zip(refs[:k], refs[k:]):
        dst[...] = src[...].astype(BF16)


def _ada_kernel(c_ref, w_ref, b_ref, o_ref):
    c = c_ref[...]
    ca = (c * _sigmoid(c)).astype(BF16)
    o_ref[0] = _dot(ca, w_ref[0].astype(BF16)) + b_ref[0]


def _ada(c8, w_ada, b_ada):
    depth, d, d6 = w_ada.shape
    tn = 1536
    return pl.pallas_call(
        _ada_kernel,
        out_shape=jax.ShapeDtypeStruct((depth, 8, d6), F32),
        grid=(depth, d6 // tn),
        in_specs=[
            pl.BlockSpec((8, d), lambda l, j: (0, 0)),
            pl.BlockSpec((1, d, tn), lambda l, j: (l, 0, j)),
            pl.BlockSpec((1, 1, tn), lambda l, j: (l, 0, j)),
        ],
        out_specs=pl.BlockSpec((1, 8, tn), lambda l, j: (l, 0, j)),
        compiler_params=_cparams(("arbitrary", "arbitrary")),
        name="ada_mod",
    )(c8, w_ada, b_ada.reshape(depth, 1, d6))


def _qkv_kernel(x_ref, mod_ref, g_ref, pos_ref, freq_ref, wqt_ref, wk_ref, wvt_ref, *rest, tk, ncast):
    qt_ref, k_ref, vt_ref = rest[ncast:ncast + 3]
    _cast_blocks(rest[:ncast] + rest[ncast + 3:])
    m = mod_ref[0]
    h = _norm_mod(x_ref[...], g_ref[...], m[0:1], m[1:2]).astype(BF16)
    d = x_ref.shape[1]
    tm = x_ref.shape[0]
    ang = freq_ref[...] * pos_ref[0].astype(F32)
    reps = LANES // HALF
    ct = jnp.concatenate([jnp.cos(ang)] * reps, axis=0)
    st = jnp.concatenate([jnp.sin(ang)] * reps, axis=0)
    c = ct.T
    s = st.T
    for gi in range(d // GROUP):
        y = _dot(h, wk_ref[:, gi * GROUP:(gi + 1) * GROUP])
        a = y[:, :LANES]
        b = y[:, LANES:]
        k_ref[:, gi * GROUP:gi * GROUP + LANES] = (a * c - b * s).astype(BF16)
        k_ref[:, gi * GROUP + LANES:(gi + 1) * GROUP] = (b * c + a * s).astype(BF16)
    mult = HEAD_DIM ** -0.5 * math.log2(math.e)
    for gi in range(d // GROUP):
        y = _dot_nt(wqt_ref[gi * GROUP:(gi + 1) * GROUP, :], h)
        a = y[:LANES]
        b = y[LANES:]
        qt_ref[0, gi * GROUP:gi * GROUP + LANES, :] = ((a * ct - b * st) * mult).astype(BF16)
        qt_ref[0, gi * GROUP + LANES:(gi + 1) * GROUP, :] = ((b * ct + a * st) * mult).astype(BF16)
    ones = jnp.ones((V_ROWS - V_DIM, tk), BF16)
    for gi in range(d // GROUP):
        vt = _dot_nt(wvt_ref[gi * GROUP:(gi + 1) * GROUP, :], h)
        for hh in range(GROUP // V_DIM):
            for ci in range(tm // tk):
                vt_ref[0, ci, 2 * gi + hh, 0:V_DIM, :] = (
                    vt[hh * V_DIM:(hh + 1) * V_DIM, ci * tk:(ci + 1) * tk].astype(BF16))
                vt_ref[0, ci, 2 * gi + hh, V_DIM:V_ROWS, :] = ones


def _qkv(x2, mod, g, pos_row, freq_col, wqt, wk, wvt, bsz, tm, tk, cast_weights):
    n, d = x2.shape
    t = n // bsz
    tiles_per_b = t // tm
    nh = d // V_DIM
    row = lambda i: (i, 0)
    const = lambda i: (0, 0)
    plans = [_cast_plan(w, n // tm, lambda i: i) for w in cast_weights]
    outs = pl.pallas_call(
        functools.partial(_qkv_kernel, tk=tk, ncast=len(plans)),
        out_shape=(jax.ShapeDtypeStruct((bsz, d, t), BF16), jax.ShapeDtypeStruct((n, d), BF16),
                   jax.ShapeDtypeStruct((bsz, t // tk, nh, V_ROWS, tk), BF16)) + tuple(p[2] for p in plans),
        grid=(n // tm,),
        in_specs=[
            pl.BlockSpec((tm, d), row),
            pl.BlockSpec((1, 8, d), lambda i: (i // tiles_per_b, 0, 0)),
            pl.BlockSpec((1, d), const),
            pl.BlockSpec((1, 1, tm), lambda i: (i, 0, 0)),
            pl.BlockSpec((HALF, 1), const),
            pl.BlockSpec((d, d), const),
            pl.BlockSpec((d, d), const),
            pl.BlockSpec((d, d), const),
        ] + [p[0] for p in plans],
        out_specs=(pl.BlockSpec((1, d, tm), lambda i: (i // tiles_per_b, 0, i % tiles_per_b)),
                   pl.BlockSpec((tm, d), row),
                   pl.BlockSpec((1, tm // tk, nh, V_ROWS, tk),
                                lambda i: (i // tiles_per_b, i % tiles_per_b, 0, 0, 0)))
        + tuple(p[1] for p in plans),
        compiler_params=_cparams(("arbitrary",)),
        name="qkv_rope",
    )(x2, mod, g, pos_row, freq_col, wqt, wk, wvt, *cast_weights)
    return outs[0], outs[1], outs[2], outs[3:]


def _attn_kernel(qt_ref, k_ref, vt_ref, lam_ref, g_ref, *rest, tk, lambda_init, ncast):
    o_ref = rest[ncast]
    s_a, s_b, mc_a, mc_b, acc = rest[2 * ncast + 1:]
    _cast_blocks(rest[:ncast] + rest[ncast + 1:2 * ncast + 1])
    tq = qt_ref.shape[2]
    nh = acc.shape[0]
    qi = pl.program_id(2)
    n_full = qi * (tq // tk)
    row = lax.broadcasted_iota(jnp.int32, (GROUP, 1), 0)
    head = (row % LANES) // HALF
    qf = qt_ref[0].astype(F32)
    qm = [jnp.where(head == hd, qf, 0.0).astype(BF16) for hd in range(nh)]

    def scores(kj, s_ref, mc_ref, masked, lo=0):
        kc = k_ref[pl.ds(pl.multiple_of(kj * tk, tk), tk), :]
        for hd in range(nh):
            s = _dot(kc, qm[hd][:, lo:])
            if masked:
                kidx = kj * tk + lax.broadcasted_iota(jnp.int32, s.shape, 0)
                qidx = qi * tq + lo + lax.broadcasted_iota(jnp.int32, s.shape, 1)
                s = jnp.where(kidx <= qidx, s, NEG)
            s_ref[hd, :, lo:] = s
            mc_ref[hd, :, lo:] = jnp.max(s, axis=0, keepdims=True)

    def softmax_pv(kj, s_ref, mc_ref, ms, first, lo=0):
        out = []
        for hd in range(nh):
            vc = vt_ref[0, kj, hd // 2]
            s = s_ref[hd, :, lo:]
            mc = mc_ref[hd, :, lo:]
            m_new = mc if first else jnp.maximum(ms[hd][:, lo:], mc)
            pv = _dot(vc, jnp.exp2(s - m_new).astype(BF16))
            if first:
                acc[hd] = pv
            else:
                acc[hd, :, lo:] = jnp.exp2(ms[hd][:, lo:] - m_new) * acc[hd, :, lo:] + pv
            out.append(m_new if lo == 0 else jnp.concatenate([ms[hd][:, :lo], m_new], axis=1))
        return tuple(out)

    assert tq == 2 * tk
    scores(n_full, s_a, mc_a, True)
    scores(n_full + 1, s_b, mc_b, True, lo=tk)
    ms = softmax_pv(n_full, s_a, mc_a, None, True)
    scores(0, s_a, mc_a, False)
    ms = softmax_pv(n_full + 1, s_b, mc_b, ms, False, lo=tk)

    def body(jj, ms):
        j = 2 * jj
        scores(j + 1, s_b, mc_b, False)
        ms = softmax_pv(j, s_a, mc_a, ms, False)
        scores(j + 2, s_a, mc_a, False)
        return softmax_pv(j + 1, s_b, mc_b, ms, False)

    ms = lax.fori_loop(0, n_full // 2 - 1, body, ms)

    @pl.when(n_full > 0)
    def _():
        scores(n_full - 1, s_b, mc_b, False)
        ms2 = softmax_pv(n_full - 2, s_a, mc_a, ms, False)
        softmax_pv(n_full - 1, s_b, mc_b, ms2, False)

    lp = lam_ref[...]
    lam = (jnp.exp(jnp.sum(lp[0:1] * lp[1:2], axis=-1, keepdims=True))
           - jnp.exp(jnp.sum(lp[2:3] * lp[3:4], axis=-1, keepdims=True)) + lambda_init)
    for pr in range(nh // 2):
        a1 = acc[2 * pr]
        a2 = acc[2 * pr + 1]
        o = a1[:V_DIM] / a1[V_DIM:V_DIM + 1] - lam * (a2[:V_DIM] / a2[V_DIM:V_DIM + 1])
        o = o * lax.rsqrt(jnp.mean(o * o, axis=0, keepdims=True) + EPS)
        o = (o * g_ref[...]) * (1.0 - lambda_init)
        o_ref[:, pr * V_DIM:(pr + 1) * V_DIM] = o.T.astype(BF16)


def _attention(qt, k, vt, lam_rows, g_col, tq, tk, lambda_init, cast_weights):
    bsz, d, t = qt.shape
    n = bsz * t
    nq = t // tq
    nh = GROUP // HEAD_DIM
    ng = d // GROUP
    plans = [_cast_plan(w, bsz * ng * nq, lambda b, g, i: (b * ng + g) * nq + i) for w in cast_weights]
    outs = pl.pallas_call(
        functools.partial(_attn_kernel, tk=tk, lambda_init=lambda_init, ncast=len(plans)),
        out_shape=(jax.ShapeDtypeStruct((n, d), BF16),) + tuple(pln[2] for pln in plans),
        grid=(bsz, ng, nq),
        in_specs=[
            pl.BlockSpec((1, GROUP, tq), lambda b, g, i: (b, g, i)),
            pl.BlockSpec((t, GROUP), lambda b, g, i: (b, g)),
            pl.BlockSpec((1, t // tk, nh // 2, V_ROWS, tk), lambda b, g, i: (b, 0, g, 0, 0)),
            pl.BlockSpec((4, HEAD_DIM), lambda b, g, i: (0, 0)),
            pl.BlockSpec((V_DIM, 1), lambda b, g, i: (0, 0)),
        ] + [pln[0] for pln in plans],
        out_specs=(pl.BlockSpec((tq, GROUP), lambda b, g, i: (b * nq + i, g)),) + tuple(pln[1] for pln in plans),
        scratch_shapes=[pltpu.VMEM((nh, tk, tq), F32), pltpu.VMEM((nh, tk, tq), F32),
                        pltpu.VMEM((nh, 1, tq), F32), pltpu.VMEM((nh, 1, tq), F32),
                        pltpu.VMEM((nh, V_ROWS, tq), F32)],
        compiler_params=_cparams(("arbitrary", "arbitrary", "arbitrary")),
        name="diff_attn",
    )(qt, k, vt, lam_rows, g_col, *cast_weights)
    return outs[0], outs[1:]


def _ffn_kernel(a_ref, wo_ref, x_ref, mod_ref, g_ref, wg_ref, wu_ref, wd_ref, o_ref, *, tf):
    m = mod_ref[0]
    x = x_ref[...] + m[2:3] * _dot(a_ref[...], wo_ref[...])
    h = _norm_mod(x, g_ref[...], m[3:4], m[4:5]).astype(BF16)
    f = wg_ref.shape[1]
    acc = None
    for ci in range(f // tf):
        sl = slice(ci * tf, (ci + 1) * tf)
        gt = _dot(h, wg_ref[:, sl])
        up = _dot(h, wu_ref[:, sl])
        a = (gt * _sigmoid(gt) * up).astype(BF16)
        part = _dot(a, wd_ref[sl, :])
        acc = part if acc is None else acc + part
    o_ref[...] = x + m[5:6] * acc


def _ffn(a, wo, x2, mod, g, wg, wu, wd, bsz, tm, tf):
    n, d = x2.shape
    f = wg.shape[1]
    tiles_per_b = n // bsz // tm
    row = lambda i: (i, 0)
    const = lambda i: (0, 0)
    return pl.pallas_call(
        functools.partial(_ffn_kernel, tf=tf),
        out_shape=jax.ShapeDtypeStruct((n, d), F32),
        grid=(n // tm,),
        in_specs=[
            pl.BlockSpec((tm, d), row),
            pl.BlockSpec((d, d), const),
            pl.BlockSpec((tm, d), row),
            pl.BlockSpec((1, 8, d), lambda i: (i // tiles_per_b, 0, 0)),
            pl.BlockSpec((1, d), const),
            pl.BlockSpec((d, f), const),
            pl.BlockSpec((d, f), const),
            pl.BlockSpec((f, d), const),
        ],
        out_specs=pl.BlockSpec((tm, d), row),
        compiler_params=_cparams(("arbitrary",)),
        name="oproj_dense_swiglu",
    )(a, wo, x2, mod, g, wg, wu, wd)


def _route_top2(wr_t, h, idx_ref, gate_ref):
    logits = lax.dot_general(wr_t, h, (((1,), (1,)), ((), ())),
                             preferred_element_type=F32, precision=lax.Precision.HIGHEST)
    ne = logits.shape[0]
    eidx = lax.broadcasted_iota(jnp.int32, logits.shape, 0)
    v1 = jnp.max(logits, axis=0, keepdims=True)
    i1 = jnp.min(jnp.where(logits == v1, eidx, ne), axis=0, keepdims=True)
    rest = jnp.where(eidx == i1, -jnp.inf, logits)
    v2 = jnp.max(rest, axis=0, keepdims=True)
    i2 = jnp.min(jnp.where(rest == v2, eidx, ne), axis=0, keepdims=True)
    e2 = jnp.exp(v2 - v1)
    den = 1.0 + e2
    idx_ref[0:1, :] = i1
    idx_ref[1:2, :] = i2
    gate_ref[0:1, :] = 1.0 / den
    gate_ref[1:2, :] = e2 / den


def _conformer_kernel(x_ref, mod_ref, gm_ref, w1_ref, b1_ref, wdw_ref, bdw_ref, lg_ref, lb_ref, w2_ref, b2_ref,
                      gf_ref, wr_ref, x_out_ref, h_ref, idx_ref, gate_ref, ext, vbuf, *, tiles_per_b, rb):
    tm, d = x_ref.shape
    nchunks = d // LANES
    first = (pl.program_id(0) % tiles_per_b) == 0

    @pl.when(first)
    def _():
        ext[:, 0:CONV_HALO, :] = jnp.zeros((nchunks, CONV_HALO, LANES), F32)

    @pl.when(jnp.logical_not(first))
    def _():
        ext[:, 0:CONV_HALO, :] = ext[:, tm:tm + CONV_HALO, :]

    m = mod_ref[0]
    x = x_ref[...]
    h = _norm_mod(x, gm_ref[...], m[0:1], m[1:2]).astype(BF16)
    off = CONV_HALO - (CONV_WIDTH - 1)
    tn = 2 * LANES
    for ci in range(d // tn):
        a = _dot(h, w1_ref[:, ci * tn:(ci + 1) * tn]) + b1_ref[:, ci * tn:(ci + 1) * tn]
        gt = _dot(h, w1_ref[:, d + ci * tn:d + (ci + 1) * tn]) + b1_ref[:, d + ci * tn:d + (ci + 1) * tn]
        u = a * _sigmoid(gt)
        for half in range(tn // LANES):
            c = ci * (tn // LANES) + half
            cols = slice(c * LANES, (c + 1) * LANES)
            ext[c, CONV_HALO:, :] = u[:, half * LANES:(half + 1) * LANES]
            for r in range(tm // rb):
                acc = jnp.broadcast_to(bdw_ref[:, cols], (rb, LANES))
                for kk in range(CONV_WIDTH):
                    acc = acc + wdw_ref[kk:kk + 1, cols] * ext[c, r * rb + off + kk:r * rb + off + kk + rb, :]
                vbuf[r * rb:(r + 1) * rb, cols] = acc
    v = vbuf[...]
    mu = jnp.mean(v, axis=-1, keepdims=True)
    vc = v - mu
    var = jnp.mean(vc * vc, axis=-1, keepdims=True)
    y = vc * lax.rsqrt(var + EPS) * lg_ref[...] + lb_ref[...]
    y = (y * _sigmoid(y)).astype(BF16)
    x = x + m[2:3] * (_dot(y, w2_ref[...]) + b2_ref[...])
    x_out_ref[...] = x
    hf = _norm_mod(x, gf_ref[...], m[3:4], m[4:5])
    h_ref[...] = hf
    _route_top2(wr_ref[...], hf, idx_ref, gate_ref)


def _conformer(x2, mod, gm, w1, b1, wdw, bdw, ln_g, ln_b, w2, b2, gf, wr_t, bsz, tm):
    n, d = x2.shape
    ne = wr_t.shape[0]
    tiles_per_b = n // bsz // tm
    row = lambda i: (i, 0)
    const = lambda i: (0, 0)
    vec = pl.BlockSpec((1, d), const)
    return pl.pallas_call(
        functools.partial(_conformer_kernel, tiles_per_b=tiles_per_b, rb=64),
        out_shape=(jax.ShapeDtypeStruct((n, d), F32), jax.ShapeDtypeStruct((n, d), F32),
                   jax.ShapeDtypeStruct((TOP_K, n), jnp.int32), jax.ShapeDtypeStruct((TOP_K, n), F32)),
        grid=(n // tm,),
        in_specs=[
            pl.BlockSpec((tm, d), row),
            pl.BlockSpec((1, 8, d), lambda i: (i // tiles_per_b, 0, 0)),
            vec,
            pl.BlockSpec((d, 2 * d), const),
            pl.BlockSpec((1, 2 * d), const),
            pl.BlockSpec((CONV_WIDTH, d), const),
            vec, vec, vec,
            pl.BlockSpec((d, d), const),
            vec, vec,
            pl.BlockSpec((ne, d), const),
        ],
        out_specs=(pl.BlockSpec((tm, d), row), pl.BlockSpec((tm, d), row),
                   pl.BlockSpec((TOP_K, tm), lambda i: (0, i)), pl.BlockSpec((TOP_K, tm), lambda i: (0, i))),
        scratch_shapes=[pltpu.VMEM((d // LANES, tm + CONV_HALO, LANES), F32), pltpu.VMEM((tm, d), F32)],
        compiler_params=_cparams(("arbitrary",)),
        name="conformer_router",
    )(x2, mod, gm, w1, b1, wdw, bdw, ln_g, ln_b, w2, b2, gf, wr_t)


def _moe_kernel(be_ref, na_ref, src_ref, dst_ref, h_hbm, wg_ref, wu_ref, wd_ref, y_hbm,
                xbuf, xb16, acc, gsem, ssem, *, n_assign, n_spare_blocks):
    tm = xb16.shape[0]
    nf = pl.num_programs(1)
    nblk = pl.num_programs(0)
    rpb = tm // LANES
    blk = pl.program_id(0)
    f = pl.program_id(1)
    na = na_ref[0]
    active = blk < na
    slot = blk % 2
    other = 1 - slot

    def gather_row(trow, col, buf, r):
        return pltpu.make_async_copy(h_hbm.at[pl.ds(src_ref[trow, col], 1)],
                                     xbuf.at[buf, pl.ds(r, 1)], gsem.at[buf])

    def scatter_row(trow, col, buf, r):
        return pltpu.make_async_copy(acc.at[buf, pl.ds(r, 1)], y_hbm.at[pl.ds(dst_ref[trow, col], 1)],
                                     ssem.at[buf])

    def wait_gather(buf):
        pltpu.make_async_copy(h_hbm.at[pl.ds(0, tm)], xbuf.at[buf], gsem.at[buf]).wait()

    def wait_scatter(buf):
        pltpu.make_async_copy(acc.at[buf], y_hbm.at[pl.ds(0, tm)], ssem.at[buf]).wait()

    def scatter_block(b, buf):
        def body(i, carry):
            scatter_row(b * rpb + i // LANES, i % LANES, buf, i).start()
            return carry
        lax.fori_loop(0, tm, body, 0)
        wait_scatter(buf)

    @pl.when(jnp.logical_and(blk == 0, f == 0))
    def _():
        def body(i, carry):
            gather_row(i // LANES, i % LANES, 0, i).start()
            return carry
        lax.fori_loop(0, tm, body, 0)
        acc[1] = jnp.zeros(acc.shape[1:], F32)
        spare_fill = [pltpu.make_async_copy(acc.at[1], y_hbm.at[pl.ds(n_assign + j * tm, tm)], ssem.at[1])
                      for j in range(n_spare_blocks)]
        for cp in spare_fill:
            cp.start()
        for cp in spare_fill:
            cp.wait()

    @pl.when(jnp.logical_and(active, f == 0))
    def _():
        wait_gather(slot)
        xb16[...] = xbuf[slot].astype(BF16)

        @pl.when(blk >= 1)
        def _():
            wait_scatter(slot)
        acc[slot] = jnp.zeros(acc.shape[1:], F32)

    @pl.when(active)
    def _():
        quarter = tm // nf
        qrows = quarter // LANES
        gbase = (blk + 1) * rpb + f * qrows
        sbase = jnp.where(blk == 0, nblk, blk - 1) * rpb + f * qrows
        for i in range(quarter):
            gather_row(gbase + i // LANES, i % LANES, other, f * quarter + i).start()
            scatter_row(sbase + i // LANES, i % LANES, other, f * quarter + i).start()
        xb = xb16[...]
        gt = _dot(xb, wg_ref[0])
        up = _dot(xb, wu_ref[0])
        acc[slot] += _dot((gt * _sigmoid(gt) * up).astype(BF16), wd_ref[0])

    @pl.when(jnp.logical_and(f == 0, blk == na))
    def _():
        wait_gather(slot)
        wait_scatter(slot)
        scatter_block(blk - 1, other)

    @pl.when(jnp.logical_and(jnp.logical_and(f == nf - 1, blk == nblk - 1), active))
    def _():
        wait_gather(other)
        wait_scatter(other)
        scatter_block(blk, slot)


def _moe(block_e, n_active, src_tab, dst_tab, h, wg, wu, wd, n_out, tm, tf):
    n, d = h.shape
    ne, _, fe = wg.shape
    nf = fe // tf
    nblk = dst_tab.shape[0] * LANES // tm - 1
    assert tm % (nf * LANES) == 0

    def wsel(blk, f, be, na, src, dst):
        return jnp.where(blk < na[0], f, nf - 1)

    grid_spec = pltpu.PrefetchScalarGridSpec(
        num_scalar_prefetch=4,
        grid=(nblk, nf),
        in_specs=[
            pl.BlockSpec(memory_space=pl.ANY),
            pl.BlockSpec((1, d, tf), lambda blk, f, be, na, src, dst: (be[blk], 0, wsel(blk, f, be, na, src, dst))),
            pl.BlockSpec((1, d, tf), lambda blk, f, be, na, src, dst: (be[blk], 0, wsel(blk, f, be, na, src, dst))),
            pl.BlockSpec((1, tf, d), lambda blk, f, be, na, src, dst: (be[blk], wsel(blk, f, be, na, src, dst), 0)),
        ],
        out_specs=pl.BlockSpec(memory_space=pl.ANY),
        scratch_shapes=[pltpu.VMEM((2, tm, d), F32), pltpu.VMEM((tm, d), BF16), pltpu.VMEM((2, tm, d), F32),
                        pltpu.SemaphoreType.DMA((2,)), pltpu.SemaphoreType.DMA((2,))],
    )
    return pl.pallas_call(
        functools.partial(_moe_kernel, n_assign=n * TOP_K, n_spare_blocks=ne),
        out_shape=jax.ShapeDtypeStruct((n_out, d), F32),
        grid_spec=grid_spec,
        compiler_params=_cparams(("arbitrary", "arbitrary")),
        name="moe_experts",
    )(block_e, n_active, src_tab, dst_tab, h, wg, wu, wd)


def _combine_kernel(y0_ref, y1_ref, x_ref, gates_ref, mod_ref, g_ref, o_ref):
    gate = mod_ref[0][5:6]
    gt = gates_ref[...]
    x = x_ref[...] + gate * (gt[:, 0:1] * y0_ref[...] + gt[:, 1:2] * y1_ref[...])
    o_ref[...] = x * lax.rsqrt(jnp.mean(x * x, axis=-1, keepdims=True) + EPS) * g_ref[...]


def _combine(y, x2, gates_t, mod, final_g, bsz, tm):
    n, d = x2.shape
    tiles_per_b = n // bsz // tm
    nt = n // tm
    row = lambda i: (i, 0)
    return pl.pallas_call(
        _combine_kernel,
        out_shape=jax.ShapeDtypeStruct((n, d), F32),
        grid=(nt,),
        in_specs=[
            pl.BlockSpec((tm, d), row),
            pl.BlockSpec((tm, d), lambda i: (nt + i, 0)),
            pl.BlockSpec((tm, d), row),
            pl.BlockSpec((tm, TOP_K), row),
            pl.BlockSpec((1, 8, d), lambda i: (i // tiles_per_b, 0, 0)),
            pl.BlockSpec((1, d), lambda i: (0, 0)),
        ],
        out_specs=pl.BlockSpec((tm, d), row),
        compiler_params=_cparams(("arbitrary",)),
        name="moe_combine_final_norm",
    )(y, y, x2, gates_t, mod, final_g)


def _routing(idx, ne, tm):
    n = idx.shape[1]
    a_tot = n * TOP_K
    flat_e = idx.T.reshape(-1)
    sizes = jnp.sum((flat_e[:, None] == jnp.arange(ne, dtype=jnp.int32)[None, :]).astype(jnp.int32), axis=0)
    padded = (sizes + tm - 1) // tm * tm
    pad_end = jnp.cumsum(padded)
    pad_start = pad_end - padded
    grp_start = jnp.cumsum(sizes) - sizes
    p = a_tot + ne * tm
    nblk = p // tm
    n_active = (pad_end[-1] // tm).astype(jnp.int32)
    blk_start = jnp.arange(nblk, dtype=jnp.int32) * tm
    block_e = jnp.sum((blk_start[:, None] >= pad_end[None, :]).astype(jnp.int32), axis=1)
    last_e = jnp.sum((jnp.maximum(pad_end[-1] - 1, 0) >= pad_end).astype(jnp.int32))
    block_e = jnp.where(jnp.arange(nblk) < n_active, block_e, last_e).astype(jnp.int32)
    order = jnp.argsort(flat_e).astype(jnp.int32)
    within0 = blk_start - pad_start[block_e]
    size_b = sizes[block_e]
    grp_b = grp_start[block_e]
    lane = jnp.arange(tm, dtype=jnp.int32)[None, :]
    within = within0[:, None] + lane
    valid = within < size_b[:, None]
    order_pad = jnp.pad(order, (0, tm))
    seg = jax.vmap(lambda s: lax.dynamic_slice(order_pad, (s,), (tm,)))(
        jnp.clip(grp_b + within0, 0, a_tot))
    src_tok = jnp.where(valid, seg // TOP_K, 0).astype(jnp.int32)
    spare = a_tot + blk_start[:, None] + lane - grp_b[:, None] - jnp.minimum(within, size_b[:, None])
    dst_row = jnp.where(valid, (seg % TOP_K) * n + seg // TOP_K, spare).astype(jnp.int32)
    src_tab = jnp.pad(src_tok.reshape(-1), (0, tm)).reshape(-1, LANES)
    dst_tab = jnp.concatenate([dst_row.reshape(-1), p + jnp.arange(tm, dtype=jnp.int32)]).reshape(-1, LANES)
    return block_e, n_active.reshape(1), src_tab, dst_tab, p + tm


def _rope_perm(d):
    perm = np.zeros((d,), np.int32)
    for h in range(d // HEAD_DIM):
        gi, r = divmod(h, 4)
        for dd in range(HEAD_DIM):
            new = gi * GROUP + (dd // HALF) * LANES + r * HALF + dd % HALF
            perm[new] = h * HEAD_DIM + dd
    return perm


def _mod_rows(mod_l, bsz, d):
    m = mod_l[:bsz].reshape(bsz, 6, d)
    return jnp.pad(m, ((0, 0), (0, 2), (0, 0)))


@jax.jit
def kernel(x, c, positions, w_ada, b_ada, norm_mix_g, norm_ffn_g, attn_w_qkv, attn_w_o, lam_q1, lam_k1,
           lam_q2, lam_k2, attn_subln_g, conv_w_pw1, conv_b_pw1, conv_w_dw, conv_b_dw, conv_ln_g,
           conv_ln_b, conv_w_pw2, conv_b_pw2, ffn_w_gate, ffn_w_up, ffn_w_down, moe_w_router,
           moe_w_gate, moe_w_up, moe_w_down, final_g):
    bsz, t, d = x.shape
    n = bsz * t
    depth = w_ada.shape[0]
    assert depth == 2 and d % GROUP == 0 and bsz <= 8
    tm = min(512, t)
    tk = min(256, t)
    x2 = x.reshape(n, d)

    c8 = jnp.pad(c, ((0, 8 - bsz), (0, 0)))
    mod = _ada(c8, w_ada, b_ada)
    mod0 = _mod_rows(mod[0], bsz, d)
    mod1 = _mod_rows(mod[1], bsz, d)

    inv_freq = ROPE_THETA ** (-jnp.arange(HALF, dtype=F32) / HALF)
    perm = _rope_perm(d)
    w_qkv = attn_w_qkv[0]
    wqt = w_qkv[:, perm].T.astype(BF16)
    wk = w_qkv[:, d + perm].astype(BF16)
    wvt = w_qkv[:, 2 * d:].T.astype(BF16)
    qt, k, vt, (ffn_wg, ffn_wu, ffn_wd, w_o, w_pw1, w_pw2) = _qkv(
        x2, mod0, norm_mix_g[0:1], positions.reshape(n // tm, 1, tm), inv_freq.reshape(HALF, 1),
        wqt, wk, wvt, bsz, tm, tk,
        [ffn_w_gate[0], ffn_w_up[0], ffn_w_down[0], attn_w_o[0], conv_w_pw1[0], conv_w_pw2[0]])
    lam_rows = jnp.concatenate([lam_q1[0:1], lam_k1[0:1], lam_q2[0:1], lam_k2[0:1]], axis=0)
    lambda_init = 0.8 - 0.6 * math.exp(-0.3 * 0)
    ne, _, fe = moe_w_gate.shape[1:]
    o, (moe_wg, moe_wu, moe_wd) = _attention(
        qt, k, vt, lam_rows, attn_subln_g[0].reshape(V_DIM, 1), tm, tk, lambda_init,
        [moe_w_gate[0].reshape(ne * d, fe), moe_w_up[0].reshape(ne * d, fe), moe_w_down[0].reshape(ne * fe, d)])
    x2 = _ffn(o, w_o, x2, mod0, norm_ffn_g[0:1], ffn_wg, ffn_wu, ffn_wd, bsz, tm, tf=256)

    x2, h, idx, gates = _conformer(x2, mod1, norm_mix_g[1:2], w_pw1, conv_b_pw1[0:1], conv_w_dw[0],
                                   conv_b_dw[0:1], conv_ln_g[0:1], conv_ln_b[0:1], w_pw2, conv_b_pw2[0:1],
                                   norm_ffn_g[1:2], moe_w_router[0].T, bsz, tm)
    block_e, n_active, src_tab, dst_tab, n_out = _routing(idx, ne, tm)
    tf_e = 1792 if fe % 1792 == 0 else fe
    y = _moe(block_e, n_active, src_tab, dst_tab, h, moe_wg.reshape(ne, d, fe), moe_wu.reshape(ne, d, fe),
             moe_wd.reshape(ne, fe, d), n_out, tm, tf_e)
    out = _combine(y, x2, gates.T, mod1, final_g.reshape(1, d), bsz, tm)
    return out.reshape(bsz, t, d)
```

```python
import functools
import math

import numpy as np
import jax
import jax.numpy as jnp
from jax import lax
from jax.experimental import pallas as pl
from jax.experimental.pallas import tpu as pltpu

EPS = 1e-6
ROPE_THETA = 10000.0
HEAD_DIM = 64
V_DIM = 128
V_ROWS = V_DIM + 16
HALF = HEAD_DIM // 2
LANES = 128
GROUP = 4 * HEAD_DIM
CONV_WIDTH = 31
CONV_HALO = 32
TOP_K = 2
NEG = -1e30
VMEM_LIMIT = 56 * 1024 * 1024

BF16 = jnp.bfloat16
F32 = jnp.float32


def _cparams(sem):
    return pltpu.CompilerParams(dimension_semantics=sem, vmem_limit_bytes=VMEM_LIMIT)


def _dot(a, b):
    return jnp.dot(a, b, preferred_element_type=F32)


def _dot_nt(a, b):
    return lax.dot_general(a, b, (((1,), (1,)), ((), ())), preferred_element_type=F32)


def _norm_mod(x, g, shift, scale):
    y = x * lax.rsqrt(jnp.mean(x * x, axis=-1, keepdims=True) + EPS)
    return (y * g) * (1.0 + scale) + shift


def _sigmoid(x):
    return 1.0 / (1.0 + jnp.exp(-x))


BF16_SUBLANES = 16


def _cast_plan(w2d, nsteps, step_of):
    rows, cols = w2d.shape
    share = 1
    while rows % (nsteps // share) or (rows // (nsteps // share)) % BF16_SUBLANES:
        share *= 2
        assert share <= nsteps, (rows, nsteps)
    rb = rows // (nsteps // share)
    imap = lambda *ids: (step_of(*ids) // share, 0)
    return (pl.BlockSpec((rb, cols), imap), pl.BlockSpec((rb, cols), imap),
            jax.ShapeDtypeStruct((rows, cols), BF16))


def _cast_blocks(refs):
    k = len(refs) // 2
    for src, dst in zip(refs[:k], refs[k:]):
        dst[...] = src[...].astype(BF16)


def _ada_kernel(c_ref, w_ref, b_ref, o_ref):
    c = c_ref[...]
    ca = (c * _sigmoid(c)).astype(BF16)
    o_ref[0] = _dot(ca, w_ref[0].astype(BF16)) + b_ref[0]


def _ada(c8, w_ada, b_ada):
    depth, d, d6 = w_ada.shape
    tn = 1536
    return pl.pallas_call(
        _ada_kernel,
        out_shape=jax.ShapeDtypeStruct((depth, 8, d6), F32),
        grid=(depth, d6 // tn),
        in_specs=[
            pl.BlockSpec((8, d), lambda l, j: (0, 0)),
            pl.BlockSpec((1, d, tn), lambda l, j: (l, 0, j)),
            pl.BlockSpec((1, 1, tn), lambda l, j: (l, 0, j)),
        ],
        out_specs=pl.BlockSpec((1, 8, tn), lambda l, j: (l, 0, j)),
        compiler_params=_cparams(("arbitrary", "arbitrary")),
        name="ada_mod",
    )(c8, w_ada, b_ada.reshape(depth, 1, d6))


def _qkv_kernel(x_ref, mod_ref, g_ref, pos_ref, freq_ref, wqt_ref, wk_ref, wvt_ref, *rest, tk, ncast):
    qt_ref, k_ref, vt_ref = rest[ncast:ncast + 3]
    _cast_blocks(rest[:ncast] + rest[ncast + 3:])
    m = mod_ref[0]
    h = _norm_mod(x_ref[...], g_ref[...], m[0:1], m[1:2]).astype(BF16)
    d = x_ref.shape[1]
    tm = x_ref.shape[0]
    ang = freq_ref[...] * pos_ref[0].astype(F32)
    reps = LANES // HALF
    ct = jnp.concatenate([jnp.cos(ang)] * reps, axis=0)
    st = jnp.concatenate([jnp.sin(ang)] * reps, axis=0)
    c = ct.T
    s = st.T
    for gi in range(d // GROUP):
        y = _dot(h, wk_ref[:, gi * GROUP:(gi + 1) * GROUP])
        a = y[:, :LANES]
        b = y[:, LANES:]
        k_ref[:, gi * GROUP:gi * GROUP + LANES] = (a * c - b * s).astype(BF16)
        k_ref[:, gi * GROUP + LANES:(gi + 1) * GROUP] = (b * c + a * s).astype(BF16)
    mult = HEAD_DIM ** -0.5 * math.log2(math.e)
    for gi in range(d // GROUP):
        y = _dot_nt(wqt_ref[gi * GROUP:(gi + 1) * GROUP, :], h)
        a = y[:LANES]
        b = y[LANES:]
        qt_ref[0, gi * GROUP:gi * GROUP + LANES, :] = ((a * ct - b * st) * mult).astype(BF16)
        qt_ref[0, gi * GROUP + LANES:(gi + 1) * GROUP, :] = ((b * ct + a * st) * mult).astype(BF16)
    ones = jnp.ones((V_ROWS - V_DIM, tk), BF16)
    for gi in range(d // GROUP):
        vt = _dot_nt(wvt_ref[gi * GROUP:(gi + 1) * GROUP, :], h)
        for hh in range(GROUP // V_DIM):
            for ci in range(tm // tk):
                vt_ref[0, ci, 2 * gi + hh, 0:V_DIM, :] = (
                    vt[hh * V_DIM:(hh + 1) * V_DIM, ci * tk:(ci + 1) * tk].astype(BF16))
                vt_ref[0, ci, 2 * gi + hh, V_DIM:V_ROWS, :] = ones


def _qkv(x2, mod, g, pos_row, freq_col, wqt, wk, wvt, bsz, tm, tk, cast_weights):
    n, d = x2.shape
    t = n // bsz
    tiles_per_b = t // tm
    nh = d // V_DIM
    row = lambda i: (i, 0)
    const = lambda i: (0, 0)
    plans = [_cast_plan(w, n // tm, lambda i: i) for w in cast_weights]
    outs = pl.pallas_call(
        functools.partial(_qkv_kernel, tk=tk, ncast=len(plans)),
        out_shape=(jax.ShapeDtypeStruct((bsz, d, t), BF16), jax.ShapeDtypeStruct((n, d), BF16),
                   jax.ShapeDtypeStruct((bsz, t // tk, nh, V_ROWS, tk), BF16)) + tuple(p[2] for p in plans),
        grid=(n // tm,),
        in_specs=[
            pl.BlockSpec((tm, d), row),
            pl.BlockSpec((1, 8, d), lambda i: (i // tiles_per_b, 0, 0)),
            pl.BlockSpec((1, d), const),
            pl.BlockSpec((1, 1, tm), lambda i: (i, 0, 0)),
            pl.BlockSpec((HALF, 1), const),
            pl.BlockSpec((d, d), const),
            pl.BlockSpec((d, d), const),
            pl.BlockSpec((d, d), const),
        ] + [p[0] for p in plans],
        out_specs=(pl.BlockSpec((1, d, tm), lambda i: (i // tiles_per_b, 0, i % tiles_per_b)),
                   pl.BlockSpec((tm, d), row),
                   pl.BlockSpec((1, tm // tk, nh, V_ROWS, tk),
                                lambda i: (i // tiles_per_b, i % tiles_per_b, 0, 0, 0)))
        + tuple(p[1] for p in plans),
        compiler_params=_cparams(("arbitrary",)),
        name="qkv_rope",
    )(x2, mod, g, pos_row, freq_col, wqt, wk, wvt, *cast_weights)
    return outs[0], outs[1], outs[2], outs[3:]


def _attn_kernel(qt_ref, k_ref, vt_ref, lam_ref, g_ref, *rest, tk, lambda_init, ncast):
    o_ref = rest[ncast]
    s_a, s_b, mc_a, mc_b, acc = rest[2 * ncast + 1:]
    _cast_blocks(rest[:ncast] + rest[ncast + 1:2 * ncast + 1])
    tq = qt_ref.shape[2]
    nh = acc.shape[0]
    qi = pl.program_id(2)
    n_full = qi * (tq // tk)
    row = lax.broadcasted_iota(jnp.int32, (GROUP, 1), 0)
    head = (row % LANES) // HALF
    qf = qt_ref[0].astype(F32)
    qm = [jnp.where(head == hd, qf, 0.0).astype(BF16) for hd in range(nh)]

    def scores(kj, s_ref, mc_ref, masked, lo=0):
        kc = k_ref[pl.ds(pl.multiple_of(kj * tk, tk), tk), :]
        for hd in range(nh):
            s = _dot(kc, qm[hd][:, lo:])
            if masked:
                kidx = kj * tk + lax.broadcasted_iota(jnp.int32, s.shape, 0)
                qidx = qi * tq + lo + lax.broadcasted_iota(jnp.int32, s.shape, 1)
                s = jnp.where(kidx <= qidx, s, NEG)
            s_ref[hd, :, lo:] = s
            mc_ref[hd, :, lo:] = jnp.max(s, axis=0, keepdims=True)

    def softmax_pv(kj, s_ref, mc_ref, ms, first, lo=0):
        out = []
        for hd in range(nh):
            vc = vt_ref[0, kj, hd // 2]
            s = s_ref[hd, :, lo:]
            mc = mc_ref[hd, :, lo:]
            m_new = mc if first else jnp.maximum(ms[hd][:, lo:], mc)
            pv = _dot(vc, jnp.exp2(s - m_new).astype(BF16))
            if first:
                acc[hd] = pv
            else:
                acc[hd, :, lo:] = jnp.exp2(ms[hd][:, lo:] - m_new) * acc[hd, :, lo:] + pv
            out.append(m_new if lo == 0 else jnp.concatenate([ms[hd][:, :lo], m_new], axis=1))
        return tuple(out)

    assert tq == 2 * tk
    scores(n_full, s_a, mc_a, True)
    scores(n_full + 1, s_b, mc_b, True, lo=tk)
    ms = softmax_pv(n_full, s_a, mc_a, None, True)
    scores(0, s_a, mc_a, False)
    ms = softmax_pv(n_full + 1, s_b, mc_b, ms, False, lo=tk)

    def body(jj, ms):
        j = 2 * jj
        scores(j + 1, s_b, mc_b, False)
        ms = softmax_pv(j, s_a, mc_a, ms, False)
        scores(j + 2, s_a, mc_a, False)
        return softmax_pv(j + 1, s_b, mc_b, ms, False)

    ms = lax.fori_loop(0, n_full // 2 - 1, body, ms)

    @pl.when(n_full > 0)
    def _():
        scores(n_full - 1, s_b, mc_b, False)
        ms2 = softmax_pv(n_full - 2, s_a, mc_a, ms, False)
        softmax_pv(n_full - 1, s_b, mc_b, ms2, False)

    lp = lam_ref[...]
    lam = (jnp.exp(jnp.sum(lp[0:1] * lp[1:2], axis=-1, keepdims=True))
           - jnp.exp(jnp.sum(lp[2:3] * lp[3:4], axis=-1, keepdims=True)) + lambda_init)
    for pr in range(nh // 2):
        a1 = acc[2 * pr]
        a2 = acc[2 * pr + 1]
        o = a1[:V_DIM] / a1[V_DIM:V_DIM + 1] - lam * (a2[:V_DIM] / a2[V_DIM:V_DIM + 1])
        o = o * lax.rsqrt(jnp.mean(o * o, axis=0, keepdims=True) + EPS)
        o = (o * g_ref[...]) * (1.0 - lambda_init)
        o_ref[:, pr * V_DIM:(pr + 1) * V_DIM] = o.T.astype(BF16)


def _attention(qt, k, vt, lam_rows, g_col, tq, tk, lambda_init, cast_weights):
    bsz, d, t = qt.shape
    n = bsz * t
    nq = t // tq
    nh = GROUP // HEAD_DIM
    ng = d // GROUP
    plans = [_cast_plan(w, bsz * ng * nq, lambda b, g, i: (b * ng + g) * nq + i) for w in cast_weights]
    outs = pl.pallas_call(
        functools.partial(_attn_kernel, tk=tk, lambda_init=lambda_init, ncast=len(plans)),
        out_shape=(jax.ShapeDtypeStruct((n, d), BF16),) + tuple(pln[2] for pln in plans),
        grid=(bsz, ng, nq),
        in_specs=[
            pl.BlockSpec((1, GROUP, tq), lambda b, g, i: (b, g, i)),
            pl.BlockSpec((t, GROUP), lambda b, g, i: (b, g)),
            pl.BlockSpec((1, t // tk, nh // 2, V_ROWS, tk), lambda b, g, i: (b, 0, g, 0, 0)),
            pl.BlockSpec((4, HEAD_DIM), lambda b, g, i: (0, 0)),
            pl.BlockSpec((V_DIM, 1), lambda b, g, i: (0, 0)),
        ] + [pln[0] for pln in plans],
        out_specs=(pl.BlockSpec((tq, GROUP), lambda b, g, i: (b * nq + i, g)),) + tuple(pln[1] for pln in plans),
        scratch_shapes=[pltpu.VMEM((nh, tk, tq), F32), pltpu.VMEM((nh, tk, tq), F32),
                        pltpu.VMEM((nh, 1, tq), F32), pltpu.VMEM((nh, 1, tq), F32),
                        pltpu.VMEM((nh, V_ROWS, tq), F32)],
        compiler_params=_cparams(("arbitrary", "arbitrary", "arbitrary")),
        name="diff_attn",
    )(qt, k, vt, lam_rows, g_col, *cast_weights)
    return outs[0], outs[1:]


def _ffn_kernel(a_ref, wo_ref, x_ref, mod_ref, g_ref, wg_ref, wu_ref, wd_ref, o_ref, *, tf):
    m = mod_ref[0]
    x = x_ref[...] + m[2:3] * _dot(a_ref[...], wo_ref[...])
    h = _norm_mod(x, g_ref[...], m[3:4], m[4:5]).astype(BF16)
    f = wg_ref.shape[1]
    acc = None
    for ci in range(f // tf):
        sl = slice(ci * tf, (ci + 1) * tf)
        gt = _dot(h, wg_ref[:, sl])
        up = _dot(h, wu_ref[:, sl])
        a = (gt * _sigmoid(gt) * up).astype(BF16)
        part = _dot(a, wd_ref[sl, :])
        acc = part if acc is None else acc + part
    o_ref[...] = x + m[5:6] * acc


def _ffn(a, wo, x2, mod, g, wg, wu, wd, bsz, tm, tf):
    n, d = x2.shape
    f = wg.shape[1]
    tiles_per_b = n // bsz // tm
    row = lambda i: (i, 0)
    const = lambda i: (0, 0)
    return pl.pallas_call(
        functools.partial(_ffn_kernel, tf=tf),
        out_shape=jax.ShapeDtypeStruct((n, d), F32),
        grid=(n // tm,),
        in_specs=[
            pl.BlockSpec((tm, d), row),
            pl.BlockSpec((d, d), const),
            pl.BlockSpec((tm, d), row),
            pl.BlockSpec((1, 8, d), lambda i: (i // tiles_per_b, 0, 0)),
            pl.BlockSpec((1, d), const),
            pl.BlockSpec((d, f), const),
            pl.BlockSpec((d, f), const),
            pl.BlockSpec((f, d), const),
        ],
        out_specs=pl.BlockSpec((tm, d), row),
        compiler_params=_cparams(("arbitrary",)),
        name="oproj_dense_swiglu",
    )(a, wo, x2, mod, g, wg, wu, wd)


def _route_top2(wr_t, h, idx_ref, gate_ref):
    logits = lax.dot_general(wr_t, h, (((1,), (1,)), ((), ())),
                             preferred_element_type=F32, precision=lax.Precision.HIGHEST)
    ne = logits.shape[0]
    eidx = lax.broadcasted_iota(jnp.int32, logits.shape, 0)
    v1 = jnp.max(logits, axis=0, keepdims=True)
    i1 = jnp.min(jnp.where(logits == v1, eidx, ne), axis=0, keepdims=True)
    rest = jnp.where(eidx == i1, -jnp.inf, logits)
    v2 = jnp.max(rest, axis=0, keepdims=True)
    i2 = jnp.min(jnp.where(rest == v2, eidx, ne), axis=0, keepdims=True)
    e2 = jnp.exp(v2 - v1)
    den = 1.0 + e2
    idx_ref[0:1, :] = i1
    idx_ref[1:2, :] = i2
    gate_ref[0:1, :] = 1.0 / den
    gate_ref[1:2, :] = e2 / den


def _conformer_kernel(x_ref, mod_ref, gm_ref, w1_ref, b1_ref, wdw_ref, bdw_ref, lg_ref, lb_ref, w2_ref, b2_ref,
                      gf_ref, wr_ref, x_out_ref, h_ref, idx_ref, gate_ref, ext, vbuf, *, tiles_per_b, rb):
    tm, d = x_ref.shape
    nchunks = d // LANES
    first = (pl.program_id(0) % tiles_per_b) == 0

    @pl.when(first)
    def _():
        ext[:, 0:CONV_HALO, :] = jnp.zeros((nchunks, CONV_HALO, LANES), F32)

    @pl.when(jnp.logical_not(first))
    def _():
        ext[:, 0:CONV_HALO, :] = ext[:, tm:tm + CONV_HALO, :]

    m = mod_ref[0]
    x = x_ref[...]
    h = _norm_mod(x, gm_ref[...], m[0:1], m[1:2]).astype(BF16)
    off = CONV_HALO - (CONV_WIDTH - 1)
    tn = 2 * LANES
    for ci in range(d // tn):
        a = _dot(h, w1_ref[:, ci * tn:(ci + 1) * tn]) + b1_ref[:, ci * tn:(ci + 1) * tn]
        gt = _dot(h, w1_ref[:, d + ci * tn:d + (ci + 1) * tn]) + b1_ref[:, d + ci * tn:d + (ci + 1) * tn]
        u = a * _sigmoid(gt)
        for half in range(tn // LANES):
            c = ci * (tn // LANES) + half
            cols = slice(c * LANES, (c + 1) * LANES)
            ext[c, CONV_HALO:, :] = u[:, half * LANES:(half + 1) * LANES]
            for r in range(tm // rb):
                acc = jnp.broadcast_to(bdw_ref[:, cols], (rb, LANES))
                for kk in range(CONV_WIDTH):
                    acc = acc + wdw_ref[kk:kk + 1, cols] * ext[c, r * rb + off + kk:r * rb + off + kk + rb, :]
                vbuf[r * rb:(r + 1) * rb, cols] = acc
    v = vbuf[...]
    mu = jnp.mean(v, axis=-1, keepdims=True)
    vc = v - mu
    var = jnp.mean(vc * vc, axis=-1, keepdims=True)
    y = vc * lax.rsqrt(var + EPS) * lg_ref[...] + lb_ref[...]
    y = (y * _sigmoid(y)).astype(BF16)
    x = x + m[2:3] * (_dot(y, w2_ref[...]) + b2_ref[...])
    x_out_ref[...] = x
    hf = _norm_mod(x, gf_ref[...], m[3:4], m[4:5])
    h_ref[...] = hf
    _route_top2(wr_ref[...], hf, idx_ref, gate_ref)


def _conformer(x2, mod, gm, w1, b1, wdw, bdw, ln_g, ln_b, w2, b2, gf, wr_t, bsz, tm):
    n, d = x2.shape
    ne = wr_t.shape[0]
    tiles_per_b = n // bsz // tm
    row = lambda i: (i, 0)
    const = lambda i: (0, 0)
    vec = pl.BlockSpec((1, d), const)
    return pl.pallas_call(
        functools.partial(_conformer_kernel, tiles_per_b=tiles_per_b, rb=64),
        out_shape=(jax.ShapeDtypeStruct((n, d), F32), jax.ShapeDtypeStruct((n, d), F32),
                   jax.ShapeDtypeStruct((TOP_K, n), jnp.int32), jax.ShapeDtypeStruct((TOP_K, n), F32)),
        grid=(n // tm,),
        in_specs=[
            pl.BlockSpec((tm, d), row),
            pl.BlockSpec((1, 8, d), lambda i: (i // tiles_per_b, 0, 0)),
            vec,
            pl.BlockSpec((d, 2 * d), const),
            pl.BlockSpec((1, 2 * d), const),
            pl.BlockSpec((CONV_WIDTH, d), const),
            vec, vec, vec,
            pl.BlockSpec((d, d), const),
            vec, vec,
            pl.BlockSpec((ne, d), const),
        ],
        out_specs=(pl.BlockSpec((tm, d), row), pl.BlockSpec((tm, d), row),
                   pl.BlockSpec((TOP_K, tm), lambda i: (0, i)), pl.BlockSpec((TOP_K, tm), lambda i: (0, i))),
        scratch_shapes=[pltpu.VMEM((d // LANES, tm + CONV_HALO, LANES), F32), pltpu.VMEM((tm, d), F32)],
        compiler_params=_cparams(("arbitrary",)),
        name="conformer_router",
    )(x2, mod, gm, w1, b1, wdw, bdw, ln_g, ln_b, w2, b2, gf, wr_t)


def _moe_kernel(be_ref, na_ref, cnt_ref, src_ref, dst_ref, h_hbm, wg_ref, wu_ref, wd_ref, y_hbm,
                xbuf, xb16, acc, gsem, ssem, *, n_assign, n_spare_blocks):
    tm = xb16.shape[0]
    nparts, prows = acc.shape[1:3]
    nblk = pl.num_programs(0)
    rpb = tm // LANES
    blk = pl.program_id(0)
    f = pl.program_id(1)
    na = na_ref[0]
    active = blk < na
    slot = blk % 2
    other = 1 - slot

    def gather_row(trow, col, buf, part, r):
        return pltpu.make_async_copy(h_hbm.at[pl.ds(src_ref[trow, col], 1)],
                                     xbuf.at[buf, part, pl.ds(r, 1)], gsem.at[buf])

    def scatter_row(trow, col, buf, part, r):
        return pltpu.make_async_copy(acc.at[buf, part, pl.ds(r, 1)], y_hbm.at[pl.ds(dst_ref[trow, col], 1)],
                                     ssem.at[buf])

    def wait_gather(buf):
        pltpu.make_async_copy(xbuf.at[buf], xbuf.at[buf], gsem.at[buf]).wait()

    def wait_scatter(buf):
        pltpu.make_async_copy(acc.at[buf], acc.at[buf], ssem.at[buf]).wait()

    def scatter_block(b, buf):
        def body(i, carry):
            scatter_row(b * rpb + i // LANES, i % LANES, buf, i // prows, i % prows).start()
            return carry
        lax.fori_loop(0, tm, body, 0)
        wait_scatter(buf)

    @pl.when(jnp.logical_and(blk == 0, f == 0))
    def _():
        def body(i, carry):
            gather_row(i // LANES, i % LANES, 0, i // prows, i % prows).start()
            return carry
        lax.fori_loop(0, tm, body, 0)
        acc[1] = jnp.zeros(acc.shape[1:], F32)
        spare_fill = [pltpu.make_async_copy(acc.at[1, 0], y_hbm.at[pl.ds(n_assign + j * prows, prows)],
                                            ssem.at[1])
                      for j in range(n_spare_blocks * nparts)]
        for cp in spare_fill:
            cp.start()
        for cp in spare_fill:
            cp.wait()

    @pl.when(jnp.logical_and(active, f == 0))
    def _():
        wait_gather(slot)
        xb16[...] = xbuf[slot].reshape(tm, xb16.shape[1]).astype(BF16)

        @pl.when(blk >= 1)
        def _():
            wait_scatter(slot)
        acc[slot] = jnp.zeros(acc.shape[1:], F32)

    def ffn_rows(part):
        xb = xb16[part * prows:(part + 1) * prows]
        gt = _dot(xb, wg_ref[0])
        up = _dot(xb, wu_ref[0])
        acc[slot, part] += _dot((gt * _sigmoid(gt) * up).astype(BF16), wd_ref[0])

    @pl.when(active)
    def _():
        qrows = prows // LANES
        gbase = (blk + 1) * rpb + f * qrows
        sbase = jnp.where(blk == 0, nblk, blk - 1) * rpb + f * qrows
        for i in range(prows):
            gather_row(gbase + i // LANES, i % LANES, other, f, i).start()
            scatter_row(sbase + i // LANES, i % LANES, other, f, i).start()
        ffn_rows(0)

    assert nparts == 2
    @pl.when(jnp.logical_and(active, cnt_ref[blk] > prows))
    def _():
        ffn_rows(1)

    @pl.when(jnp.logical_and(f == 0, blk == na))
    def _():
        wait_gather(slot)
        wait_scatter(slot)
        scatter_block(blk - 1, other)

    @pl.when(jnp.logical_and(jnp.logical_and(f == nparts - 1, blk == nblk - 1), active))
    def _():
        wait_gather(other)
        wait_scatter(other)
        scatter_block(blk, slot)


def _moe(block_e, n_active, block_cnt, src_tab, dst_tab, h, wg, wu, wd, n_out, tm, tf):
    n, d = h.shape
    ne, _, fe = wg.shape
    nf = fe // tf
    nblk = dst_tab.shape[0] * LANES // tm - 1
    assert tm % (nf * LANES) == 0

    def wsel(blk, f, na):
        return jnp.where(blk < na[0], f, nf - 1)

    grid_spec = pltpu.PrefetchScalarGridSpec(
        num_scalar_prefetch=5,
        grid=(nblk, nf),
        in_specs=[
            pl.BlockSpec(memory_space=pl.ANY),
            pl.BlockSpec((1, d, tf), lambda blk, f, be, na, cnt, src, dst: (be[blk], 0, wsel(blk, f, na))),
            pl.BlockSpec((1, d, tf), lambda blk, f, be, na, cnt, src, dst: (be[blk], 0, wsel(blk, f, na))),
            pl.BlockSpec((1, tf, d), lambda blk, f, be, na, cnt, src, dst: (be[blk], wsel(blk, f, na), 0)),
        ],
        out_specs=pl.BlockSpec(memory_space=pl.ANY),
        scratch_shapes=[pltpu.VMEM((2, nf, tm // nf, d), F32), pltpu.VMEM((tm, d), BF16),
                        pltpu.VMEM((2, nf, tm // nf, d), F32),
                        pltpu.SemaphoreType.DMA((2,)), pltpu.SemaphoreType.DMA((2,))],
    )
    return pl.pallas_call(
        functools.partial(_moe_kernel, n_assign=n * TOP_K, n_spare_blocks=ne),
        out_shape=jax.ShapeDtypeStruct((n_out, d), F32),
        grid_spec=grid_spec,
        compiler_params=_cparams(("arbitrary", "arbitrary")),
        name="moe_experts",
    )(block_e, n_active, block_cnt, src_tab, dst_tab, h, wg, wu, wd)


def _combine_kernel(y0_ref, y1_ref, x_ref, gates_ref, mod_ref, g_ref, o_ref):
    gate = mod_ref[0][5:6]
    gt = gates_ref[...]
    x = x_ref[...] + gate * (gt[:, 0:1] * y0_ref[...] + gt[:, 1:2] * y1_ref[...])
    o_ref[...] = x * lax.rsqrt(jnp.mean(x * x, axis=-1, keepdims=True) + EPS) * g_ref[...]


def _combine(y, x2, gates_t, mod, final_g, bsz, tm):
    n, d = x2.shape
    tiles_per_b = n // bsz // tm
    nt = n // tm
    row = lambda i: (i, 0)
    return pl.pallas_call(
        _combine_kernel,
        out_shape=jax.ShapeDtypeStruct((n, d), F32),
        grid=(nt,),
        in_specs=[
            pl.BlockSpec((tm, d), row),
            pl.BlockSpec((tm, d), lambda i: (nt + i, 0)),
            pl.BlockSpec((tm, d), row),
            pl.BlockSpec((tm, TOP_K), row),
            pl.BlockSpec((1, 8, d), lambda i: (i // tiles_per_b, 0, 0)),
            pl.BlockSpec((1, d), lambda i: (0, 0)),
        ],
        out_specs=pl.BlockSpec((tm, d), row),
        compiler_params=_cparams(("arbitrary",)),
        name="moe_combine_final_norm",
    )(y, y, x2, gates_t, mod, final_g)


def _routing(idx, ne, tm):
    n = idx.shape[1]
    a_tot = n * TOP_K
    flat_e = idx.T.reshape(-1)
    sizes = jnp.sum((flat_e[:, None] == jnp.arange(ne, dtype=jnp.int32)[None, :]).astype(jnp.int32), axis=0)
    padded = (sizes + tm - 1) // tm * tm
    pad_end = jnp.cumsum(padded)
    pad_start = pad_end - padded
    grp_start = jnp.cumsum(sizes) - sizes
    p = a_tot + ne * tm
    nblk = p // tm
    n_active = (pad_end[-1] // tm).astype(jnp.int32)
    blk_start = jnp.arange(nblk, dtype=jnp.int32) * tm
    block_e = jnp.sum((blk_start[:, None] >= pad_end[None, :]).astype(jnp.int32), axis=1)
    last_e = jnp.sum((jnp.maximum(pad_end[-1] - 1, 0) >= pad_end).astype(jnp.int32))
    block_e = jnp.where(jnp.arange(nblk) < n_active, block_e, last_e).astype(jnp.int32)
    order = jnp.argsort(flat_e).astype(jnp.int32)
    within0 = blk_start - pad_start[block_e]
    size_b = sizes[block_e]
    grp_b = grp_start[block_e]
    lane = jnp.arange(tm, dtype=jnp.int32)[None, :]
    within = within0[:, None] + lane
    valid = within < size_b[:, None]
    seg = order[jnp.clip(grp_b[:, None] + within, 0, a_tot - 1)]
    src_tok = jnp.where(valid, seg // TOP_K, 0).astype(jnp.int32)
    spare = a_tot + blk_start[:, None] + lane - grp_b[:, None] - jnp.minimum(within, size_b[:, None])
    dst_row = jnp.where(valid, (seg % TOP_K) * n + seg // TOP_K, spare).astype(jnp.int32)
    src_tab = jnp.pad(src_tok.reshape(-1), (0, tm)).reshape(-1, LANES)
    dst_tab = jnp.concatenate([dst_row.reshape(-1), p + jnp.arange(tm, dtype=jnp.int32)]).reshape(-1, LANES)
    block_cnt = jnp.clip(size_b - within0, 0, tm).astype(jnp.int32)
    return block_e, n_active.reshape(1), block_cnt, src_tab, dst_tab, p + tm


def _rope_perm(d):
    perm = np.zeros((d,), np.int32)
    for h in range(d // HEAD_DIM):
        gi, r = divmod(h, 4)
        for dd in range(HEAD_DIM):
            new = gi * GROUP + (dd // HALF) * LANES + r * HALF + dd % HALF
            perm[new] = h * HEAD_DIM + dd
    return perm


def _mod_rows(mod_l, bsz, d):
    m = mod_l[:bsz].reshape(bsz, 6, d)
    return jnp.pad(m, ((0, 0), (0, 2), (0, 0)))


@jax.jit
def kernel(x, c, positions, w_ada, b_ada, norm_mix_g, norm_ffn_g, attn_w_qkv, attn_w_o, lam_q1, lam_k1,
           lam_q2, lam_k2, attn_subln_g, conv_w_pw1, conv_b_pw1, conv_w_dw, conv_b_dw, conv_ln_g,
           conv_ln_b, conv_w_pw2, conv_b_pw2, ffn_w_gate, ffn_w_up, ffn_w_down, moe_w_router,
           moe_w_gate, moe_w_up, moe_w_down, final_g):
    bsz, t, d = x.shape
    n = bsz * t
    depth = w_ada.shape[0]
    assert depth == 2 and d % GROUP == 0 and bsz <= 8
    tm = min(512, t)
    tk = min(256, t)
    x2 = x.reshape(n, d)

    c8 = jnp.pad(c, ((0, 8 - bsz), (0, 0)))
    mod = _ada(c8, w_ada, b_ada)
    mod0 = _mod_rows(mod[0], bsz, d)
    mod1 = _mod_rows(mod[1], bsz, d)

    inv_freq = ROPE_THETA ** (-jnp.arange(HALF, dtype=F32) / HALF)
    perm = _rope_perm(d)
    w_qkv = attn_w_qkv[0]
    wqt = w_qkv[:, perm].T.astype(BF16)
    wk = w_qkv[:, d + perm].astype(BF16)
    wvt = w_qkv[:, 2 * d:].T.astype(BF16)
    qt, k, vt, (ffn_wg, ffn_wu, ffn_wd, w_o, w_pw1, w_pw2) = _qkv(
        x2, mod0, norm_mix_g[0:1], positions.reshape(n // tm, 1, tm), inv_freq.reshape(HALF, 1),
        wqt, wk, wvt, bsz, tm, tk,
        [ffn_w_gate[0], ffn_w_up[0], ffn_w_down[0], attn_w_o[0], conv_w_pw1[0], conv_w_pw2[0]])
    lam_rows = jnp.concatenate([lam_q1[0:1], lam_k1[0:1], lam_q2[0:1], lam_k2[0:1]], axis=0)
    lambda_init = 0.8 - 0.6 * math.exp(-0.3 * 0)
    ne, _, fe = moe_w_gate.shape[1:]
    o, (moe_wg, moe_wu, moe_wd) = _attention(
        qt, k, vt, lam_rows, attn_subln_g[0].reshape(V_DIM, 1), tm, tk, lambda_init,
        [moe_w_gate[0].reshape(ne * d, fe), moe_w_up[0].reshape(ne * d, fe), moe_w_down[0].reshape(ne * fe, d)])
    x2 = _ffn(o, w_o, x2, mod0, norm_ffn_g[0:1], ffn_wg, ffn_wu, ffn_wd, bsz, tm, tf=256)

    x2, h, idx, gates = _conformer(x2, mod1, norm_mix_g[1:2], w_pw1, conv_b_pw1[0:1], conv_w_dw[0],
                                   conv_b_dw[0:1], conv_ln_g[0:1], conv_ln_b[0:1], w_pw2, conv_b_pw2[0:1],
                                   norm_ffn_g[1:2], moe_w_router[0].T, bsz, tm)
    tm_e = 2 * tm
    block_e, n_active, block_cnt, src_tab, dst_tab, n_out = _routing(idx, ne, tm_e)
    tf_e = 1792 if fe % 1792 == 0 else fe
    y = _moe(block_e, n_active, block_cnt, src_tab, dst_tab, h, moe_wg.reshape(ne, d, fe),
             moe_wu.reshape(ne, d, fe), moe_wd.reshape(ne, fe, d), n_out, tm_e, tf_e)
    out = _combine(y, x2, gates.T, mod1, final_g.reshape(1, d), bsz, tm)
    return out.reshape(bsz, t, d)
```

```python
import functools
import math

import numpy as np
import jax
import jax.numpy as jnp
from jax import lax
from jax.experimental import pallas as pl
from jax.experimental.pallas import tpu as pltpu

EPS = 1e-6
ROPE_THETA = 10000.0
HEAD_DIM = 64
V_DIM = 128
V_ROWS = V_DIM + 16
HALF = HEAD_DIM // 2
LANES = 128
GROUP = 4 * HEAD_DIM
CONV_WIDTH = 31
CONV_HALO = 32
TOP_K = 2
NEG = -1e30
VMEM_LIMIT = 56 * 1024 * 1024

BF16 = jnp.bfloat16
F32 = jnp.float32


def _cparams(sem):
    return pltpu.CompilerParams(dimension_semantics=sem, vmem_limit_bytes=VMEM_LIMIT)


def _dot(a, b):
    return jnp.dot(a, b, preferred_element_type=F32)


def _dot_nt(a, b):
    return lax.dot_general(a, b, (((1,), (1,)), ((), ())), preferred_element_type=F32)


def _norm_mod(x, g, shift, scale):
    y = x * lax.rsqrt(jnp.mean(x * x, axis=-1, keepdims=True) + EPS)
    return (y * g) * (1.0 + scale) + shift


def _sigmoid(x):
    return 1.0 / (1.0 + jnp.exp(-x))


SLAB = 8


def _store_slabs(ref, val):
    rows = val.shape[0]
    for s in range(SLAB):
        ref[pl.ds(s, rows, stride=SLAB), :] = val[:, s * LANES:(s + 1) * LANES]


def _load_slab_cols(ref, s, rows):
    return ref[pl.ds(s, rows, stride=SLAB), :]


BF16_SUBLANES = 16


def _cast_plan(w2d, nsteps, step_of):
    rows, cols = w2d.shape
    share = 1
    while rows % (nsteps // share) or (rows // (nsteps // share)) % BF16_SUBLANES:
        share *= 2
        assert share <= nsteps, (rows, nsteps)
    rb = rows // (nsteps // share)
    imap = lambda *ids: (step_of(*ids) // share, 0)
    return (pl.BlockSpec((rb, cols), imap), pl.BlockSpec((rb, cols), imap),
            jax.ShapeDtypeStruct((rows, cols), BF16))


def _cast_blocks(refs):
    k = len(refs) // 2
    for src, dst in zip(refs[:k], refs[k:]):
        dst[...] = src[...].astype(BF16)


def _ada_kernel(c_ref, w_ref, b_ref, o_ref):
    c = c_ref[...]
    ca = (c * _sigmoid(c)).astype(BF16)
    o_ref[0] = _dot(ca, w_ref[0].astype(BF16)) + b_ref[0]


def _ada(c8, w_ada, b_ada):
    depth, d, d6 = w_ada.shape
    tn = 1536
    return pl.pallas_call(
        _ada_kernel,
        out_shape=jax.ShapeDtypeStruct((depth, 8, d6), F32),
        grid=(depth, d6 // tn),
        in_specs=[
            pl.BlockSpec((8, d), lambda l, j: (0, 0)),
            pl.BlockSpec((1, d, tn), lambda l, j: (l, 0, j)),
            pl.BlockSpec((1, 1, tn), lambda l, j: (l, 0, j)),
        ],
        out_specs=pl.BlockSpec((1, 8, tn), lambda l, j: (l, 0, j)),
        compiler_params=_cparams(("arbitrary", "arbitrary")),
        name="ada_mod",
    )(c8, w_ada, b_ada.reshape(depth, 1, d6))


def _qkv_kernel(x_ref, mod_ref, g_ref, pos_ref, freq_ref, wqt_ref, wk_ref, wvt_ref, *rest, tk, ncast):
    qt_ref, k_ref, vt_ref = rest[ncast:ncast + 3]
    _cast_blocks(rest[:ncast] + rest[ncast + 3:])
    m = mod_ref[0]
    h = _norm_mod(x_ref[...], g_ref[...], m[0:1], m[1:2]).astype(BF16)
    d = x_ref.shape[1]
    tm = x_ref.shape[0]
    ang = freq_ref[...] * pos_ref[0].astype(F32)
    reps = LANES // HALF
    ct = jnp.concatenate([jnp.cos(ang)] * reps, axis=0)
    st = jnp.concatenate([jnp.sin(ang)] * reps, axis=0)
    c = ct.T
    s = st.T
    for gi in range(d // GROUP):
        y = _dot(h, wk_ref[:, gi * GROUP:(gi + 1) * GROUP])
        a = y[:, :LANES]
        b = y[:, LANES:]
        k_ref[:, gi * GROUP:gi * GROUP + LANES] = (a * c - b * s).astype(BF16)
        k_ref[:, gi * GROUP + LANES:(gi + 1) * GROUP] = (b * c + a * s).astype(BF16)
    mult = HEAD_DIM ** -0.5 * math.log2(math.e)
    for gi in range(d // GROUP):
        y = _dot_nt(wqt_ref[gi * GROUP:(gi + 1) * GROUP, :], h)
        a = y[:LANES]
        b = y[LANES:]
        qt_ref[0, gi * GROUP:gi * GROUP + LANES, :] = ((a * ct - b * st) * mult).astype(BF16)
        qt_ref[0, gi * GROUP + LANES:(gi + 1) * GROUP, :] = ((b * ct + a * st) * mult).astype(BF16)
    ones = jnp.ones((V_ROWS - V_DIM, tk), BF16)
    for gi in range(d // GROUP):
        vt = _dot_nt(wvt_ref[gi * GROUP:(gi + 1) * GROUP, :], h)
        for hh in range(GROUP // V_DIM):
            for ci in range(tm // tk):
                vt_ref[0, ci, 2 * gi + hh, 0:V_DIM, :] = (
                    vt[hh * V_DIM:(hh + 1) * V_DIM, ci * tk:(ci + 1) * tk].astype(BF16))
                vt_ref[0, ci, 2 * gi + hh, V_DIM:V_ROWS, :] = ones


def _qkv(x2, mod, g, pos_row, freq_col, wqt, wk, wvt, bsz, tm, tk, cast_weights):
    n, d = x2.shape
    t = n // bsz
    tiles_per_b = t // tm
    nh = d // V_DIM
    row = lambda i: (i, 0)
    const = lambda i: (0, 0)
    plans = [_cast_plan(w, n // tm, lambda i: i) for w in cast_weights]
    outs = pl.pallas_call(
        functools.partial(_qkv_kernel, tk=tk, ncast=len(plans)),
        out_shape=(jax.ShapeDtypeStruct((bsz, d, t), BF16), jax.ShapeDtypeStruct((n, d), BF16),
                   jax.ShapeDtypeStruct((bsz, t // tk, nh, V_ROWS, tk), BF16)) + tuple(p[2] for p in plans),
        grid=(n // tm,),
        in_specs=[
            pl.BlockSpec((tm, d), row),
            pl.BlockSpec((1, 8, d), lambda i: (i // tiles_per_b, 0, 0)),
            pl.BlockSpec((1, d), const),
            pl.BlockSpec((1, 1, tm), lambda i: (i, 0, 0)),
            pl.BlockSpec((HALF, 1), const),
            pl.BlockSpec((d, d), const),
            pl.BlockSpec((d, d), const),
            pl.BlockSpec((d, d), const),
        ] + [p[0] for p in plans],
        out_specs=(pl.BlockSpec((1, d, tm), lambda i: (i // tiles_per_b, 0, i % tiles_per_b)),
                   pl.BlockSpec((tm, d), row),
                   pl.BlockSpec((1, tm // tk, nh, V_ROWS, tk),
                                lambda i: (i // tiles_per_b, i % tiles_per_b, 0, 0, 0)))
        + tuple(p[1] for p in plans),
        compiler_params=_cparams(("arbitrary",)),
        name="qkv_rope",
    )(x2, mod, g, pos_row, freq_col, wqt, wk, wvt, *cast_weights)
    return outs[0], outs[1], outs[2], outs[3:]


def _attn_kernel(qt_ref, k_ref, vt_ref, lam_ref, g_ref, *rest, tk, lambda_init, ncast):
    o_ref = rest[ncast]
    s_a, s_b, mc_a, mc_b, acc = rest[2 * ncast + 1:]
    _cast_blocks(rest[:ncast] + rest[ncast + 1:2 * ncast + 1])
    tq = qt_ref.shape[2]
    nh = acc.shape[0]
    qi = pl.program_id(2)
    n_full = qi * (tq // tk)
    row = lax.broadcasted_iota(jnp.int32, (GROUP, 1), 0)
    head = (row % LANES) // HALF
    qf = qt_ref[0].astype(F32)
    qm = [jnp.where(head == hd, qf, 0.0).astype(BF16) for hd in range(nh)]

    def scores(kj, s_ref, mc_ref, masked, lo=0):
        kc = k_ref[pl.ds(pl.multiple_of(kj * tk, tk), tk), :]
        for hd in range(nh):
            s = _dot(kc, qm[hd][:, lo:])
            if masked:
                kidx = kj * tk + lax.broadcasted_iota(jnp.int32, s.shape, 0)
                qidx = qi * tq + lo + lax.broadcasted_iota(jnp.int32, s.shape, 1)
                s = jnp.where(kidx <= qidx, s, NEG)
            s_ref[hd, :, lo:] = s
            mc_ref[hd, :, lo:] = jnp.max(s, axis=0, keepdims=True)

    def softmax_pv(kj, s_ref, mc_ref, ms, first, lo=0):
        out = []
        for hd in range(nh):
            vc = vt_ref[0, kj, hd // 2]
            s = s_ref[hd, :, lo:]
            mc = mc_ref[hd, :, lo:]
            m_new = mc if first else jnp.maximum(ms[hd][:, lo:], mc)
            pv = _dot(vc, jnp.exp2(s - m_new).astype(BF16))
            if first:
                acc[hd] = pv
            else:
                acc[hd, :, lo:] = jnp.exp2(ms[hd][:, lo:] - m_new) * acc[hd, :, lo:] + pv
            out.append(m_new if lo == 0 else jnp.concatenate([ms[hd][:, :lo], m_new], axis=1))
        return tuple(out)

    assert tq == 2 * tk
    scores(n_full, s_a, mc_a, True)
    scores(n_full + 1, s_b, mc_b, True, lo=tk)
    ms = softmax_pv(n_full, s_a, mc_a, None, True)
    scores(0, s_a, mc_a, False)
    ms = softmax_pv(n_full + 1, s_b, mc_b, ms, False, lo=tk)

    def body(jj, ms):
        j = 2 * jj
        scores(j + 1, s_b, mc_b, False)
        ms = softmax_pv(j, s_a, mc_a, ms, False)
        scores(j + 2, s_a, mc_a, False)
        return softmax_pv(j + 1, s_b, mc_b, ms, False)

    ms = lax.fori_loop(0, n_full // 2 - 1, body, ms)

    @pl.when(n_full > 0)
    def _():
        scores(n_full - 1, s_b, mc_b, False)
        ms2 = softmax_pv(n_full - 2, s_a, mc_a, ms, False)
        softmax_pv(n_full - 1, s_b, mc_b, ms2, False)

    lp = lam_ref[...]
    lam = (jnp.exp(jnp.sum(lp[0:1] * lp[1:2], axis=-1, keepdims=True))
           - jnp.exp(jnp.sum(lp[2:3] * lp[3:4], axis=-1, keepdims=True)) + lambda_init)
    for pr in range(nh // 2):
        a1 = acc[2 * pr]
        a2 = acc[2 * pr + 1]
        o = a1[:V_DIM] / a1[V_DIM:V_DIM + 1] - lam * (a2[:V_DIM] / a2[V_DIM:V_DIM + 1])
        o = o * lax.rsqrt(jnp.mean(o * o, axis=0, keepdims=True) + EPS)
        o = (o * g_ref[...]) * (1.0 - lambda_init)
        o_ref[:, pr * V_DIM:(pr + 1) * V_DIM] = o.T.astype(BF16)


def _attention(qt, k, vt, lam_rows, g_col, tq, tk, lambda_init, cast_weights):
    bsz, d, t = qt.shape
    n = bsz * t
    nq = t // tq
    nh = GROUP // HEAD_DIM
    ng = d // GROUP
    plans = [_cast_plan(w, bsz * ng * nq, lambda b, g, i: (b * ng + g) * nq + i) for w in cast_weights]
    outs = pl.pallas_call(
        functools.partial(_attn_kernel, tk=tk, lambda_init=lambda_init, ncast=len(plans)),
        out_shape=(jax.ShapeDtypeStruct((n, d), BF16),) + tuple(pln[2] for pln in plans),
        grid=(bsz, ng, nq),
        in_specs=[
            pl.BlockSpec((1, GROUP, tq), lambda b, g, i: (b, g, i)),
            pl.BlockSpec((t, GROUP), lambda b, g, i: (b, g)),
            pl.BlockSpec((1, t // tk, nh // 2, V_ROWS, tk), lambda b, g, i: (b, 0, g, 0, 0)),
            pl.BlockSpec((4, HEAD_DIM), lambda b, g, i: (0, 0)),
            pl.BlockSpec((V_DIM, 1), lambda b, g, i: (0, 0)),
        ] + [pln[0] for pln in plans],
        out_specs=(pl.BlockSpec((tq, GROUP), lambda b, g, i: (b * nq + i, g)),) + tuple(pln[1] for pln in plans),
        scratch_shapes=[pltpu.VMEM((nh, tk, tq), F32), pltpu.VMEM((nh, tk, tq), F32),
                        pltpu.VMEM((nh, 1, tq), F32), pltpu.VMEM((nh, 1, tq), F32),
                        pltpu.VMEM((nh, V_ROWS, tq), F32)],
        compiler_params=_cparams(("arbitrary", "arbitrary", "arbitrary")),
        name="diff_attn",
    )(qt, k, vt, lam_rows, g_col, *cast_weights)
    return outs[0], outs[1:]


def _ffn_kernel(a_ref, wo_ref, x_ref, mod_ref, g_ref, wg_ref, wu_ref, wd_ref, o_ref, *, tf):
    m = mod_ref[0]
    x = x_ref[...] + m[2:3] * _dot(a_ref[...], wo_ref[...])
    h = _norm_mod(x, g_ref[...], m[3:4], m[4:5]).astype(BF16)
    f = wg_ref.shape[1]
    acc = None
    for ci in range(f // tf):
        sl = slice(ci * tf, (ci + 1) * tf)
        gt = _dot(h, wg_ref[:, sl])
        up = _dot(h, wu_ref[:, sl])
        a = (gt * _sigmoid(gt) * up).astype(BF16)
        part = _dot(a, wd_ref[sl, :])
        acc = part if acc is None else acc + part
    o_ref[...] = x + m[5:6] * acc


def _ffn(a, wo, x2, mod, g, wg, wu, wd, bsz, tm, tf):
    n, d = x2.shape
    f = wg.shape[1]
    tiles_per_b = n // bsz // tm
    row = lambda i: (i, 0)
    const = lambda i: (0, 0)
    return pl.pallas_call(
        functools.partial(_ffn_kernel, tf=tf),
        out_shape=jax.ShapeDtypeStruct((n, d), F32),
        grid=(n // tm,),
        in_specs=[
            pl.BlockSpec((tm, d), row),
            pl.BlockSpec((d, d), const),
            pl.BlockSpec((tm, d), row),
            pl.BlockSpec((1, 8, d), lambda i: (i // tiles_per_b, 0, 0)),
            pl.BlockSpec((1, d), const),
            pl.BlockSpec((d, f), const),
            pl.BlockSpec((d, f), const),
            pl.BlockSpec((f, d), const),
        ],
        out_specs=pl.BlockSpec((tm, d), row),
        compiler_params=_cparams(("arbitrary",)),
        name="oproj_dense_swiglu",
    )(a, wo, x2, mod, g, wg, wu, wd)


def _route_top2(wr_t, h, idx_ref, gate_ref):
    logits = lax.dot_general(wr_t, h, (((1,), (1,)), ((), ())),
                             preferred_element_type=F32, precision=lax.Precision.HIGHEST)
    ne = logits.shape[0]
    eidx = lax.broadcasted_iota(jnp.int32, logits.shape, 0)
    v1 = jnp.max(logits, axis=0, keepdims=True)
    i1 = jnp.min(jnp.where(logits == v1, eidx, ne), axis=0, keepdims=True)
    rest = jnp.where(eidx == i1, -jnp.inf, logits)
    v2 = jnp.max(rest, axis=0, keepdims=True)
    i2 = jnp.min(jnp.where(rest == v2, eidx, ne), axis=0, keepdims=True)
    e2 = jnp.exp(v2 - v1)
    den = 1.0 + e2
    idx_ref[0:1, :] = i1
    idx_ref[1:2, :] = i2
    gate_ref[0:1, :] = 1.0 / den
    gate_ref[1:2, :] = e2 / den


def _conformer_kernel(x_ref, mod_ref, gm_ref, w1_ref, b1_ref, wdw_ref, bdw_ref, lg_ref, lb_ref, w2_ref, b2_ref,
                      gf_ref, wr_ref, x_out_ref, h_ref, idx_ref, gate_ref, ext, vbuf, *, tiles_per_b, rb):
    tm, d = x_ref.shape
    nchunks = d // LANES
    first = (pl.program_id(0) % tiles_per_b) == 0

    @pl.when(first)
    def _():
        ext[:, 0:CONV_HALO, :] = jnp.zeros((nchunks, CONV_HALO, LANES), F32)

    @pl.when(jnp.logical_not(first))
    def _():
        ext[:, 0:CONV_HALO, :] = ext[:, tm:tm + CONV_HALO, :]

    m = mod_ref[0]
    x = x_ref[...]
    h = _norm_mod(x, gm_ref[...], m[0:1], m[1:2]).astype(BF16)
    off = CONV_HALO - (CONV_WIDTH - 1)
    tn = 2 * LANES
    for ci in range(d // tn):
        a = _dot(h, w1_ref[:, ci * tn:(ci + 1) * tn]) + b1_ref[:, ci * tn:(ci + 1) * tn]
        gt = _dot(h, w1_ref[:, d + ci * tn:d + (ci + 1) * tn]) + b1_ref[:, d + ci * tn:d + (ci + 1) * tn]
        u = a * _sigmoid(gt)
        for half in range(tn // LANES):
            c = ci * (tn // LANES) + half
            cols = slice(c * LANES, (c + 1) * LANES)
            ext[c, CONV_HALO:, :] = u[:, half * LANES:(half + 1) * LANES]
            for r in range(tm // rb):
                acc = jnp.broadcast_to(bdw_ref[:, cols], (rb, LANES))
                for kk in range(CONV_WIDTH):
                    acc = acc + wdw_ref[kk:kk + 1, cols] * ext[c, r * rb + off + kk:r * rb + off + kk + rb, :]
                vbuf[r * rb:(r + 1) * rb, cols] = acc
    v = vbuf[...]
    mu = jnp.mean(v, axis=-1, keepdims=True)
    vc = v - mu
    var = jnp.mean(vc * vc, axis=-1, keepdims=True)
    y = vc * lax.rsqrt(var + EPS) * lg_ref[...] + lb_ref[...]
    y = (y * _sigmoid(y)).astype(BF16)
    x = x + m[2:3] * (_dot(y, w2_ref[...]) + b2_ref[...])
    x_out_ref[...] = x
    hf = _norm_mod(x, gf_ref[...], m[3:4], m[4:5])
    _store_slabs(h_ref, hf)
    _route_top2(wr_ref[...], hf, idx_ref, gate_ref)


def _conformer(x2, mod, gm, w1, b1, wdw, bdw, ln_g, ln_b, w2, b2, gf, wr_t, bsz, tm):
    n, d = x2.shape
    ne = wr_t.shape[0]
    tiles_per_b = n // bsz // tm
    row = lambda i: (i, 0)
    const = lambda i: (0, 0)
    vec = pl.BlockSpec((1, d), const)
    return pl.pallas_call(
        functools.partial(_conformer_kernel, tiles_per_b=tiles_per_b, rb=64),
        out_shape=(jax.ShapeDtypeStruct((n, d), F32), jax.ShapeDtypeStruct((n * SLAB, LANES), F32),
                   jax.ShapeDtypeStruct((TOP_K, n), jnp.int32), jax.ShapeDtypeStruct((TOP_K, n), F32)),
        grid=(n // tm,),
        in_specs=[
            pl.BlockSpec((tm, d), row),
            pl.BlockSpec((1, 8, d), lambda i: (i // tiles_per_b, 0, 0)),
            vec,
            pl.BlockSpec((d, 2 * d), const),
            pl.BlockSpec((1, 2 * d), const),
            pl.BlockSpec((CONV_WIDTH, d), const),
            vec, vec, vec,
            pl.BlockSpec((d, d), const),
            vec, vec,
            pl.BlockSpec((ne, d), const),
        ],
        out_specs=(pl.BlockSpec((tm, d), row), pl.BlockSpec((tm * SLAB, LANES), row),
                   pl.BlockSpec((TOP_K, tm), lambda i: (0, i)), pl.BlockSpec((TOP_K, tm), lambda i: (0, i))),
        scratch_shapes=[pltpu.VMEM((d // LANES, tm + CONV_HALO, LANES), F32), pltpu.VMEM((tm, d), F32)],
        compiler_params=_cparams(("arbitrary",)),
        name="conformer_router",
    )(x2, mod, gm, w1, b1, wdw, bdw, ln_g, ln_b, w2, b2, gf, wr_t)


def _moe_kernel(be_ref, na_ref, cnt_ref, src_ref, dst_ref, h_hbm, wg_ref, wu_ref, wd_ref, y_hbm,
                xbuf, xb16, acc, ybuf, gsem, ssem, *, n_assign, n_spare_blocks):
    tm = xb16.shape[0]
    nparts, prows = acc.shape[0:2]
    nblk = pl.num_programs(0)
    rpb = tm // LANES
    blk = pl.program_id(0)
    f = pl.program_id(1)
    na = na_ref[0]
    active = blk < na
    slot = blk % 2
    other = 1 - slot

    def slab(row):
        start = row * SLAB
        return pl.ds(start if isinstance(start, int) else pl.multiple_of(start, SLAB), SLAB)

    def gather_row(trow, col, buf, part, r):
        return pltpu.make_async_copy(h_hbm.at[slab(src_ref[trow, col])], xbuf.at[buf, part, slab(r)],
                                     gsem.at[buf])

    def scatter_row(trow, col, buf, part, r):
        return pltpu.make_async_copy(ybuf.at[buf, part, slab(r)], y_hbm.at[slab(dst_ref[trow, col])],
                                     ssem.at[buf])

    def wait_gather(buf):
        pltpu.make_async_copy(xbuf.at[buf], xbuf.at[buf], gsem.at[buf]).wait()

    def wait_scatter(buf):
        pltpu.make_async_copy(ybuf.at[buf], ybuf.at[buf], ssem.at[buf]).wait()

    def scatter_block(b, buf):
        def body(i, carry):
            scatter_row(b * rpb + i // LANES, i % LANES, buf, i // prows, i % prows).start()
            return carry
        lax.fori_loop(0, tm, body, 0)
        wait_scatter(buf)

    @pl.when(jnp.logical_and(blk == 0, f == 0))
    def _():
        def body(i, carry):
            gather_row(i // LANES, i % LANES, 0, i // prows, i % prows).start()
            return carry
        lax.fori_loop(0, tm, body, 0)
        ybuf[1] = jnp.zeros(ybuf.shape[1:], F32)
        spare_fill = [pltpu.make_async_copy(
            ybuf.at[1, 0], y_hbm.at[pl.ds((n_assign + j * prows) * SLAB, prows * SLAB)], ssem.at[1])
            for j in range(n_spare_blocks * nparts)]
        for cp in spare_fill:
            cp.start()
        for cp in spare_fill:
            cp.wait()

    @pl.when(jnp.logical_and(active, f == 0))
    def _():
        wait_gather(slot)
        for part in range(nparts):
            for s in range(SLAB):
                xb16[part * prows:(part + 1) * prows, s * LANES:(s + 1) * LANES] = (
                    _load_slab_cols(xbuf.at[slot, part], s, prows).astype(BF16))
        acc[...] = jnp.zeros(acc.shape, F32)

    def ffn_rows(part):
        xb = xb16[part * prows:(part + 1) * prows]
        gt = _dot(xb, wg_ref[0])
        up = _dot(xb, wu_ref[0])
        acc[part] += _dot((gt * _sigmoid(gt) * up).astype(BF16), wd_ref[0])

    @pl.when(active)
    def _():
        qrows = prows // LANES
        gbase = (blk + 1) * rpb + f * qrows
        sbase = jnp.where(blk == 0, nblk, blk - 1) * rpb + f * qrows
        for i in range(prows):
            gather_row(gbase + i // LANES, i % LANES, other, f, i).start()
            scatter_row(sbase + i // LANES, i % LANES, other, f, i).start()
        ffn_rows(0)

    assert nparts == 2
    @pl.when(jnp.logical_and(active, cnt_ref[blk] > prows))
    def _():
        ffn_rows(1)

    @pl.when(jnp.logical_and(active, f == nparts - 1))
    def _():
        @pl.when(blk >= 1)
        def _():
            wait_scatter(slot)
        for part in range(nparts):
            _store_slabs(ybuf.at[slot, part], acc[part])

    @pl.when(jnp.logical_and(f == 0, blk == na))
    def _():
        wait_gather(slot)
        wait_scatter(slot)
        scatter_block(blk - 1, other)

    @pl.when(jnp.logical_and(jnp.logical_and(f == nparts - 1, blk == nblk - 1), active))
    def _():
        wait_gather(other)
        wait_scatter(other)
        scatter_block(blk, slot)


def _moe(block_e, n_active, block_cnt, src_tab, dst_tab, h, wg, wu, wd, n_out, tm, tf):
    n = h.shape[0] // SLAB
    ne, d, fe = wg.shape
    assert d == SLAB * LANES
    nf = fe // tf
    nblk = dst_tab.shape[0] * LANES // tm - 1
    assert tm % (nf * LANES) == 0

    def wsel(blk, f, na):
        return jnp.where(blk < na[0], f, nf - 1)

    grid_spec = pltpu.PrefetchScalarGridSpec(
        num_scalar_prefetch=5,
        grid=(nblk, nf),
        in_specs=[
            pl.BlockSpec(memory_space=pl.ANY),
            pl.BlockSpec((1, d, tf), lambda blk, f, be, na, cnt, src, dst: (be[blk], 0, wsel(blk, f, na))),
            pl.BlockSpec((1, d, tf), lambda blk, f, be, na, cnt, src, dst: (be[blk], 0, wsel(blk, f, na))),
            pl.BlockSpec((1, tf, d), lambda blk, f, be, na, cnt, src, dst: (be[blk], wsel(blk, f, na), 0)),
        ],
        out_specs=pl.BlockSpec(memory_space=pl.ANY),
        scratch_shapes=[pltpu.VMEM((2, nf, tm // nf * SLAB, LANES), F32), pltpu.VMEM((tm, d), BF16),
                        pltpu.VMEM((nf, tm // nf, d), F32), pltpu.VMEM((2, nf, tm // nf * SLAB, LANES), F32),
                        pltpu.SemaphoreType.DMA((2,)), pltpu.SemaphoreType.DMA((2,))],
    )
    return pl.pallas_call(
        functools.partial(_moe_kernel, n_assign=n * TOP_K, n_spare_blocks=ne),
        out_shape=jax.ShapeDtypeStruct((n_out * SLAB, LANES), F32),
        grid_spec=grid_spec,
        compiler_params=_cparams(("arbitrary", "arbitrary")),
        name="moe_experts",
    )(block_e, n_active, block_cnt, src_tab, dst_tab, h, wg, wu, wd)


def _combine_kernel(y0_ref, y1_ref, x_ref, gates_ref, mod_ref, g_ref, o_ref):
    tm, d = x_ref.shape
    gate = mod_ref[0][5:6]
    gt = gates_ref[...]
    g0 = gt[:, 0:1]
    g1 = gt[:, 1:2]
    ssq = jnp.zeros((tm, 1), F32)
    for s in range(SLAB):
        cols = slice(s * LANES, (s + 1) * LANES)
        xs = x_ref[:, cols] + gate[:, cols] * (g0 * _load_slab_cols(y0_ref, s, tm)
                                               + g1 * _load_slab_cols(y1_ref, s, tm))
        o_ref[:, cols] = xs
        ssq = ssq + jnp.sum(xs * xs, axis=-1, keepdims=True)
    o_ref[...] = o_ref[...] * lax.rsqrt(ssq * (1.0 / d) + EPS) * g_ref[...]


def _combine(y, x2, gates_t, mod, final_g, bsz, tm):
    n, d = x2.shape
    tiles_per_b = n // bsz // tm
    nt = n // tm
    row = lambda i: (i, 0)
    return pl.pallas_call(
        _combine_kernel,
        out_shape=jax.ShapeDtypeStruct((n, d), F32),
        grid=(nt,),
        in_specs=[
            pl.BlockSpec((tm * SLAB, LANES), row),
            pl.BlockSpec((tm * SLAB, LANES), lambda i: (nt + i, 0)),
            pl.BlockSpec((tm, d), row),
            pl.BlockSpec((tm, TOP_K), row),
            pl.BlockSpec((1, 8, d), lambda i: (i // tiles_per_b, 0, 0)),
            pl.BlockSpec((1, d), lambda i: (0, 0)),
        ],
        out_specs=pl.BlockSpec((tm, d), row),
        compiler_params=_cparams(("arbitrary",)),
        name="moe_combine_final_norm",
    )(y, y, x2, gates_t, mod, final_g)


def _routing(idx, ne, tm):
    n = idx.shape[1]
    a_tot = n * TOP_K
    flat_e = idx.T.reshape(-1)
    sizes = jnp.sum((flat_e[:, None] == jnp.arange(ne, dtype=jnp.int32)[None, :]).astype(jnp.int32), axis=0)
    padded = (sizes + tm - 1) // tm * tm
    pad_end = jnp.cumsum(padded)
    pad_start = pad_end - padded
    grp_start = jnp.cumsum(sizes) - sizes
    p = a_tot + ne * tm
    nblk = p // tm
    n_active = (pad_end[-1] // tm).astype(jnp.int32)
    blk_start = jnp.arange(nblk, dtype=jnp.int32) * tm
    block_e = jnp.sum((blk_start[:, None] >= pad_end[None, :]).astype(jnp.int32), axis=1)
    last_e = jnp.sum((jnp.maximum(pad_end[-1] - 1, 0) >= pad_end).astype(jnp.int32))
    block_e = jnp.where(jnp.arange(nblk) < n_active, block_e, last_e).astype(jnp.int32)
    order = jnp.argsort(flat_e).astype(jnp.int32)
    within0 = blk_start - pad_start[block_e]
    size_b = sizes[block_e]
    grp_b = grp_start[block_e]
    lane = jnp.arange(tm, dtype=jnp.int32)[None, :]
    within = within0[:, None] + lane
    valid = within < size_b[:, None]
    seg = order[jnp.clip(grp_b[:, None] + within, 0, a_tot - 1)]
    src_tok = jnp.where(valid, seg // TOP_K, 0).astype(jnp.int32)
    spare = a_tot + blk_start[:, None] + lane - grp_b[:, None] - jnp.minimum(within, size_b[:, None])
    dst_row = jnp.where(valid, (seg % TOP_K) * n + seg // TOP_K, spare).astype(jnp.int32)
    src_tab = jnp.pad(src_tok.reshape(-1), (0, tm)).reshape(-1, LANES)
    dst_tab = jnp.concatenate([dst_row.reshape(-1), p + jnp.arange(tm, dtype=jnp.int32)]).reshape(-1, LANES)
    block_cnt = jnp.clip(size_b - within0, 0, tm).astype(jnp.int32)
    return block_e, n_active.reshape(1), block_cnt, src_tab, dst_tab, p + tm


def _rope_perm(d):
    perm = np.zeros((d,), np.int32)
    for h in range(d // HEAD_DIM):
        gi, r = divmod(h, 4)
        for dd in range(HEAD_DIM):
            new = gi * GROUP + (dd // HALF) * LANES + r * HALF + dd % HALF
            perm[new] = h * HEAD_DIM + dd
    return perm


def _mod_rows(mod_l, bsz, d):
    m = mod_l[:bsz].reshape(bsz, 6, d)
    return jnp.pad(m, ((0, 0), (0, 2), (0, 0)))


@jax.jit
def kernel(x, c, positions, w_ada, b_ada, norm_mix_g, norm_ffn_g, attn_w_qkv, attn_w_o, lam_q1, lam_k1,
           lam_q2, lam_k2, attn_subln_g, conv_w_pw1, conv_b_pw1, conv_w_dw, conv_b_dw, conv_ln_g,
           conv_ln_b, conv_w_pw2, conv_b_pw2, ffn_w_gate, ffn_w_up, ffn_w_down, moe_w_router,
           moe_w_gate, moe_w_up, moe_w_down, final_g):
    bsz, t, d = x.shape
    n = bsz * t
    depth = w_ada.shape[0]
    assert depth == 2 and d % GROUP == 0 and bsz <= 8
    tm = min(512, t)
    tk = min(256, t)
    x2 = x.reshape(n, d)

    c8 = jnp.pad(c, ((0, 8 - bsz), (0, 0)))
    mod = _ada(c8, w_ada, b_ada)
    mod0 = _mod_rows(mod[0], bsz, d)
    mod1 = _mod_rows(mod[1], bsz, d)

    inv_freq = ROPE_THETA ** (-jnp.arange(HALF, dtype=F32) / HALF)
    perm = _rope_perm(d)
    w_qkv = attn_w_qkv[0]
    wqt = w_qkv[:, perm].T.astype(BF16)
    wk = w_qkv[:, d + perm].astype(BF16)
    wvt = w_qkv[:, 2 * d:].T.astype(BF16)
    qt, k, vt, (ffn_wg, ffn_wu, ffn_wd, w_o, w_pw1, w_pw2) = _qkv(
        x2, mod0, norm_mix_g[0:1], positions.reshape(n // tm, 1, tm), inv_freq.reshape(HALF, 1),
        wqt, wk, wvt, bsz, tm, tk,
        [ffn_w_gate[0], ffn_w_up[0], ffn_w_down[0], attn_w_o[0], conv_w_pw1[0], conv_w_pw2[0]])
    lam_rows = jnp.concatenate([lam_q1[0:1], lam_k1[0:1], lam_q2[0:1], lam_k2[0:1]], axis=0)
    lambda_init = 0.8 - 0.6 * math.exp(-0.3 * 0)
    ne, _, fe = moe_w_gate.shape[1:]
    o, (moe_wg, moe_wu, moe_wd) = _attention(
        qt, k, vt, lam_rows, attn_subln_g[0].reshape(V_DIM, 1), tm, tk, lambda_init,
        [moe_w_gate[0].reshape(ne * d, fe), moe_w_up[0].reshape(ne * d, fe), moe_w_down[0].reshape(ne * fe, d)])
    x2 = _ffn(o, w_o, x2, mod0, norm_ffn_g[0:1], ffn_wg, ffn_wu, ffn_wd, bsz, tm, tf=256)

    x2, h, idx, gates = _conformer(x2, mod1, norm_mix_g[1:2], w_pw1, conv_b_pw1[0:1], conv_w_dw[0],
                                   conv_b_dw[0:1], conv_ln_g[0:1], conv_ln_b[0:1], w_pw2, conv_b_pw2[0:1],
                                   norm_ffn_g[1:2], moe_w_router[0].T, bsz, tm)
    tm_e = 2 * tm
    block_e, n_active, block_cnt, src_tab, dst_tab, n_out = _routing(idx, ne, tm_e)
    tf_e = 1792 if fe % 1792 == 0 else fe
    y = _moe(block_e, n_active, block_cnt, src_tab, dst_tab, h, moe_wg.reshape(ne, d, fe),
             moe_wu.reshape(ne, d, fe), moe_wd.reshape(ne, fe, d), n_out, tm_e, tf_e)
    out = _combine(y, x2, gates.T, mod1, final_g.reshape(1, d), bsz, tm)
    return out.reshape(bsz, t, d)
```

```python
import functools
import math

import jax
import jax.numpy as jnp
from jax import lax
from jax.experimental import pallas as pl
from jax.experimental.pallas import tpu as pltpu

EPS = 1e-6
ROPE_THETA = 10000.0
HEAD_DIM = 64
V_DIM = 128
V_ROWS = V_DIM + 16
HALF = HEAD_DIM // 2
LANES = 128
GROUP = 4 * HEAD_DIM
CONV_WIDTH = 31
CONV_HALO = 32
TOP_K = 2
NEG = -1e30
VMEM_LIMIT = 56 * 1024 * 1024

BF16 = jnp.bfloat16
F32 = jnp.float32


def _cparams(sem):
    return pltpu.CompilerParams(dimension_semantics=sem, vmem_limit_bytes=VMEM_LIMIT)


def _dot(a, b):
    return jnp.dot(a, b, preferred_element_type=F32)


def _dot_nt(a, b):
    return lax.dot_general(a, b, (((1,), (1,)), ((), ())), preferred_element_type=F32)


def _norm_mod(x, g, shift, scale):
    y = x * lax.rsqrt(jnp.mean(x * x, axis=-1, keepdims=True) + EPS)
    return (y * g) * (1.0 + scale) + shift


def _sigmoid(x):
    return 1.0 / (1.0 + jnp.exp(-x))


SLAB = 8


def _store_slabs(ref, val):
    rows = val.shape[0]
    for s in range(SLAB):
        ref[pl.ds(s, rows, stride=SLAB), :] = val[:, s * LANES:(s + 1) * LANES]


def _load_slab_cols(ref, s, rows):
    return ref[pl.ds(s, rows, stride=SLAB), :]


BF16_SUBLANES = 16


def _cast_plan(w2d, nsteps, step_of):
    rows, cols = w2d.shape
    share = 1
    while rows % (nsteps // share) or (rows // (nsteps // share)) % BF16_SUBLANES:
        share *= 2
        assert share <= nsteps, (rows, nsteps)
    rb = rows // (nsteps // share)
    imap = lambda *ids: (step_of(*ids) // share, 0)
    return (pl.BlockSpec((rb, cols), imap), pl.BlockSpec((rb, cols), imap),
            jax.ShapeDtypeStruct((rows, cols), BF16))


def _cast_blocks(refs):
    k = len(refs) // 2
    for src, dst in zip(refs[:k], refs[k:]):
        dst[...] = src[...].astype(BF16)


def _ada_kernel(c_ref, w_ref, b_ref, o_ref):
    c = c_ref[...]
    ca = (c * _sigmoid(c)).astype(BF16)
    o_ref[0] = _dot(ca, w_ref[0].astype(BF16)) + b_ref[0]


def _ada(c8, w_ada, b_ada):
    depth, d, d6 = w_ada.shape
    tn = 1536
    return pl.pallas_call(
        _ada_kernel,
        out_shape=jax.ShapeDtypeStruct((depth, 8, d6), F32),
        grid=(depth, d6 // tn),
        in_specs=[
            pl.BlockSpec((8, d), lambda l, j: (0, 0)),
            pl.BlockSpec((1, d, tn), lambda l, j: (l, 0, j)),
            pl.BlockSpec((1, 1, tn), lambda l, j: (l, 0, j)),
        ],
        out_specs=pl.BlockSpec((1, 8, tn), lambda l, j: (l, 0, j)),
        compiler_params=_cparams(("arbitrary", "arbitrary")),
        name="ada_mod",
    )(c8, w_ada, b_ada.reshape(depth, 1, d6))


def _qkv_kernel(x_ref, mod_ref, g_ref, pos_ref, freq_ref, wqt_ref, wk_ref, wvt_ref, *rest, tk, ncast):
    qt_ref, k_ref, vt_ref = rest[ncast:ncast + 3]
    _cast_blocks(rest[:ncast] + rest[ncast + 3:])
    m = mod_ref[0]
    h = _norm_mod(x_ref[...], g_ref[...], m[0:1], m[1:2]).astype(BF16)
    d = x_ref.shape[1]
    tm = x_ref.shape[0]
    ang = freq_ref[...] * pos_ref[0].astype(F32)
    reps = LANES // HALF
    ct = jnp.concatenate([jnp.cos(ang)] * reps, axis=0)
    st = jnp.concatenate([jnp.sin(ang)] * reps, axis=0)
    c = ct.T
    s = st.T
    for gi in range(d // GROUP):
        y = _dot(h, wk_ref[:, gi * GROUP:(gi + 1) * GROUP])
        a = y[:, :LANES]
        b = y[:, LANES:]
        k_ref[:, gi * GROUP:gi * GROUP + LANES] = (a * c - b * s).astype(BF16)
        k_ref[:, gi * GROUP + LANES:(gi + 1) * GROUP] = (b * c + a * s).astype(BF16)
    mult = HEAD_DIM ** -0.5 * math.log2(math.e)
    for gi in range(d // GROUP):
        y = _dot_nt(wqt_ref[gi * GROUP:(gi + 1) * GROUP, :], h)
        a = y[:LANES]
        b = y[LANES:]
        qt_ref[0, gi * GROUP:gi * GROUP + LANES, :] = ((a * ct - b * st) * mult).astype(BF16)
        qt_ref[0, gi * GROUP + LANES:(gi + 1) * GROUP, :] = ((b * ct + a * st) * mult).astype(BF16)
    ones = jnp.ones((V_ROWS - V_DIM, tk), BF16)
    for gi in range(d // GROUP):
        vt = _dot_nt(wvt_ref[gi * GROUP:(gi + 1) * GROUP, :], h)
        for hh in range(GROUP // V_DIM):
            for ci in range(tm // tk):
                vt_ref[0, ci, 2 * gi + hh, 0:V_DIM, :] = (
                    vt[hh * V_DIM:(hh + 1) * V_DIM, ci * tk:(ci + 1) * tk].astype(BF16))
                vt_ref[0, ci, 2 * gi + hh, V_DIM:V_ROWS, :] = ones


def _qkv(x2, mod, g, pos_row, freq_col, wqt, wk, wvt, bsz, tm, tk, cast_weights):
    n, d = x2.shape
    t = n // bsz
    tiles_per_b = t // tm
    nh = d // V_DIM
    row = lambda i: (i, 0)
    const = lambda i: (0, 0)
    plans = [_cast_plan(w, n // tm, lambda i: i) for w in cast_weights]
    outs = pl.pallas_call(
        functools.partial(_qkv_kernel, tk=tk, ncast=len(plans)),
        out_shape=(jax.ShapeDtypeStruct((bsz, d, t), BF16), jax.ShapeDtypeStruct((n, d), BF16),
                   jax.ShapeDtypeStruct((bsz, t // tk, nh, V_ROWS, tk), BF16)) + tuple(p[2] for p in plans),
        grid=(n // tm,),
        in_specs=[
            pl.BlockSpec((tm, d), row),
            pl.BlockSpec((1, 8, d), lambda i: (i // tiles_per_b, 0, 0)),
            pl.BlockSpec((1, d), const),
            pl.BlockSpec((1, 1, tm), lambda i: (i, 0, 0)),
            pl.BlockSpec((HALF, 1), const),
            pl.BlockSpec((d, d), const),
            pl.BlockSpec((d, d), const),
            pl.BlockSpec((d, d), const),
        ] + [p[0] for p in plans],
        out_specs=(pl.BlockSpec((1, d, tm), lambda i: (i // tiles_per_b, 0, i % tiles_per_b)),
                   pl.BlockSpec((tm, d), row),
                   pl.BlockSpec((1, tm // tk, nh, V_ROWS, tk),
                                lambda i: (i // tiles_per_b, i % tiles_per_b, 0, 0, 0)))
        + tuple(p[1] for p in plans),
        compiler_params=_cparams(("arbitrary",)),
        name="qkv_rope",
    )(x2, mod, g, pos_row, freq_col, wqt, wk, wvt, *cast_weights)
    return outs[0], outs[1], outs[2], outs[3:]


def _attn_kernel(qt_ref, k_ref, vt_ref, lam_ref, g_ref, *rest, tk, lambda_init, ncast):
    o_ref = rest[ncast]
    s_a, s_b, mc_a, mc_b, acc = rest[2 * ncast + 1:]
    _cast_blocks(rest[:ncast] + rest[ncast + 1:2 * ncast + 1])
    tq = qt_ref.shape[2]
    nh = acc.shape[0]
    qi = pl.program_id(2)
    n_full = qi * (tq // tk)
    row = lax.broadcasted_iota(jnp.int32, (GROUP, 1), 0)
    head = (row % LANES) // HALF
    qf = qt_ref[0].astype(F32)
    qm = [jnp.where(head == hd, qf, 0.0).astype(BF16) for hd in range(nh)]

    def scores(kj, s_ref, mc_ref, masked, lo=0):
        kc = k_ref[pl.ds(pl.multiple_of(kj * tk, tk), tk), :]
        for hd in range(nh):
            s = _dot(kc, qm[hd][:, lo:])
            if masked:
                kidx = kj * tk + lax.broadcasted_iota(jnp.int32, s.shape, 0)
                qidx = qi * tq + lo + lax.broadcasted_iota(jnp.int32, s.shape, 1)
                s = jnp.where(kidx <= qidx, s, NEG)
            s_ref[hd, :, lo:] = s
            mc_ref[hd, :, lo:] = jnp.max(s, axis=0, keepdims=True)

    def softmax_pv(kj, s_ref, mc_ref, ms, first, lo=0):
        out = []
        for hd in range(nh):
            vc = vt_ref[0, kj, hd // 2]
            s = s_ref[hd, :, lo:]
            mc = mc_ref[hd, :, lo:]
            m_new = mc if first else jnp.maximum(ms[hd][:, lo:], mc)
            pv = _dot(vc, jnp.exp2(s - m_new).astype(BF16))
            if first:
                acc[hd] = pv
            else:
                acc[hd, :, lo:] = jnp.exp2(ms[hd][:, lo:] - m_new) * acc[hd, :, lo:] + pv
            out.append(m_new if lo == 0 else jnp.concatenate([ms[hd][:, :lo], m_new], axis=1))
        return tuple(out)

    assert tq == 2 * tk
    scores(n_full, s_a, mc_a, True)
    scores(n_full + 1, s_b, mc_b, True, lo=tk)
    ms = softmax_pv(n_full, s_a, mc_a, None, True)
    scores(0, s_a, mc_a, False)
    ms = softmax_pv(n_full + 1, s_b, mc_b, ms, False, lo=tk)

    def body(jj, ms):
        j = 2 * jj
        scores(j + 1, s_b, mc_b, False)
        ms = softmax_pv(j, s_a, mc_a, ms, False)
        scores(j + 2, s_a, mc_a, False)
        return softmax_pv(j + 1, s_b, mc_b, ms, False)

    npairs = n_full // 2 - 1
    ms = lax.fori_loop(0, npairs // 2, lambda t, ms: body(2 * t + 1, body(2 * t, ms)), ms)
    ms = lax.cond(jnp.logical_and(npairs > 0, npairs % 2 == 1),
                  lambda ms: body(npairs - 1, ms), lambda ms: ms, ms)

    @pl.when(n_full > 0)
    def _():
        scores(n_full - 1, s_b, mc_b, False)
        ms2 = softmax_pv(n_full - 2, s_a, mc_a, ms, False)
        softmax_pv(n_full - 1, s_b, mc_b, ms2, False)

    lp = lam_ref[...]
    lam = (jnp.exp(jnp.sum(lp[0:1] * lp[1:2], axis=-1, keepdims=True))
           - jnp.exp(jnp.sum(lp[2:3] * lp[3:4], axis=-1, keepdims=True)) + lambda_init)
    for pr in range(nh // 2):
        a1 = acc[2 * pr]
        a2 = acc[2 * pr + 1]
        o = a1[:V_DIM] / a1[V_DIM:V_DIM + 1] - lam * (a2[:V_DIM] / a2[V_DIM:V_DIM + 1])
        o = o * lax.rsqrt(jnp.mean(o * o, axis=0, keepdims=True) + EPS)
        o = (o * g_ref[...]) * (1.0 - lambda_init)
        o_ref[:, pr * V_DIM:(pr + 1) * V_DIM] = o.T.astype(BF16)


def _attention(qt, k, vt, lam_rows, g_col, tq, tk, lambda_init, cast_weights):
    bsz, d, t = qt.shape
    n = bsz * t
    nq = t // tq
    nh = GROUP // HEAD_DIM
    ng = d // GROUP
    plans = [_cast_plan(w, bsz * ng * nq, lambda b, g, i: (b * ng + g) * nq + i) for w in cast_weights]
    outs = pl.pallas_call(
        functools.partial(_attn_kernel, tk=tk, lambda_init=lambda_init, ncast=len(plans)),
        out_shape=(jax.ShapeDtypeStruct((n, d), BF16),) + tuple(pln[2] for pln in plans),
        grid=(bsz, ng, nq),
        in_specs=[
            pl.BlockSpec((1, GROUP, tq), lambda b, g, i: (b, g, i)),
            pl.BlockSpec((t, GROUP), lambda b, g, i: (b, g)),
            pl.BlockSpec((1, t // tk, nh // 2, V_ROWS, tk), lambda b, g, i: (b, 0, g, 0, 0)),
            pl.BlockSpec((4, HEAD_DIM), lambda b, g, i: (0, 0)),
            pl.BlockSpec((V_DIM, 1), lambda b, g, i: (0, 0)),
        ] + [pln[0] for pln in plans],
        out_specs=(pl.BlockSpec((tq, GROUP), lambda b, g, i: (b * nq + i, g)),) + tuple(pln[1] for pln in plans),
        scratch_shapes=[pltpu.VMEM((nh, tk, tq), F32), pltpu.VMEM((nh, tk, tq), F32),
                        pltpu.VMEM((nh, 1, tq), F32), pltpu.VMEM((nh, 1, tq), F32),
                        pltpu.VMEM((nh, V_ROWS, tq), F32)],
        compiler_params=_cparams(("arbitrary", "arbitrary", "arbitrary")),
        name="diff_attn",
    )(qt, k, vt, lam_rows, g_col, *cast_weights)
    return outs[0], outs[1:]


def _ffn_kernel(a_ref, wo_ref, x_ref, mod_ref, g_ref, wg_ref, wu_ref, wd_ref, o_ref, *, tf):
    m = mod_ref[0]
    x = x_ref[...] + m[2:3] * _dot(a_ref[...], wo_ref[...])
    h = _norm_mod(x, g_ref[...], m[3:4], m[4:5]).astype(BF16)
    f = wg_ref.shape[1]
    acc = None
    for ci in range(f // tf):
        sl = slice(ci * tf, (ci + 1) * tf)
        gt = _dot(h, wg_ref[:, sl])
        up = _dot(h, wu_ref[:, sl])
        a = (gt * _sigmoid(gt) * up).astype(BF16)
        part = _dot(a, wd_ref[sl, :])
        acc = part if acc is None else acc + part
    o_ref[...] = x + m[5:6] * acc


def _ffn(a, wo, x2, mod, g, wg, wu, wd, bsz, tm, tf):
    n, d = x2.shape
    f = wg.shape[1]
    tiles_per_b = n // bsz // tm
    row = lambda i: (i, 0)
    const = lambda i: (0, 0)
    return pl.pallas_call(
        functools.partial(_ffn_kernel, tf=tf),
        out_shape=jax.ShapeDtypeStruct((n, d), F32),
        grid=(n // tm,),
        in_specs=[
            pl.BlockSpec((tm, d), row),
            pl.BlockSpec((d, d), const),
            pl.BlockSpec((tm, d), row),
            pl.BlockSpec((1, 8, d), lambda i: (i // tiles_per_b, 0, 0)),
            pl.BlockSpec((1, d), const),
            pl.BlockSpec((d, f), const),
            pl.BlockSpec((d, f), const),
            pl.BlockSpec((f, d), const),
        ],
        out_specs=pl.BlockSpec((tm, d), row),
        compiler_params=_cparams(("arbitrary",)),
        name="oproj_dense_swiglu",
    )(a, wo, x2, mod, g, wg, wu, wd)


def _route_top2(wr_t, h, idx_ref, gate_ref):
    logits = lax.dot_general(wr_t, h, (((1,), (1,)), ((), ())),
                             preferred_element_type=F32, precision=lax.Precision.HIGHEST)
    ne = logits.shape[0]
    eidx = lax.broadcasted_iota(jnp.int32, logits.shape, 0)
    v1 = jnp.max(logits, axis=0, keepdims=True)
    i1 = jnp.min(jnp.where(logits == v1, eidx, ne), axis=0, keepdims=True)
    rest = jnp.where(eidx == i1, -jnp.inf, logits)
    v2 = jnp.max(rest, axis=0, keepdims=True)
    i2 = jnp.min(jnp.where(rest == v2, eidx, ne), axis=0, keepdims=True)
    e2 = jnp.exp(v2 - v1)
    den = 1.0 + e2
    idx_ref[0:1, :] = i1
    idx_ref[1:2, :] = i2
    gate_ref[0:1, :] = 1.0 / den
    gate_ref[1:2, :] = e2 / den


def _conformer_kernel(x_ref, mod_ref, gm_ref, w1_ref, b1_ref, wdw_ref, bdw_ref, lg_ref, lb_ref, w2_ref, b2_ref,
                      gf_ref, wr_ref, x_out_ref, h_ref, idx_ref, gate_ref, ext, vbuf, *, tiles_per_b, rb):
    tm, d = x_ref.shape
    nchunks = d // LANES
    first = (pl.program_id(0) % tiles_per_b) == 0

    @pl.when(first)
    def _():
        ext[:, 0:CONV_HALO, :] = jnp.zeros((nchunks, CONV_HALO, LANES), F32)

    @pl.when(jnp.logical_not(first))
    def _():
        ext[:, 0:CONV_HALO, :] = ext[:, tm:tm + CONV_HALO, :]

    m = mod_ref[0]
    x = x_ref[...]
    h = _norm_mod(x, gm_ref[...], m[0:1], m[1:2]).astype(BF16)
    off = CONV_HALO - (CONV_WIDTH - 1)
    tn = 2 * LANES
    for ci in range(d // tn):
        a = _dot(h, w1_ref[:, ci * tn:(ci + 1) * tn]) + b1_ref[:, ci * tn:(ci + 1) * tn]
        gt = _dot(h, w1_ref[:, d + ci * tn:d + (ci + 1) * tn]) + b1_ref[:, d + ci * tn:d + (ci + 1) * tn]
        u = a * _sigmoid(gt)
        for half in range(tn // LANES):
            c = ci * (tn // LANES) + half
            cols = slice(c * LANES, (c + 1) * LANES)
            ext[c, CONV_HALO:, :] = u[:, half * LANES:(half + 1) * LANES]
            for r in range(tm // rb):
                acc = jnp.broadcast_to(bdw_ref[:, cols], (rb, LANES))
                for kk in range(CONV_WIDTH):
                    acc = acc + wdw_ref[kk:kk + 1, cols] * ext[c, r * rb + off + kk:r * rb + off + kk + rb, :]
                vbuf[r * rb:(r + 1) * rb, cols] = acc
    v = vbuf[...]
    mu = jnp.mean(v, axis=-1, keepdims=True)
    vc = v - mu
    var = jnp.mean(vc * vc, axis=-1, keepdims=True)
    y = vc * lax.rsqrt(var + EPS) * lg_ref[...] + lb_ref[...]
    y = (y * _sigmoid(y)).astype(BF16)
    x = x + m[2:3] * (_dot(y, w2_ref[...]) + b2_ref[...])
    x_out_ref[...] = x
    hf = _norm_mod(x, gf_ref[...], m[3:4], m[4:5])
    _store_slabs(h_ref, hf)
    _route_top2(wr_ref[...], hf, idx_ref, gate_ref)


def _conformer(x2, mod, gm, w1, b1, wdw, bdw, ln_g, ln_b, w2, b2, gf, wr_t, bsz, tm):
    n, d = x2.shape
    ne = wr_t.shape[0]
    tiles_per_b = n // bsz // tm
    row = lambda i: (i, 0)
    const = lambda i: (0, 0)
    vec = pl.BlockSpec((1, d), const)
    return pl.pallas_call(
        functools.partial(_conformer_kernel, tiles_per_b=tiles_per_b, rb=64),
        out_shape=(jax.ShapeDtypeStruct((n, d), F32), jax.ShapeDtypeStruct((n * SLAB, LANES), F32),
                   jax.ShapeDtypeStruct((TOP_K, n), jnp.int32), jax.ShapeDtypeStruct((TOP_K, n), F32)),
        grid=(n // tm,),
        in_specs=[
            pl.BlockSpec((tm, d), row),
            pl.BlockSpec((1, 8, d), lambda i: (i // tiles_per_b, 0, 0)),
            vec,
            pl.BlockSpec((d, 2 * d), const),
            pl.BlockSpec((1, 2 * d), const),
            pl.BlockSpec((CONV_WIDTH, d), const),
            vec, vec, vec,
            pl.BlockSpec((d, d), const),
            vec, vec,
            pl.BlockSpec((ne, d), const),
        ],
        out_specs=(pl.BlockSpec((tm, d), row), pl.BlockSpec((tm * SLAB, LANES), row),
                   pl.BlockSpec((TOP_K, tm), lambda i: (0, i)), pl.BlockSpec((TOP_K, tm), lambda i: (0, i))),
        scratch_shapes=[pltpu.VMEM((d // LANES, tm + CONV_HALO, LANES), F32), pltpu.VMEM((tm, d), F32)],
        compiler_params=_cparams(("arbitrary",)),
        name="conformer_router",
    )(x2, mod, gm, w1, b1, wdw, bdw, ln_g, ln_b, w2, b2, gf, wr_t)


def _moe_kernel(be_ref, na_ref, cnt_ref, src_ref, dst_ref, h_hbm, wg_ref, wu_ref, wd_ref, y_hbm,
                xbuf, xb16, acc, ybuf, gsem, ssem, *, n_assign, n_spare_blocks):
    tm = xb16.shape[0]
    nparts, prows = acc.shape[0:2]
    nblk = pl.num_programs(0)
    rpb = tm // LANES
    blk = pl.program_id(0)
    f = pl.program_id(1)
    na = na_ref[0]
    active = blk < na
    slot = blk % 2
    other = 1 - slot

    def slab(row):
        start = row * SLAB
        return pl.ds(start if isinstance(start, int) else pl.multiple_of(start, SLAB), SLAB)

    def gather_row(trow, col, buf, part, r):
        return pltpu.make_async_copy(h_hbm.at[slab(src_ref[trow, col])], xbuf.at[buf, part, slab(r)],
                                     gsem.at[buf])

    def scatter_row(trow, col, buf, part, r):
        return pltpu.make_async_copy(ybuf.at[buf, part, slab(r)], y_hbm.at[slab(dst_ref[trow, col])],
                                     ssem.at[buf])

    def wait_gather(buf):
        pltpu.make_async_copy(xbuf.at[buf], xbuf.at[buf], gsem.at[buf]).wait()

    def wait_scatter(buf):
        pltpu.make_async_copy(ybuf.at[buf], ybuf.at[buf], ssem.at[buf]).wait()

    def scatter_block(b, buf):
        def body(i, carry):
            scatter_row(b * rpb + i // LANES, i % LANES, buf, i // prows, i % prows).start()
            return carry
        lax.fori_loop(0, tm, body, 0)
        wait_scatter(buf)

    @pl.when(jnp.logical_and(blk == 0, f == 0))
    def _():
        def body(i, carry):
            gather_row(i // LANES, i % LANES, 0, i // prows, i % prows).start()
            return carry
        lax.fori_loop(0, tm, body, 0)
        ybuf[1] = jnp.zeros(ybuf.shape[1:], F32)
        spare_fill = [pltpu.make_async_copy(
            ybuf.at[1, 0], y_hbm.at[pl.ds((n_assign + j * prows) * SLAB, prows * SLAB)], ssem.at[1])
            for j in range(n_spare_blocks * nparts)]
        for cp in spare_fill:
            cp.start()
        for cp in spare_fill:
            cp.wait()

    @pl.when(jnp.logical_and(active, f == 0))
    def _():
        wait_gather(slot)
        for part in range(nparts):
            for s in range(SLAB):
                xb16[part * prows:(part + 1) * prows, s * LANES:(s + 1) * LANES] = (
                    _load_slab_cols(xbuf.at[slot, part], s, prows).astype(BF16))
        acc[...] = jnp.zeros(acc.shape, F32)

    def ffn_rows(part):
        xb = xb16[part * prows:(part + 1) * prows]
        gt = _dot(xb, wg_ref[0])
        up = _dot(xb, wu_ref[0])
        acc[part] += _dot((gt * _sigmoid(gt) * up).astype(BF16), wd_ref[0])

    @pl.when(active)
    def _():
        qrows = prows // LANES
        gbase = (blk + 1) * rpb + f * qrows
        sbase = jnp.where(blk == 0, nblk, blk - 1) * rpb + f * qrows
        for i in range(prows):
            gather_row(gbase + i // LANES, i % LANES, other, f, i).start()
            scatter_row(sbase + i // LANES, i % LANES, other, f, i).start()
        ffn_rows(0)

    assert nparts == 2
    @pl.when(jnp.logical_and(active, cnt_ref[blk] > prows))
    def _():
        ffn_rows(1)

    @pl.when(jnp.logical_and(active, f == nparts - 1))
    def _():
        @pl.when(blk >= 1)
        def _():
            wait_scatter(slot)
        for part in range(nparts):
            _store_slabs(ybuf.at[slot, part], acc[part])

    @pl.when(jnp.logical_and(f == 0, blk == na))
    def _():
        wait_gather(slot)
        wait_scatter(slot)
        scatter_block(blk - 1, other)

    @pl.when(jnp.logical_and(jnp.logical_and(f == nparts - 1, blk == nblk - 1), active))
    def _():
        wait_gather(other)
        wait_scatter(other)
        scatter_block(blk, slot)


def _moe(block_e, n_active, block_cnt, src_tab, dst_tab, h, wg, wu, wd, n_out, tm, tf):
    n = h.shape[0] // SLAB
    ne, d, fe = wg.shape
    assert d == SLAB * LANES
    nf = fe // tf
    nblk = dst_tab.shape[0] * LANES // tm - 1
    assert tm % (nf * LANES) == 0

    def wsel(blk, f, na):
        return jnp.where(blk < na[0], f, nf - 1)

    grid_spec = pltpu.PrefetchScalarGridSpec(
        num_scalar_prefetch=5,
        grid=(nblk, nf),
        in_specs=[
            pl.BlockSpec(memory_space=pl.ANY),
            pl.BlockSpec((1, d, tf), lambda blk, f, be, na, cnt, src, dst: (be[blk], 0, wsel(blk, f, na))),
            pl.BlockSpec((1, d, tf), lambda blk, f, be, na, cnt, src, dst: (be[blk], 0, wsel(blk, f, na))),
            pl.BlockSpec((1, tf, d), lambda blk, f, be, na, cnt, src, dst: (be[blk], wsel(blk, f, na), 0)),
        ],
        out_specs=pl.BlockSpec(memory_space=pl.ANY),
        scratch_shapes=[pltpu.VMEM((2, nf, tm // nf * SLAB, LANES), F32), pltpu.VMEM((tm, d), BF16),
                        pltpu.VMEM((nf, tm // nf, d), F32), pltpu.VMEM((2, nf, tm // nf * SLAB, LANES), F32),
                        pltpu.SemaphoreType.DMA((2,)), pltpu.SemaphoreType.DMA((2,))],
    )
    return pl.pallas_call(
        functools.partial(_moe_kernel, n_assign=n * TOP_K, n_spare_blocks=ne),
        out_shape=jax.ShapeDtypeStruct((n_out * SLAB, LANES), F32),
        grid_spec=grid_spec,
        compiler_params=_cparams(("arbitrary", "arbitrary")),
        name="moe_experts",
    )(block_e, n_active, block_cnt, src_tab, dst_tab, h, wg, wu, wd)


def _combine_kernel(y0_ref, y1_ref, x_ref, gates_ref, mod_ref, g_ref, o_ref):
    tm, d = x_ref.shape
    gate = mod_ref[0][5:6]
    gt = gates_ref[...]
    g0 = gt[:, 0:1]
    g1 = gt[:, 1:2]
    ssq = jnp.zeros((tm, 1), F32)
    for s in range(SLAB):
        cols = slice(s * LANES, (s + 1) * LANES)
        xs = x_ref[:, cols] + gate[:, cols] * (g0 * _load_slab_cols(y0_ref, s, tm)
                                               + g1 * _load_slab_cols(y1_ref, s, tm))
        o_ref[:, cols] = xs
        ssq = ssq + jnp.sum(xs * xs, axis=-1, keepdims=True)
    o_ref[...] = o_ref[...] * lax.rsqrt(ssq * (1.0 / d) + EPS) * g_ref[...]


def _combine(y, x2, gates_t, mod, final_g, bsz, tm):
    n, d = x2.shape
    tiles_per_b = n // bsz // tm
    nt = n // tm
    row = lambda i: (i, 0)
    return pl.pallas_call(
        _combine_kernel,
        out_shape=jax.ShapeDtypeStruct((n, d), F32),
        grid=(nt,),
        in_specs=[
            pl.BlockSpec((tm * SLAB, LANES), row),
            pl.BlockSpec((tm * SLAB, LANES), lambda i: (nt + i, 0)),
            pl.BlockSpec((tm, d), row),
            pl.BlockSpec((tm, TOP_K), row),
            pl.BlockSpec((1, 8, d), lambda i: (i // tiles_per_b, 0, 0)),
            pl.BlockSpec((1, d), lambda i: (0, 0)),
        ],
        out_specs=pl.BlockSpec((tm, d), row),
        compiler_params=_cparams(("arbitrary",)),
        name="moe_combine_final_norm",
    )(y, y, x2, gates_t, mod, final_g)


def _routing(idx, ne, tm):
    n = idx.shape[1]
    a_tot = n * TOP_K
    flat_e = idx.T.reshape(-1)
    sizes = jnp.sum((flat_e[:, None] == jnp.arange(ne, dtype=jnp.int32)[None, :]).astype(jnp.int32), axis=0)
    padded = (sizes + tm - 1) // tm * tm
    pad_end = jnp.cumsum(padded)
    pad_start = pad_end - padded
    grp_start = jnp.cumsum(sizes) - sizes
    p = a_tot + ne * tm
    nblk = p // tm
    n_active = (pad_end[-1] // tm).astype(jnp.int32)
    blk_start = jnp.arange(nblk, dtype=jnp.int32) * tm
    block_e = jnp.sum((blk_start[:, None] >= pad_end[None, :]).astype(jnp.int32), axis=1)
    last_e = jnp.sum((jnp.maximum(pad_end[-1] - 1, 0) >= pad_end).astype(jnp.int32))
    block_e = jnp.where(jnp.arange(nblk) < n_active, block_e, last_e).astype(jnp.int32)
    order = jnp.argsort(flat_e).astype(jnp.int32)
    within0 = blk_start - pad_start[block_e]
    size_b = sizes[block_e]
    grp_b = grp_start[block_e]
    lane = jnp.arange(tm, dtype=jnp.int32)[None, :]
    within = within0[:, None] + lane
    valid = within < size_b[:, None]
    seg = order[jnp.clip(grp_b[:, None] + within, 0, a_tot - 1)]
    src_tok = jnp.where(valid, seg // TOP_K, 0).astype(jnp.int32)
    spare = a_tot + blk_start[:, None] + lane - grp_b[:, None] - jnp.minimum(within, size_b[:, None])
    dst_row = jnp.where(valid, (seg % TOP_K) * n + seg // TOP_K, spare).astype(jnp.int32)
    src_tab = jnp.pad(src_tok.reshape(-1), (0, tm)).reshape(-1, LANES)
    dst_tab = jnp.concatenate([dst_row.reshape(-1), p + jnp.arange(tm, dtype=jnp.int32)]).reshape(-1, LANES)
    block_cnt = jnp.clip(size_b - within0, 0, tm).astype(jnp.int32)
    return block_e, n_active.reshape(1), block_cnt, src_tab, dst_tab, p + tm


def _rope_layout(w):
    rows, d = w.shape
    heads_per_group = GROUP // HEAD_DIM
    w5 = w.reshape(rows, d // GROUP, heads_per_group, 2, HALF)
    return w5.transpose(0, 1, 3, 2, 4).reshape(rows, d)


def _mod_rows(mod_l, bsz, d):
    m = mod_l[:bsz].reshape(bsz, 6, d)
    return jnp.pad(m, ((0, 0), (0, 2), (0, 0)))


@jax.jit
def kernel(x, c, positions, w_ada, b_ada, norm_mix_g, norm_ffn_g, attn_w_qkv, attn_w_o, lam_q1, lam_k1,
           lam_q2, lam_k2, attn_subln_g, conv_w_pw1, conv_b_pw1, conv_w_dw, conv_b_dw, conv_ln_g,
           conv_ln_b, conv_w_pw2, conv_b_pw2, ffn_w_gate, ffn_w_up, ffn_w_down, moe_w_router,
           moe_w_gate, moe_w_up, moe_w_down, final_g):
    bsz, t, d = x.shape
    n = bsz * t
    depth = w_ada.shape[0]
    assert depth == 2 and d % GROUP == 0 and bsz <= 8
    tm = min(512, t)
    tk = min(256, t)
    x2 = x.reshape(n, d)

    c8 = jnp.pad(c, ((0, 8 - bsz), (0, 0)))
    mod = _ada(c8, w_ada, b_ada)
    mod0 = _mod_rows(mod[0], bsz, d)
    mod1 = _mod_rows(mod[1], bsz, d)

    inv_freq = ROPE_THETA ** (-jnp.arange(HALF, dtype=F32) / HALF)
    w_qkv = attn_w_qkv[0]
    wqt = _rope_layout(w_qkv[:, :d]).T.astype(BF16)
    wk = _rope_layout(w_qkv[:, d:2 * d]).astype(BF16)
    wvt = w_qkv[:, 2 * d:].T.astype(BF16)
    qt, k, vt, (ffn_wg, ffn_wu, ffn_wd, w_o, w_pw1, w_pw2) = _qkv(
        x2, mod0, norm_mix_g[0:1], positions.reshape(n // tm, 1, tm), inv_freq.reshape(HALF, 1),
        wqt, wk, wvt, bsz, tm, tk,
        [ffn_w_gate[0], ffn_w_up[0], ffn_w_down[0], attn_w_o[0], conv_w_pw1[0], conv_w_pw2[0]])
    lam_rows = jnp.concatenate([lam_q1[0:1], lam_k1[0:1], lam_q2[0:1], lam_k2[0:1]], axis=0)
    lambda_init = 0.8 - 0.6 * math.exp(-0.3 * 0)
    ne, _, fe = moe_w_gate.shape[1:]
    o, (moe_wg, moe_wu, moe_wd) = _attention(
        qt, k, vt, lam_rows, attn_subln_g[0].reshape(V_DIM, 1), tm, tk, lambda_init,
        [moe_w_gate[0].reshape(ne * d, fe), moe_w_up[0].reshape(ne * d, fe), moe_w_down[0].reshape(ne * fe, d)])
    x2 = _ffn(o, w_o, x2, mod0, norm_ffn_g[0:1], ffn_wg, ffn_wu, ffn_wd, bsz, tm, tf=256)

    x2, h, idx, gates = _conformer(x2, mod1, norm_mix_g[1:2], w_pw1, conv_b_pw1[0:1], conv_w_dw[0],
                                   conv_b_dw[0:1], conv_ln_g[0:1], conv_ln_b[0:1], w_pw2, conv_b_pw2[0:1],
                                   norm_ffn_g[1:2], moe_w_router[0].T, bsz, tm)
    tm_e = 2 * tm
    block_e, n_active, block_cnt, src_tab, dst_tab, n_out = _routing(idx, ne, tm_e)
    tf_e = 1792 if fe % 1792 == 0 else fe
    y = _moe(block_e, n_active, block_cnt, src_tab, dst_tab, h, moe_wg.reshape(ne, d, fe),
             moe_wu.reshape(ne, d, fe), moe_wd.reshape(ne, fe, d), n_out, tm_e, tf_e)
    out = _combine(y, x2, gates.T, mod1, final_g.reshape(1, d), bsz, tm)
    return out.reshape(bsz, t, d)
```

```python
import functools
import math

import jax
import jax.numpy as jnp
from jax import lax
from jax.experimental import pallas as pl
from jax.experimental.pallas import tpu as pltpu

EPS = 1e-6
ROPE_THETA = 10000.0
HEAD_DIM = 64
V_DIM = 128
V_ROWS = V_DIM + 16
HALF = HEAD_DIM // 2
LANES = 128
GROUP = 4 * HEAD_DIM
CONV_WIDTH = 31
CONV_HALO = 32
TOP_K = 2
NEG = -1e30
VMEM_LIMIT = 56 * 1024 * 1024

ROW_TILE = 512
KEY_CHUNK = 256
FFN_CHUNK = 256
EXPERT_BLOCK_TILES = 2
ADA_COL_TILE = 1536
CONV_ROW_CHUNK = 64
MOD_ROWS = 8

BF16 = jnp.bfloat16
F32 = jnp.float32


def _cparams(sem):
    return pltpu.CompilerParams(dimension_semantics=sem, vmem_limit_bytes=VMEM_LIMIT)


def _dot(a, b):
    return jnp.dot(a, b, preferred_element_type=F32)


def _dot_nt(a, b):
    return lax.dot_general(a, b, (((1,), (1,)), ((), ())), preferred_element_type=F32)


def _norm_mod(x, g, shift, scale):
    y = x * lax.rsqrt(jnp.mean(x * x, axis=-1, keepdims=True) + EPS)
    return (y * g) * (1.0 + scale) + shift


def _sigmoid(x):
    return 1.0 / (1.0 + jnp.exp(-x))


SLAB = 8


def _store_slabs(ref, val):
    rows = val.shape[0]
    for s in range(SLAB):
        ref[pl.ds(s, rows, stride=SLAB), :] = val[:, s * LANES:(s + 1) * LANES]


def _load_slab_cols(ref, s, rows):
    return ref[pl.ds(s, rows, stride=SLAB), :]


BF16_SUBLANES = 16


def _cast_plan(w2d, nsteps, step_of):
    rows, cols = w2d.shape
    share = 1
    while rows % (nsteps // share) or (rows // (nsteps // share)) % BF16_SUBLANES:
        share *= 2
        assert share <= nsteps, (rows, nsteps)
    rb = rows // (nsteps // share)
    imap = lambda *ids: (step_of(*ids) // share, 0)
    return (pl.BlockSpec((rb, cols), imap), pl.BlockSpec((rb, cols), imap),
            jax.ShapeDtypeStruct((rows, cols), BF16))


def _cast_blocks(refs):
    k = len(refs) // 2
    for src, dst in zip(refs[:k], refs[k:]):
        dst[...] = src[...].astype(BF16)


def _ada_kernel(c_ref, w_ref, b_ref, o_ref):
    c = c_ref[...]
    ca = (c * _sigmoid(c)).astype(BF16)
    o_ref[0] = _dot(ca, w_ref[0].astype(BF16)) + b_ref[0]


def _ada(c8, w_ada, b_ada):
    depth, d, d6 = w_ada.shape
    tn = ADA_COL_TILE
    return pl.pallas_call(
        _ada_kernel,
        out_shape=jax.ShapeDtypeStruct((depth, MOD_ROWS, d6), F32),
        grid=(depth, d6 // tn),
        in_specs=[
            pl.BlockSpec((MOD_ROWS, d), lambda l, j: (0, 0)),
            pl.BlockSpec((1, d, tn), lambda l, j: (l, 0, j)),
            pl.BlockSpec((1, 1, tn), lambda l, j: (l, 0, j)),
        ],
        out_specs=pl.BlockSpec((1, MOD_ROWS, tn), lambda l, j: (l, 0, j)),
        compiler_params=_cparams(("arbitrary", "arbitrary")),
        name="ada_mod",
    )(c8, w_ada, b_ada.reshape(depth, 1, d6))


def _qkv_kernel(x_ref, mod_ref, g_ref, pos_ref, freq_ref, wqt_ref, wk_ref, wvt_ref, *rest, tk, ncast):
    qt_ref, k_ref, vt_ref = rest[ncast:ncast + 3]
    _cast_blocks(rest[:ncast] + rest[ncast + 3:])
    m = mod_ref[0]
    h = _norm_mod(x_ref[...], g_ref[...], m[0:1], m[1:2]).astype(BF16)
    d = x_ref.shape[1]
    tm = x_ref.shape[0]
    ang = freq_ref[...] * pos_ref[0].astype(F32)
    reps = LANES // HALF
    ct = jnp.concatenate([jnp.cos(ang)] * reps, axis=0)
    st = jnp.concatenate([jnp.sin(ang)] * reps, axis=0)
    c = ct.T
    s = st.T
    for gi in range(d // GROUP):
        y = _dot(h, wk_ref[:, gi * GROUP:(gi + 1) * GROUP])
        a = y[:, :LANES]
        b = y[:, LANES:]
        k_ref[:, gi * GROUP:gi * GROUP + LANES] = (a * c - b * s).astype(BF16)
        k_ref[:, gi * GROUP + LANES:(gi + 1) * GROUP] = (b * c + a * s).astype(BF16)
    mult = HEAD_DIM ** -0.5 * math.log2(math.e)
    for gi in range(d // GROUP):
        y = _dot_nt(wqt_ref[gi * GROUP:(gi + 1) * GROUP, :], h)
        a = y[:LANES]
        b = y[LANES:]
        qt_ref[0, gi * GROUP:gi * GROUP + LANES, :] = ((a * ct - b * st) * mult).astype(BF16)
        qt_ref[0, gi * GROUP + LANES:(gi + 1) * GROUP, :] = ((b * ct + a * st) * mult).astype(BF16)
    ones = jnp.ones((V_ROWS - V_DIM, tk), BF16)
    for gi in range(d // GROUP):
        vt = _dot_nt(wvt_ref[gi * GROUP:(gi + 1) * GROUP, :], h)
        for hh in range(GROUP // V_DIM):
            for ci in range(tm // tk):
                vt_ref[0, ci, 2 * gi + hh, 0:V_DIM, :] = (
                    vt[hh * V_DIM:(hh + 1) * V_DIM, ci * tk:(ci + 1) * tk].astype(BF16))
                vt_ref[0, ci, 2 * gi + hh, V_DIM:V_ROWS, :] = ones


def _qkv(x2, mod, g, pos_row, freq_col, wqt, wk, wvt, bsz, tm, tk, cast_weights):
    n, d = x2.shape
    t = n // bsz
    tiles_per_b = t // tm
    nh = d // V_DIM
    row = lambda i: (i, 0)
    const = lambda i: (0, 0)
    plans = [_cast_plan(w, n // tm, lambda i: i) for w in cast_weights]
    outs = pl.pallas_call(
        functools.partial(_qkv_kernel, tk=tk, ncast=len(plans)),
        out_shape=(jax.ShapeDtypeStruct((bsz, d, t), BF16), jax.ShapeDtypeStruct((n, d), BF16),
                   jax.ShapeDtypeStruct((bsz, t // tk, nh, V_ROWS, tk), BF16)) + tuple(p[2] for p in plans),
        grid=(n // tm,),
        in_specs=[
            pl.BlockSpec((tm, d), row),
            pl.BlockSpec((1, MOD_ROWS, d), lambda i: (i // tiles_per_b, 0, 0)),
            pl.BlockSpec((1, d), const),
            pl.BlockSpec((1, 1, tm), lambda i: (i, 0, 0)),
            pl.BlockSpec((HALF, 1), const),
            pl.BlockSpec((d, d), const),
            pl.BlockSpec((d, d), const),
            pl.BlockSpec((d, d), const),
        ] + [p[0] for p in plans],
        out_specs=(pl.BlockSpec((1, d, tm), lambda i: (i // tiles_per_b, 0, i % tiles_per_b)),
                   pl.BlockSpec((tm, d), row),
                   pl.BlockSpec((1, tm // tk, nh, V_ROWS, tk),
                                lambda i: (i // tiles_per_b, i % tiles_per_b, 0, 0, 0)))
        + tuple(p[1] for p in plans),
        compiler_params=_cparams(("arbitrary",)),
        name="qkv_rope",
    )(x2, mod, g, pos_row, freq_col, wqt, wk, wvt, *cast_weights)
    return outs[0], outs[1], outs[2], outs[3:]


def _attn_kernel(qt_ref, k_ref, vt_ref, lam_ref, g_ref, *rest, tk, lambda_init, ncast):
    o_ref = rest[ncast]
    s_a, s_b, mc_a, mc_b, acc = rest[2 * ncast + 1:]
    _cast_blocks(rest[:ncast] + rest[ncast + 1:2 * ncast + 1])
    tq = qt_ref.shape[2]
    nh = acc.shape[0]
    qi = pl.program_id(2)
    n_full = qi * (tq // tk)
    row = lax.broadcasted_iota(jnp.int32, (GROUP, 1), 0)
    head = (row % LANES) // HALF
    qf = qt_ref[0].astype(F32)
    qm = [jnp.where(head == hd, qf, 0.0).astype(BF16) for hd in range(nh)]

    def scores(kj, s_ref, mc_ref, masked, lo=0):
        kc = k_ref[pl.ds(pl.multiple_of(kj * tk, tk), tk), :]
        for hd in range(nh):
            s = _dot(kc, qm[hd][:, lo:])
            if masked:
                kidx = kj * tk + lax.broadcasted_iota(jnp.int32, s.shape, 0)
                qidx = qi * tq + lo + lax.broadcasted_iota(jnp.int32, s.shape, 1)
                s = jnp.where(kidx <= qidx, s, NEG)
            s_ref[hd, :, lo:] = s
            mc_ref[hd, :, lo:] = jnp.max(s, axis=0, keepdims=True)

    def softmax_pv(kj, s_ref, mc_ref, ms, first, lo=0):
        out = []
        for hd in range(nh):
            vc = vt_ref[0, kj, hd // 2]
            s = s_ref[hd, :, lo:]
            mc = mc_ref[hd, :, lo:]
            m_new = mc if first else jnp.maximum(ms[hd][:, lo:], mc)
            pv = _dot(vc, jnp.exp2(s - m_new).astype(BF16))
            if first:
                acc[hd] = pv
            else:
                acc[hd, :, lo:] = jnp.exp2(ms[hd][:, lo:] - m_new) * acc[hd, :, lo:] + pv
            out.append(m_new if lo == 0 else jnp.concatenate([ms[hd][:, :lo], m_new], axis=1))
        return tuple(out)

    assert tq == 2 * tk
    scores(n_full, s_a, mc_a, True)
    scores(n_full + 1, s_b, mc_b, True, lo=tk)
    ms = softmax_pv(n_full, s_a, mc_a, None, True)
    scores(0, s_a, mc_a, False)
    ms = softmax_pv(n_full + 1, s_b, mc_b, ms, False, lo=tk)

    def body(jj, ms):
        j = 2 * jj
        scores(j + 1, s_b, mc_b, False)
        ms = softmax_pv(j, s_a, mc_a, ms, False)
        scores(j + 2, s_a, mc_a, False)
        return softmax_pv(j + 1, s_b, mc_b, ms, False)

    npairs = n_full // 2 - 1
    ms = lax.fori_loop(0, npairs // 2, lambda t, ms: body(2 * t + 1, body(2 * t, ms)), ms)
    ms = lax.cond(jnp.logical_and(npairs > 0, npairs % 2 == 1),
                  lambda ms: body(npairs - 1, ms), lambda ms: ms, ms)

    @pl.when(n_full > 0)
    def _():
        scores(n_full - 1, s_b, mc_b, False)
        ms2 = softmax_pv(n_full - 2, s_a, mc_a, ms, False)
        softmax_pv(n_full - 1, s_b, mc_b, ms2, False)

    lp = lam_ref[...]
    lam = (jnp.exp(jnp.sum(lp[0:1] * lp[1:2], axis=-1, keepdims=True))
           - jnp.exp(jnp.sum(lp[2:3] * lp[3:4], axis=-1, keepdims=True)) + lambda_init)
    for pr in range(nh // 2):
        a1 = acc[2 * pr]
        a2 = acc[2 * pr + 1]
        o = a1[:V_DIM] / a1[V_DIM:V_DIM + 1] - lam * (a2[:V_DIM] / a2[V_DIM:V_DIM + 1])
        o = o * lax.rsqrt(jnp.mean(o * o, axis=0, keepdims=True) + EPS)
        o = (o * g_ref[...]) * (1.0 - lambda_init)
        o_ref[:, pr * V_DIM:(pr + 1) * V_DIM] = o.T.astype(BF16)


def _attention(qt, k, vt, lam_rows, g_col, tq, tk, lambda_init, cast_weights):
    bsz, d, t = qt.shape
    n = bsz * t
    nq = t // tq
    nh = GROUP // HEAD_DIM
    ng = d // GROUP
    plans = [_cast_plan(w, bsz * ng * nq, lambda b, g, i: (b * ng + g) * nq + i) for w in cast_weights]
    outs = pl.pallas_call(
        functools.partial(_attn_kernel, tk=tk, lambda_init=lambda_init, ncast=len(plans)),
        out_shape=(jax.ShapeDtypeStruct((n, d), BF16),) + tuple(pln[2] for pln in plans),
        grid=(bsz, ng, nq),
        in_specs=[
            pl.BlockSpec((1, GROUP, tq), lambda b, g, i: (b, g, i)),
            pl.BlockSpec((t, GROUP), lambda b, g, i: (b, g)),
            pl.BlockSpec((1, t // tk, nh // 2, V_ROWS, tk), lambda b, g, i: (b, 0, g, 0, 0)),
            pl.BlockSpec((4, HEAD_DIM), lambda b, g, i: (0, 0)),
            pl.BlockSpec((V_DIM, 1), lambda b, g, i: (0, 0)),
        ] + [pln[0] for pln in plans],
        out_specs=(pl.BlockSpec((tq, GROUP), lambda b, g, i: (b * nq + i, g)),) + tuple(pln[1] for pln in plans),
        scratch_shapes=[pltpu.VMEM((nh, tk, tq), F32), pltpu.VMEM((nh, tk, tq), F32),
                        pltpu.VMEM((nh, 1, tq), F32), pltpu.VMEM((nh, 1, tq), F32),
                        pltpu.VMEM((nh, V_ROWS, tq), F32)],
        compiler_params=_cparams(("arbitrary", "arbitrary", "arbitrary")),
        name="diff_attn",
    )(qt, k, vt, lam_rows, g_col, *cast_weights)
    return outs[0], outs[1:]


def _ffn_kernel(a_ref, wo_ref, x_ref, mod_ref, g_ref, wg_ref, wu_ref, wd_ref, o_ref, *, tf):
    m = mod_ref[0]
    x = x_ref[...] + m[2:3] * _dot(a_ref[...], wo_ref[...])
    h = _norm_mod(x, g_ref[...], m[3:4], m[4:5]).astype(BF16)
    f = wg_ref.shape[1]
    acc = None
    for ci in range(f // tf):
        sl = slice(ci * tf, (ci + 1) * tf)
        gt = _dot(h, wg_ref[:, sl])
        up = _dot(h, wu_ref[:, sl])
        a = (gt * _sigmoid(gt) * up).astype(BF16)
        part = _dot(a, wd_ref[sl, :])
        acc = part if acc is None else acc + part
    o_ref[...] = x + m[5:6] * acc


def _ffn(a, wo, x2, mod, g, wg, wu, wd, bsz, tm, tf):
    n, d = x2.shape
    f = wg.shape[1]
    tiles_per_b = n // bsz // tm
    row = lambda i: (i, 0)
    const = lambda i: (0, 0)
    return pl.pallas_call(
        functools.partial(_ffn_kernel, tf=tf),
        out_shape=jax.ShapeDtypeStruct((n, d), F32),
        grid=(n // tm,),
        in_specs=[
            pl.BlockSpec((tm, d), row),
            pl.BlockSpec((d, d), const),
            pl.BlockSpec((tm, d), row),
            pl.BlockSpec((1, MOD_ROWS, d), lambda i: (i // tiles_per_b, 0, 0)),
            pl.BlockSpec((1, d), const),
            pl.BlockSpec((d, f), const),
            pl.BlockSpec((d, f), const),
            pl.BlockSpec((f, d), const),
        ],
        out_specs=pl.BlockSpec((tm, d), row),
        compiler_params=_cparams(("arbitrary",)),
        name="oproj_dense_swiglu",
    )(a, wo, x2, mod, g, wg, wu, wd)


def _route_top2(wr_t, h, idx_ref, gate_ref):
    logits = lax.dot_general(wr_t, h, (((1,), (1,)), ((), ())),
                             preferred_element_type=F32, precision=lax.Precision.HIGHEST)
    ne = logits.shape[0]
    eidx = lax.broadcasted_iota(jnp.int32, logits.shape, 0)
    v1 = jnp.max(logits, axis=0, keepdims=True)
    i1 = jnp.min(jnp.where(logits == v1, eidx, ne), axis=0, keepdims=True)
    rest = jnp.where(eidx == i1, -jnp.inf, logits)
    v2 = jnp.max(rest, axis=0, keepdims=True)
    i2 = jnp.min(jnp.where(rest == v2, eidx, ne), axis=0, keepdims=True)
    e2 = jnp.exp(v2 - v1)
    den = 1.0 + e2
    idx_ref[0:1, :] = i1
    idx_ref[1:2, :] = i2
    gate_ref[0:1, :] = 1.0 / den
    gate_ref[1:2, :] = e2 / den


def _conformer_kernel(x_ref, mod_ref, gm_ref, w1_ref, b1_ref, wdw_ref, bdw_ref, lg_ref, lb_ref, w2_ref, b2_ref,
                      gf_ref, wr_ref, x_out_ref, h_ref, idx_ref, gate_ref, ext, vbuf, *, tiles_per_b, rb):
    tm, d = x_ref.shape
    nchunks = d // LANES
    first = (pl.program_id(0) % tiles_per_b) == 0

    @pl.when(first)
    def _():
        ext[:, 0:CONV_HALO, :] = jnp.zeros((nchunks, CONV_HALO, LANES), F32)

    @pl.when(jnp.logical_not(first))
    def _():
        ext[:, 0:CONV_HALO, :] = ext[:, tm:tm + CONV_HALO, :]

    m = mod_ref[0]
    x = x_ref[...]
    h = _norm_mod(x, gm_ref[...], m[0:1], m[1:2]).astype(BF16)
    off = CONV_HALO - (CONV_WIDTH - 1)
    tn = 2 * LANES
    for ci in range(d // tn):
        a = _dot(h, w1_ref[:, ci * tn:(ci + 1) * tn]) + b1_ref[:, ci * tn:(ci + 1) * tn]
        gt = _dot(h, w1_ref[:, d + ci * tn:d + (ci + 1) * tn]) + b1_ref[:, d + ci * tn:d + (ci + 1) * tn]
        u = a * _sigmoid(gt)
        for half in range(tn // LANES):
            c = ci * (tn // LANES) + half
            cols = slice(c * LANES, (c + 1) * LANES)
            ext[c, CONV_HALO:, :] = u[:, half * LANES:(half + 1) * LANES]
            for r in range(tm // rb):
                acc = jnp.broadcast_to(bdw_ref[:, cols], (rb, LANES))
                for kk in range(CONV_WIDTH):
                    acc = acc + wdw_ref[kk:kk + 1, cols] * ext[c, r * rb + off + kk:r * rb + off + kk + rb, :]
                vbuf[r * rb:(r + 1) * rb, cols] = acc
    v = vbuf[...]
    mu = jnp.mean(v, axis=-1, keepdims=True)
    vc = v - mu
    var = jnp.mean(vc * vc, axis=-1, keepdims=True)
    y = vc * lax.rsqrt(var + EPS) * lg_ref[...] + lb_ref[...]
    y = (y * _sigmoid(y)).astype(BF16)
    x = x + m[2:3] * (_dot(y, w2_ref[...]) + b2_ref[...])
    x_out_ref[...] = x
    hf = _norm_mod(x, gf_ref[...], m[3:4], m[4:5])
    _store_slabs(h_ref, hf)
    _route_top2(wr_ref[...], hf, idx_ref, gate_ref)


def _conformer(x2, mod, gm, w1, b1, wdw, bdw, ln_g, ln_b, w2, b2, gf, wr_t, bsz, tm):
    n, d = x2.shape
    ne = wr_t.shape[0]
    tiles_per_b = n // bsz // tm
    row = lambda i: (i, 0)
    const = lambda i: (0, 0)
    vec = pl.BlockSpec((1, d), const)
    return pl.pallas_call(
        functools.partial(_conformer_kernel, tiles_per_b=tiles_per_b, rb=CONV_ROW_CHUNK),
        out_shape=(jax.ShapeDtypeStruct((n, d), F32), jax.ShapeDtypeStruct((n * SLAB, LANES), F32),
                   jax.ShapeDtypeStruct((TOP_K, n), jnp.int32), jax.ShapeDtypeStruct((TOP_K, n), F32)),
        grid=(n // tm,),
        in_specs=[
            pl.BlockSpec((tm, d), row),
            pl.BlockSpec((1, MOD_ROWS, d), lambda i: (i // tiles_per_b, 0, 0)),
            vec,
            pl.BlockSpec((d, 2 * d), const),
            pl.BlockSpec((1, 2 * d), const),
            pl.BlockSpec((CONV_WIDTH, d), const),
            vec, vec, vec,
            pl.BlockSpec((d, d), const),
            vec, vec,
            pl.BlockSpec((ne, d), const),
        ],
        out_specs=(pl.BlockSpec((tm, d), row), pl.BlockSpec((tm * SLAB, LANES), row),
                   pl.BlockSpec((TOP_K, tm), lambda i: (0, i)), pl.BlockSpec((TOP_K, tm), lambda i: (0, i))),
        scratch_shapes=[pltpu.VMEM((d // LANES, tm + CONV_HALO, LANES), F32), pltpu.VMEM((tm, d), F32)],
        compiler_params=_cparams(("arbitrary",)),
        name="conformer_router",
    )(x2, mod, gm, w1, b1, wdw, bdw, ln_g, ln_b, w2, b2, gf, wr_t)


def _moe_kernel(be_ref, na_ref, cnt_ref, src_ref, dst_ref, h_hbm, wg_ref, wu_ref, wd_ref, y_hbm,
                xbuf, xb16, acc, ybuf, gsem, ssem, *, n_assign, n_spare_blocks):
    tm = xb16.shape[0]
    nparts, prows = acc.shape[0:2]
    nblk = pl.num_programs(0)
    rpb = tm // LANES
    blk = pl.program_id(0)
    f = pl.program_id(1)
    na = na_ref[0]
    active = blk < na
    slot = blk % 2
    other = 1 - slot

    def slab(row):
        start = row * SLAB
        return pl.ds(start if isinstance(start, int) else pl.multiple_of(start, SLAB), SLAB)

    def gather_row(trow, col, buf, part, r):
        return pltpu.make_async_copy(h_hbm.at[slab(src_ref[trow, col])], xbuf.at[buf, part, slab(r)],
                                     gsem.at[buf])

    def scatter_row(trow, col, buf, part, r):
        return pltpu.make_async_copy(ybuf.at[buf, part, slab(r)], y_hbm.at[slab(dst_ref[trow, col])],
                                     ssem.at[buf])

    def wait_gather(buf):
        pltpu.make_async_copy(xbuf.at[buf], xbuf.at[buf], gsem.at[buf]).wait()

    def wait_scatter(buf):
        pltpu.make_async_copy(ybuf.at[buf], ybuf.at[buf], ssem.at[buf]).wait()

    def scatter_block(b, buf):
        def body(i, carry):
            scatter_row(b * rpb + i // LANES, i % LANES, buf, i // prows, i % prows).start()
            return carry
        lax.fori_loop(0, tm, body, 0)
        wait_scatter(buf)

    @pl.when(jnp.logical_and(blk == 0, f == 0))
    def _():
        def body(i, carry):
            gather_row(i // LANES, i % LANES, 0, i // prows, i % prows).start()
            return carry
        lax.fori_loop(0, tm, body, 0)
        ybuf[1] = jnp.zeros(ybuf.shape[1:], F32)
        spare_fill = [pltpu.make_async_copy(
            ybuf.at[1, 0], y_hbm.at[pl.ds((n_assign + j * prows) * SLAB, prows * SLAB)], ssem.at[1])
            for j in range(n_spare_blocks * nparts)]
        for cp in spare_fill:
            cp.start()
        for cp in spare_fill:
            cp.wait()

    @pl.when(jnp.logical_and(active, f == 0))
    def _():
        wait_gather(slot)
        for part in range(nparts):
            for s in range(SLAB):
                xb16[part * prows:(part + 1) * prows, s * LANES:(s + 1) * LANES] = (
                    _load_slab_cols(xbuf.at[slot, part], s, prows).astype(BF16))
        acc[...] = jnp.zeros(acc.shape, F32)

    def ffn_rows(part):
        xb = xb16[part * prows:(part + 1) * prows]
        gt = _dot(xb, wg_ref[0])
        up = _dot(xb, wu_ref[0])
        acc[part] += _dot((gt * _sigmoid(gt) * up).astype(BF16), wd_ref[0])

    @pl.when(active)
    def _():
        qrows = prows // LANES
        gbase = (blk + 1) * rpb + f * qrows
        sbase = jnp.where(blk == 0, nblk, blk - 1) * rpb + f * qrows
        for i in range(prows):
            gather_row(gbase + i // LANES, i % LANES, other, f, i).start()
            scatter_row(sbase + i // LANES, i % LANES, other, f, i).start()
        ffn_rows(0)

    assert nparts == 2
    @pl.when(jnp.logical_and(active, cnt_ref[blk] > prows))
    def _():
        ffn_rows(1)

    @pl.when(jnp.logical_and(active, f == nparts - 1))
    def _():
        @pl.when(blk >= 1)
        def _():
            wait_scatter(slot)
        for part in range(nparts):
            _store_slabs(ybuf.at[slot, part], acc[part])

    @pl.when(jnp.logical_and(f == 0, blk == na))
    def _():
        wait_gather(slot)
        wait_scatter(slot)
        scatter_block(blk - 1, other)

    @pl.when(jnp.logical_and(jnp.logical_and(f == nparts - 1, blk == nblk - 1), active))
    def _():
        wait_gather(other)
        wait_scatter(other)
        scatter_block(blk, slot)


def _moe(block_e, n_active, block_cnt, src_tab, dst_tab, h, wg, wu, wd, n_out, tm, tf):
    n = h.shape[0] // SLAB
    ne, d, fe = wg.shape
    assert d == SLAB * LANES
    nf = fe // tf
    nblk = dst_tab.shape[0] * LANES // tm - 1
    assert tm % (nf * LANES) == 0

    def wsel(blk, f, na):
        return jnp.where(blk < na[0], f, nf - 1)

    grid_spec = pltpu.PrefetchScalarGridSpec(
        num_scalar_prefetch=5,
        grid=(nblk, nf),
        in_specs=[
            pl.BlockSpec(memory_space=pl.ANY),
            pl.BlockSpec((1, d, tf), lambda blk, f, be, na, cnt, src, dst: (be[blk], 0, wsel(blk, f, na))),
            pl.BlockSpec((1, d, tf), lambda blk, f, be, na, cnt, src, dst: (be[blk], 0, wsel(blk, f, na))),
            pl.BlockSpec((1, tf, d), lambda blk, f, be, na, cnt, src, dst: (be[blk], wsel(blk, f, na), 0)),
        ],
        out_specs=pl.BlockSpec(memory_space=pl.ANY),
        scratch_shapes=[pltpu.VMEM((2, nf, tm // nf * SLAB, LANES), F32), pltpu.VMEM((tm, d), BF16),
                        pltpu.VMEM((nf, tm // nf, d), F32), pltpu.VMEM((2, nf, tm // nf * SLAB, LANES), F32),
                        pltpu.SemaphoreType.DMA((2,)), pltpu.SemaphoreType.DMA((2,))],
    )
    return pl.pallas_call(
        functools.partial(_moe_kernel, n_assign=n * TOP_K, n_spare_blocks=ne),
        out_shape=jax.ShapeDtypeStruct((n_out * SLAB, LANES), F32),
        grid_spec=grid_spec,
        compiler_params=_cparams(("arbitrary", "arbitrary")),
        name="moe_experts",
    )(block_e, n_active, block_cnt, src_tab, dst_tab, h, wg, wu, wd)


def _combine_kernel(y0_ref, y1_ref, x_ref, gates_ref, mod_ref, g_ref, o_ref):
    tm, d = x_ref.shape
    gate = mod_ref[0][5:6]
    gt = gates_ref[...]
    g0 = gt[:, 0:1]
    g1 = gt[:, 1:2]
    ssq = jnp.zeros((tm, 1), F32)
    for s in range(SLAB):
        cols = slice(s * LANES, (s + 1) * LANES)
        xs = x_ref[:, cols] + gate[:, cols] * (g0 * _load_slab_cols(y0_ref, s, tm)
                                               + g1 * _load_slab_cols(y1_ref, s, tm))
        o_ref[:, cols] = xs
        ssq = ssq + jnp.sum(xs * xs, axis=-1, keepdims=True)
    o_ref[...] = o_ref[...] * lax.rsqrt(ssq * (1.0 / d) + EPS) * g_ref[...]


def _combine(y, x2, gates_t, mod, final_g, bsz, tm):
    n, d = x2.shape
    tiles_per_b = n // bsz // tm
    nt = n // tm
    row = lambda i: (i, 0)
    return pl.pallas_call(
        _combine_kernel,
        out_shape=jax.ShapeDtypeStruct((n, d), F32),
        grid=(nt,),
        in_specs=[
            pl.BlockSpec((tm * SLAB, LANES), row),
            pl.BlockSpec((tm * SLAB, LANES), lambda i: (nt + i, 0)),
            pl.BlockSpec((tm, d), row),
            pl.BlockSpec((tm, TOP_K), row),
            pl.BlockSpec((1, MOD_ROWS, d), lambda i: (i // tiles_per_b, 0, 0)),
            pl.BlockSpec((1, d), lambda i: (0, 0)),
        ],
        out_specs=pl.BlockSpec((tm, d), row),
        compiler_params=_cparams(("arbitrary",)),
        name="moe_combine_final_norm",
    )(y, y, x2, gates_t, mod, final_g)


def _routing(idx, ne, tm):
    n = idx.shape[1]
    a_tot = n * TOP_K
    flat_e = idx.T.reshape(-1)
    sizes = jnp.sum((flat_e[:, None] == jnp.arange(ne, dtype=jnp.int32)[None, :]).astype(jnp.int32), axis=0)
    padded = (sizes + tm - 1) // tm * tm
    pad_end = jnp.cumsum(padded)
    pad_start = pad_end - padded
    grp_start = jnp.cumsum(sizes) - sizes
    p = a_tot + ne * tm
    nblk = p // tm
    n_active = (pad_end[-1] // tm).astype(jnp.int32)
    blk_start = jnp.arange(nblk, dtype=jnp.int32) * tm
    block_e = jnp.sum((blk_start[:, None] >= pad_end[None, :]).astype(jnp.int32), axis=1)
    last_e = jnp.sum((jnp.maximum(pad_end[-1] - 1, 0) >= pad_end).astype(jnp.int32))
    block_e = jnp.where(jnp.arange(nblk) < n_active, block_e, last_e).astype(jnp.int32)
    order = jnp.argsort(flat_e).astype(jnp.int32)
    within0 = blk_start - pad_start[block_e]
    size_b = sizes[block_e]
    grp_b = grp_start[block_e]
    lane = jnp.arange(tm, dtype=jnp.int32)[None, :]
    within = within0[:, None] + lane
    valid = within < size_b[:, None]
    seg = order[jnp.clip(grp_b[:, None] + within, 0, a_tot - 1)]
    src_tok = jnp.where(valid, seg // TOP_K, 0).astype(jnp.int32)
    spare = a_tot + blk_start[:, None] + lane - grp_b[:, None] - jnp.minimum(within, size_b[:, None])
    dst_row = jnp.where(valid, (seg % TOP_K) * n + seg // TOP_K, spare).astype(jnp.int32)
    src_tab = jnp.pad(src_tok.reshape(-1), (0, tm)).reshape(-1, LANES)
    dst_tab = jnp.concatenate([dst_row.reshape(-1), p + jnp.arange(tm, dtype=jnp.int32)]).reshape(-1, LANES)
    block_cnt = jnp.clip(size_b - within0, 0, tm).astype(jnp.int32)
    return block_e, n_active.reshape(1), block_cnt, src_tab, dst_tab, p + tm


def _rope_layout(w):
    rows, d = w.shape
    heads_per_group = GROUP // HEAD_DIM
    w5 = w.reshape(rows, d // GROUP, heads_per_group, 2, HALF)
    return w5.transpose(0, 1, 3, 2, 4).reshape(rows, d)


def _mod_rows(mod_l, bsz, d):
    m = mod_l[:bsz].reshape(bsz, 6, d)
    return jnp.pad(m, ((0, 0), (0, MOD_ROWS - 6), (0, 0)))


@jax.jit
def kernel(x, c, positions, w_ada, b_ada, norm_mix_g, norm_ffn_g, attn_w_qkv, attn_w_o, lam_q1, lam_k1,
           lam_q2, lam_k2, attn_subln_g, conv_w_pw1, conv_b_pw1, conv_w_dw, conv_b_dw, conv_ln_g,
           conv_ln_b, conv_w_pw2, conv_b_pw2, ffn_w_gate, ffn_w_up, ffn_w_down, moe_w_router,
           moe_w_gate, moe_w_up, moe_w_down, final_g):
    bsz, t, d = x.shape
    n = bsz * t
    depth = w_ada.shape[0]
    assert depth == 2 and d % GROUP == 0 and bsz <= MOD_ROWS
    tm = min(ROW_TILE, t)
    tk = min(KEY_CHUNK, t)
    x2 = x.reshape(n, d)

    c8 = jnp.pad(c, ((0, MOD_ROWS - bsz), (0, 0)))
    mod = _ada(c8, w_ada, b_ada)
    mod0 = _mod_rows(mod[0], bsz, d)
    mod1 = _mod_rows(mod[1], bsz, d)

    inv_freq = ROPE_THETA ** (-jnp.arange(HALF, dtype=F32) / HALF)
    w_qkv = attn_w_qkv[0]
    wqt = _rope_layout(w_qkv[:, :d]).T.astype(BF16)
    wk = _rope_layout(w_qkv[:, d:2 * d]).astype(BF16)
    wvt = w_qkv[:, 2 * d:].T.astype(BF16)
    qt, k, vt, (ffn_wg, ffn_wu, ffn_wd, w_o, w_pw1, w_pw2) = _qkv(
        x2, mod0, norm_mix_g[0:1], positions.reshape(n // tm, 1, tm), inv_freq.reshape(HALF, 1),
        wqt, wk, wvt, bsz, tm, tk,
        [ffn_w_gate[0], ffn_w_up[0], ffn_w_down[0], attn_w_o[0], conv_w_pw1[0], conv_w_pw2[0]])
    lam_rows = jnp.concatenate([lam_q1[0:1], lam_k1[0:1], lam_q2[0:1], lam_k2[0:1]], axis=0)
    lambda_init = 0.8 - 0.6 * math.exp(-0.3 * 0)
    ne, _, fe = moe_w_gate.shape[1:]
    o, (moe_wg, moe_wu, moe_wd) = _attention(
        qt, k, vt, lam_rows, attn_subln_g[0].reshape(V_DIM, 1), tm, tk, lambda_init,
        [moe_w_gate[0].reshape(ne * d, fe), moe_w_up[0].reshape(ne * d, fe), moe_w_down[0].reshape(ne * fe, d)])
    x2 = _ffn(o, w_o, x2, mod0, norm_ffn_g[0:1], ffn_wg, ffn_wu, ffn_wd, bsz, tm, tf=FFN_CHUNK)

    x2, h, idx, gates = _conformer(x2, mod1, norm_mix_g[1:2], w_pw1, conv_b_pw1[0:1], conv_w_dw[0],
                                   conv_b_dw[0:1], conv_ln_g[0:1], conv_ln_b[0:1], w_pw2, conv_b_pw2[0:1],
                                   norm_ffn_g[1:2], moe_w_router[0].T, bsz, tm)
    tm_e = EXPERT_BLOCK_TILES * tm
    block_e, n_active, block_cnt, src_tab, dst_tab, n_out = _routing(idx, ne, tm_e)
    tf_e = fe // EXPERT_BLOCK_TILES
    y = _moe(block_e, n_active, block_cnt, src_tab, dst_tab, h, moe_wg.reshape(ne, d, fe),
             moe_wu.reshape(ne, d, fe), moe_wd.reshape(ne, fe, d), n_out, tm_e, tf_e)
    out = _combine(y, x2, gates.T, mod1, final_g.reshape(1, d), bsz, tm)
    return out.reshape(bsz, t, d)
```

```python
import functools
import math

import jax
import jax.numpy as jnp
from jax import lax
from jax.experimental import pallas as pl
from jax.experimental.pallas import tpu as pltpu

EPS = 1e-6
ROPE_THETA = 10000.0
HEAD_DIM = 64
V_DIM = 128
V_ROWS = V_DIM + 16
HALF = HEAD_DIM // 2
LANES = 128
GROUP = 4 * HEAD_DIM
CONV_WIDTH = 31
CONV_HALO = 32
TOP_K = 2
NEG = -1e30
VMEM_LIMIT = 56 * 1024 * 1024

ROW_TILE = 512
KEY_CHUNK = 256
FFN_CHUNK = 256
EXPERT_BLOCK_TILES = 2
ADA_COL_TILE = 1536
CONV_ROW_CHUNK = 64
MOD_ROWS = 8

BF16 = jnp.bfloat16
F32 = jnp.float32


def _cparams(sem):
    return pltpu.CompilerParams(dimension_semantics=sem, vmem_limit_bytes=VMEM_LIMIT)


def _dot(a, b):
    return jnp.dot(a, b, preferred_element_type=F32)


def _dot_nt(a, b):
    return lax.dot_general(a, b, (((1,), (1,)), ((), ())), preferred_element_type=F32)


def _norm_mod(x, g, shift, scale):
    y = x * lax.rsqrt(jnp.mean(x * x, axis=-1, keepdims=True) + EPS)
    return (y * g) * (1.0 + scale) + shift


def _sigmoid(x):
    return 1.0 / (1.0 + jnp.exp(-x))


SLAB = 8


def _store_slabs(ref, val):
    rows = val.shape[0]
    for s in range(SLAB):
        ref[pl.ds(s, rows, stride=SLAB), :] = val[:, s * LANES:(s + 1) * LANES]


def _load_slab_cols(ref, s, rows):
    return ref[pl.ds(s, rows, stride=SLAB), :]


BF16_SUBLANES = 16


def _cast_plan(w2d, nsteps, step_of):
    rows, cols = w2d.shape
    share = 1
    while rows % (nsteps // share) or (rows // (nsteps // share)) % BF16_SUBLANES:
        share *= 2
        assert share <= nsteps, (rows, nsteps)
    rb = rows // (nsteps // share)
    imap = lambda *ids: (step_of(*ids) // share, 0)
    return (pl.BlockSpec((rb, cols), imap), pl.BlockSpec((rb, cols), imap),
            jax.ShapeDtypeStruct((rows, cols), BF16))


def _cast_blocks(refs):
    k = len(refs) // 2
    for src, dst in zip(refs[:k], refs[k:]):
        dst[...] = src[...].astype(BF16)


def _ada_kernel(c_ref, w_ref, b_ref, o_ref):
    c = c_ref[...]
    ca = (c * _sigmoid(c)).astype(BF16)
    o_ref[0] = _dot(ca, w_ref[0].astype(BF16)) + b_ref[0]


def _ada(c8, w_ada, b_ada):
    depth, d, d6 = w_ada.shape
    tn = ADA_COL_TILE
    return pl.pallas_call(
        _ada_kernel,
        out_shape=jax.ShapeDtypeStruct((depth, MOD_ROWS, d6), F32),
        grid=(depth, d6 // tn),
        in_specs=[
            pl.BlockSpec((MOD_ROWS, d), lambda l, j: (0, 0)),
            pl.BlockSpec((1, d, tn), lambda l, j: (l, 0, j)),
            pl.BlockSpec((1, 1, tn), lambda l, j: (l, 0, j)),
        ],
        out_specs=pl.BlockSpec((1, MOD_ROWS, tn), lambda l, j: (l, 0, j)),
        compiler_params=_cparams(("arbitrary", "arbitrary")),
        name="ada_mod",
    )(c8, w_ada, b_ada.reshape(depth, 1, d6))


def _qkv_kernel(x_ref, mod_ref, g_ref, pos_ref, freq_ref, wqt_ref, wk_ref, wvt_ref, *rest, tk, ncast):
    qt_ref, k_ref, vt_ref = rest[ncast:ncast + 3]
    _cast_blocks(rest[:ncast] + rest[ncast + 3:])
    m = mod_ref[0]
    h = _norm_mod(x_ref[...], g_ref[...], m[0:1], m[1:2]).astype(BF16)
    d = x_ref.shape[1]
    tm = x_ref.shape[0]
    ang = freq_ref[...] * pos_ref[0].astype(F32)
    reps = LANES // HALF
    ct = jnp.concatenate([jnp.cos(ang)] * reps, axis=0)
    st = jnp.concatenate([jnp.sin(ang)] * reps, axis=0)
    c = ct.T
    s = st.T
    for gi in range(d // GROUP):
        y = _dot(h, wk_ref[:, gi * GROUP:(gi + 1) * GROUP])
        a = y[:, :LANES]
        b = y[:, LANES:]
        k_ref[:, gi * GROUP:gi * GROUP + LANES] = (a * c - b * s).astype(BF16)
        k_ref[:, gi * GROUP + LANES:(gi + 1) * GROUP] = (b * c + a * s).astype(BF16)
    mult = HEAD_DIM ** -0.5 * math.log2(math.e)
    for gi in range(d // GROUP):
        y = _dot_nt(wqt_ref[gi * GROUP:(gi + 1) * GROUP, :], h)
        a = y[:LANES]
        b = y[LANES:]
        qt_ref[0, gi * GROUP:gi * GROUP + LANES, :] = ((a * ct - b * st) * mult).astype(BF16)
        qt_ref[0, gi * GROUP + LANES:(gi + 1) * GROUP, :] = ((b * ct + a * st) * mult).astype(BF16)
    ones = jnp.ones((V_ROWS - V_DIM, tk), BF16)
    for gi in range(d // GROUP):
        vt = _dot_nt(wvt_ref[gi * GROUP:(gi + 1) * GROUP, :], h)
        for hh in range(GROUP // V_DIM):
            for ci in range(tm // tk):
                vt_ref[0, ci, 2 * gi + hh, 0:V_DIM, :] = (
                    vt[hh * V_DIM:(hh + 1) * V_DIM, ci * tk:(ci + 1) * tk].astype(BF16))
                vt_ref[0, ci, 2 * gi + hh, V_DIM:V_ROWS, :] = ones


def _qkv(x2, mod, g, pos_row, freq_col, wqt, wk, wvt, bsz, tm, tk, cast_weights):
    n, d = x2.shape
    t = n // bsz
    tiles_per_b = t // tm
    nh = d // V_DIM
    row = lambda i: (i, 0)
    const = lambda i: (0, 0)
    plans = [_cast_plan(w, n // tm, lambda i: i) for w in cast_weights]
    outs = pl.pallas_call(
        functools.partial(_qkv_kernel, tk=tk, ncast=len(plans)),
        out_shape=(jax.ShapeDtypeStruct((bsz, d, t), BF16), jax.ShapeDtypeStruct((n, d), BF16),
                   jax.ShapeDtypeStruct((bsz, t // tk, nh, V_ROWS, tk), BF16)) + tuple(p[2] for p in plans),
        grid=(n // tm,),
        in_specs=[
            pl.BlockSpec((tm, d), row),
            pl.BlockSpec((1, MOD_ROWS, d), lambda i: (i // tiles_per_b, 0, 0)),
            pl.BlockSpec((1, d), const),
            pl.BlockSpec((1, 1, tm), lambda i: (i, 0, 0)),
            pl.BlockSpec((HALF, 1), const),
            pl.BlockSpec((d, d), const),
            pl.BlockSpec((d, d), const),
            pl.BlockSpec((d, d), const),
        ] + [p[0] for p in plans],
        out_specs=(pl.BlockSpec((1, d, tm), lambda i: (i // tiles_per_b, 0, i % tiles_per_b)),
                   pl.BlockSpec((tm, d), row),
                   pl.BlockSpec((1, tm // tk, nh, V_ROWS, tk),
                                lambda i: (i // tiles_per_b, i % tiles_per_b, 0, 0, 0)))
        + tuple(p[1] for p in plans),
        compiler_params=_cparams(("arbitrary",)),
        name="qkv_rope",
    )(x2, mod, g, pos_row, freq_col, wqt, wk, wvt, *cast_weights)
    return outs[0], outs[1], outs[2], outs[3:]


def _attn_kernel(qt_ref, k_ref, vt_ref, lam_ref, g_ref, *rest, tk, lambda_init, ncast):
    o_ref = rest[ncast]
    s_a, s_b, mc_a, mc_b, acc = rest[2 * ncast + 1:]
    _cast_blocks(rest[:ncast] + rest[ncast + 1:2 * ncast + 1])
    tq = qt_ref.shape[2]
    nh = acc.shape[0]
    qi = pl.program_id(2)
    n_full = qi * (tq // tk)
    row = lax.broadcasted_iota(jnp.int32, (GROUP, 1), 0)
    head = (row % LANES) // HALF
    qf = qt_ref[0].astype(F32)
    qm = [jnp.where(head == hd, qf, 0.0).astype(BF16) for hd in range(nh)]

    def scores(kj, s_ref, mc_ref, masked, lo=0):
        kc = k_ref[pl.ds(pl.multiple_of(kj * tk, tk), tk), :]
        for hd in range(nh):
            s = _dot(kc, qm[hd][:, lo:])
            if masked:
                kidx = kj * tk + lax.broadcasted_iota(jnp.int32, s.shape, 0)
                qidx = qi * tq + lo + lax.broadcasted_iota(jnp.int32, s.shape, 1)
                s = jnp.where(kidx <= qidx, s, NEG)
            s_ref[hd, :, lo:] = s
            mc_ref[hd, :, lo:] = jnp.max(s, axis=0, keepdims=True)

    def softmax_pv(kj, s_ref, mc_ref, ms, first, lo=0):
        out = []
        for hd in range(nh):
            vc = vt_ref[0, kj, hd // 2]
            s = s_ref[hd, :, lo:]
            mc = mc_ref[hd, :, lo:]
            m_new = mc if first else jnp.maximum(ms[hd][:, lo:], mc)
            pv = _dot(vc, jnp.exp2(s - m_new).astype(BF16))
            if first:
                acc[hd] = pv
            else:
                acc[hd, :, lo:] = jnp.exp2(ms[hd][:, lo:] - m_new) * acc[hd, :, lo:] + pv
            out.append(m_new if lo == 0 else jnp.concatenate([ms[hd][:, :lo], m_new], axis=1))
        return tuple(out)

    assert tq == 2 * tk
    scores(n_full, s_a, mc_a, True)
    scores(n_full + 1, s_b, mc_b, True, lo=tk)
    ms = softmax_pv(n_full, s_a, mc_a, None, True)
    scores(0, s_a, mc_a, False)
    ms = softmax_pv(n_full + 1, s_b, mc_b, ms, False, lo=tk)

    def body(jj, ms):
        j = 2 * jj
        scores(j + 1, s_b, mc_b, False)
        ms = softmax_pv(j, s_a, mc_a, ms, False)
        scores(j + 2, s_a, mc_a, False)
        return softmax_pv(j + 1, s_b, mc_b, ms, False)

    npairs = n_full // 2 - 1
    ms = lax.fori_loop(0, npairs // 2, lambda t, ms: body(2 * t + 1, body(2 * t, ms)), ms)
    ms = lax.cond(jnp.logical_and(npairs > 0, npairs % 2 == 1),
                  lambda ms: body(npairs - 1, ms), lambda ms: ms, ms)

    @pl.when(n_full > 0)
    def _():
        scores(n_full - 1, s_b, mc_b, False)
        ms2 = softmax_pv(n_full - 2, s_a, mc_a, ms, False)
        softmax_pv(n_full - 1, s_b, mc_b, ms2, False)

    lp = lam_ref[...]
    lam = (jnp.exp(jnp.sum(lp[0:1] * lp[1:2], axis=-1, keepdims=True))
           - jnp.exp(jnp.sum(lp[2:3] * lp[3:4], axis=-1, keepdims=True)) + lambda_init)
    for pr in range(nh // 2):
        a1 = acc[2 * pr]
        a2 = acc[2 * pr + 1]
        o = a1[:V_DIM] / a1[V_DIM:V_DIM + 1] - lam * (a2[:V_DIM] / a2[V_DIM:V_DIM + 1])
        o = o * lax.rsqrt(jnp.mean(o * o, axis=0, keepdims=True) + EPS)
        o = (o * g_ref[...]) * (1.0 - lambda_init)
        o_ref[:, pr * V_DIM:(pr + 1) * V_DIM] = o.T.astype(BF16)


def _attention(qt, k, vt, lam_rows, g_col, tq, tk, lambda_init, cast_weights):
    bsz, d, t = qt.shape
    n = bsz * t
    nq = t // tq
    nh = GROUP // HEAD_DIM
    ng = d // GROUP
    plans = [_cast_plan(w, bsz * ng * nq, lambda b, g, i: (b * ng + g) * nq + i) for w in cast_weights]
    outs = pl.pallas_call(
        functools.partial(_attn_kernel, tk=tk, lambda_init=lambda_init, ncast=len(plans)),
        out_shape=(jax.ShapeDtypeStruct((n, d), BF16),) + tuple(pln[2] for pln in plans),
        grid=(bsz, ng, nq),
        in_specs=[
            pl.BlockSpec((1, GROUP, tq), lambda b, g, i: (b, g, i)),
            pl.BlockSpec((t, GROUP), lambda b, g, i: (b, g)),
            pl.BlockSpec((1, t // tk, nh // 2, V_ROWS, tk), lambda b, g, i: (b, 0, g, 0, 0)),
            pl.BlockSpec((4, HEAD_DIM), lambda b, g, i: (0, 0)),
            pl.BlockSpec((V_DIM, 1), lambda b, g, i: (0, 0)),
        ] + [pln[0] for pln in plans],
        out_specs=(pl.BlockSpec((tq, GROUP), lambda b, g, i: (b * nq + i, g)),) + tuple(pln[1] for pln in plans),
        scratch_shapes=[pltpu.VMEM((nh, tk, tq), F32), pltpu.VMEM((nh, tk, tq), F32),
                        pltpu.VMEM((nh, 1, tq), F32), pltpu.VMEM((nh, 1, tq), F32),
                        pltpu.VMEM((nh, V_ROWS, tq), F32)],
        compiler_params=_cparams(("arbitrary", "arbitrary", "arbitrary")),
        name="diff_attn",
    )(qt, k, vt, lam_rows, g_col, *cast_weights)
    return outs[0], outs[1:]


def _ffn_kernel(a_ref, wo_ref, x_ref, mod_ref, g_ref, wg_ref, wu_ref, wd_ref, o_ref, *, tf):
    m = mod_ref[0]
    x = x_ref[...] + m[2:3] * _dot(a_ref[...], wo_ref[...])
    h = _norm_mod(x, g_ref[...], m[3:4], m[4:5]).astype(BF16)
    f = wg_ref.shape[1]
    acc = None
    for ci in range(f // tf):
        sl = slice(ci * tf, (ci + 1) * tf)
        gt = _dot(h, wg_ref[:, sl])
        up = _dot(h, wu_ref[:, sl])
        a = (gt * _sigmoid(gt) * up).astype(BF16)
        part = _dot(a, wd_ref[sl, :])
        acc = part if acc is None else acc + part
    o_ref[...] = x + m[5:6] * acc


def _ffn(a, wo, x2, mod, g, wg, wu, wd, bsz, tm, tf):
    n, d = x2.shape
    f = wg.shape[1]
    tiles_per_b = n // bsz // tm
    row = lambda i: (i, 0)
    const = lambda i: (0, 0)
    return pl.pallas_call(
        functools.partial(_ffn_kernel, tf=tf),
        out_shape=jax.ShapeDtypeStruct((n, d), F32),
        grid=(n // tm,),
        in_specs=[
            pl.BlockSpec((tm, d), row),
            pl.BlockSpec((d, d), const),
            pl.BlockSpec((tm, d), row),
            pl.BlockSpec((1, MOD_ROWS, d), lambda i: (i // tiles_per_b, 0, 0)),
            pl.BlockSpec((1, d), const),
            pl.BlockSpec((d, f), const),
            pl.BlockSpec((d, f), const),
            pl.BlockSpec((f, d), const),
        ],
        out_specs=pl.BlockSpec((tm, d), row),
        compiler_params=_cparams(("arbitrary",)),
        name="oproj_dense_swiglu",
    )(a, wo, x2, mod, g, wg, wu, wd)


def _route_top2(wr_t, h, idx_ref, gate_ref):
    logits = lax.dot_general(wr_t, h, (((1,), (1,)), ((), ())),
                             preferred_element_type=F32, precision=lax.Precision.HIGHEST)
    ne = logits.shape[0]
    eidx = lax.broadcasted_iota(jnp.int32, logits.shape, 0)
    v1 = jnp.max(logits, axis=0, keepdims=True)
    i1 = jnp.min(jnp.where(logits == v1, eidx, ne), axis=0, keepdims=True)
    rest = jnp.where(eidx == i1, -jnp.inf, logits)
    v2 = jnp.max(rest, axis=0, keepdims=True)
    i2 = jnp.min(jnp.where(rest == v2, eidx, ne), axis=0, keepdims=True)
    e2 = jnp.exp(v2 - v1)
    den = 1.0 + e2
    idx_ref[0:1, :] = i1
    idx_ref[1:2, :] = i2
    gate_ref[0:1, :] = 1.0 / den
    gate_ref[1:2, :] = e2 / den


def _conformer_kernel(x_ref, mod_ref, gm_ref, w1_ref, b1_ref, wdw_ref, bdw_ref, lg_ref, lb_ref, w2_ref, b2_ref,
                      gf_ref, wr_ref, x_out_ref, h_ref, idx_ref, gate_ref, ext, vbuf, *, tiles_per_b, rb):
    tm, d = x_ref.shape
    nchunks = d // LANES
    first = (pl.program_id(0) % tiles_per_b) == 0

    @pl.when(first)
    def _():
        ext[:, 0:CONV_HALO, :] = jnp.zeros((nchunks, CONV_HALO, LANES), F32)

    @pl.when(jnp.logical_not(first))
    def _():
        ext[:, 0:CONV_HALO, :] = ext[:, tm:tm + CONV_HALO, :]

    m = mod_ref[0]
    x = x_ref[...]
    h = _norm_mod(x, gm_ref[...], m[0:1], m[1:2]).astype(BF16)
    off = CONV_HALO - (CONV_WIDTH - 1)
    tn = 2 * LANES
    for ci in range(d // tn):
        a = _dot(h, w1_ref[:, ci * tn:(ci + 1) * tn]) + b1_ref[:, ci * tn:(ci + 1) * tn]
        gt = _dot(h, w1_ref[:, d + ci * tn:d + (ci + 1) * tn]) + b1_ref[:, d + ci * tn:d + (ci + 1) * tn]
        u = a * _sigmoid(gt)
        for half in range(tn // LANES):
            c = ci * (tn // LANES) + half
            cols = slice(c * LANES, (c + 1) * LANES)
            ext[c, CONV_HALO:, :] = u[:, half * LANES:(half + 1) * LANES]
            for r in range(tm // rb):
                acc = jnp.broadcast_to(bdw_ref[:, cols], (rb, LANES))
                for kk in range(CONV_WIDTH):
                    acc = acc + wdw_ref[kk:kk + 1, cols] * ext[c, r * rb + off + kk:r * rb + off + kk + rb, :]
                vbuf[r * rb:(r + 1) * rb, cols] = acc
    v = vbuf[...]
    mu = jnp.mean(v, axis=-1, keepdims=True)
    vc = v - mu
    var = jnp.mean(vc * vc, axis=-1, keepdims=True)
    y = vc * lax.rsqrt(var + EPS) * lg_ref[...] + lb_ref[...]
    y = (y * _sigmoid(y)).astype(BF16)
    x = x + m[2:3] * (_dot(y, w2_ref[...]) + b2_ref[...])
    x_out_ref[...] = x
    hf = _norm_mod(x, gf_ref[...], m[3:4], m[4:5])
    _store_slabs(h_ref, hf)
    _route_top2(wr_ref[...], hf, idx_ref, gate_ref)


def _conformer(x2, mod, gm, w1, b1, wdw, bdw, ln_g, ln_b, w2, b2, gf, wr_t, bsz, tm):
    n, d = x2.shape
    ne = wr_t.shape[0]
    tiles_per_b = n // bsz // tm
    row = lambda i: (i, 0)
    const = lambda i: (0, 0)
    vec = pl.BlockSpec((1, d), const)
    return pl.pallas_call(
        functools.partial(_conformer_kernel, tiles_per_b=tiles_per_b, rb=CONV_ROW_CHUNK),
        out_shape=(jax.ShapeDtypeStruct((n, d), F32), jax.ShapeDtypeStruct((n * SLAB, LANES), F32),
                   jax.ShapeDtypeStruct((TOP_K, n), jnp.int32), jax.ShapeDtypeStruct((TOP_K, n), F32)),
        grid=(n // tm,),
        in_specs=[
            pl.BlockSpec((tm, d), row),
            pl.BlockSpec((1, MOD_ROWS, d), lambda i: (i // tiles_per_b, 0, 0)),
            vec,
            pl.BlockSpec((d, 2 * d), const),
            pl.BlockSpec((1, 2 * d), const),
            pl.BlockSpec((CONV_WIDTH, d), const),
            vec, vec, vec,
            pl.BlockSpec((d, d), const),
            vec, vec,
            pl.BlockSpec((ne, d), const),
        ],
        out_specs=(pl.BlockSpec((tm, d), row), pl.BlockSpec((tm * SLAB, LANES), row),
                   pl.BlockSpec((TOP_K, tm), lambda i: (0, i)), pl.BlockSpec((TOP_K, tm), lambda i: (0, i))),
        scratch_shapes=[pltpu.VMEM((d // LANES, tm + CONV_HALO, LANES), F32), pltpu.VMEM((tm, d), F32)],
        compiler_params=_cparams(("arbitrary",)),
        name="conformer_router",
    )(x2, mod, gm, w1, b1, wdw, bdw, ln_g, ln_b, w2, b2, gf, wr_t)


def _moe_kernel(be_ref, na_ref, cnt_ref, src_ref, dst_ref, h_hbm, wg_ref, wu_ref, wd_ref, y_hbm,
                xbuf, xb16, acc, ybuf, gsem, ssem, *, n_assign, n_spare_blocks):
    tm = xb16.shape[0]
    nparts, prows = acc.shape[0:2]
    nblk = pl.num_programs(0)
    rpb = tm // LANES
    blk = pl.program_id(0)
    f = pl.program_id(1)
    na = na_ref[0]
    active = blk < na
    slot = blk % 2
    other = 1 - slot

    def slab(row):
        start = row * SLAB
        return pl.ds(start if isinstance(start, int) else pl.multiple_of(start, SLAB), SLAB)

    def gather_row(trow, col, buf, part, r):
        return pltpu.make_async_copy(h_hbm.at[slab(src_ref[trow, col])], xbuf.at[buf, part, slab(r)],
                                     gsem.at[buf])

    def scatter_row(trow, col, buf, part, r):
        return pltpu.make_async_copy(ybuf.at[buf, part, slab(r)], y_hbm.at[slab(dst_ref[trow, col])],
                                     ssem.at[buf])

    def wait_gather(buf):
        pltpu.make_async_copy(xbuf.at[buf], xbuf.at[buf], gsem.at[buf]).wait()

    def wait_scatter(buf):
        pltpu.make_async_copy(ybuf.at[buf], ybuf.at[buf], ssem.at[buf]).wait()

    def scatter_block(b, buf):
        def body(i, carry):
            scatter_row(b * rpb + i // LANES, i % LANES, buf, i // prows, i % prows).start()
            return carry
        lax.fori_loop(0, tm, body, 0)
        wait_scatter(buf)

    @pl.when(jnp.logical_and(blk == 0, f == 0))
    def _():
        def body(i, carry):
            gather_row(i // LANES, i % LANES, 0, i // prows, i % prows).start()
            return carry
        lax.fori_loop(0, tm, body, 0)
        ybuf[1] = jnp.zeros(ybuf.shape[1:], F32)
        spare_fill = [pltpu.make_async_copy(
            ybuf.at[1, 0], y_hbm.at[pl.ds((n_assign + j * prows) * SLAB, prows * SLAB)], ssem.at[1])
            for j in range(n_spare_blocks * nparts)]
        for cp in spare_fill:
            cp.start()
        for cp in spare_fill:
            cp.wait()

    @pl.when(jnp.logical_and(active, f == 0))
    def _():
        wait_gather(slot)
        for part in range(nparts):
            for s in range(SLAB):
                xb16[part * prows:(part + 1) * prows, s * LANES:(s + 1) * LANES] = (
                    _load_slab_cols(xbuf.at[slot, part], s, prows).astype(BF16))
        acc[...] = jnp.zeros(acc.shape, F32)

    def ffn_rows(part):
        xb = xb16[part * prows:(part + 1) * prows]
        gt = _dot(xb, wg_ref[0])
        up = _dot(xb, wu_ref[0])
        acc[part] += _dot((gt * _sigmoid(gt) * up).astype(BF16), wd_ref[0])

    @pl.when(active)
    def _():
        qrows = prows // LANES
        gbase = (blk + 1) * rpb + f * qrows
        sbase = jnp.where(blk == 0, nblk, blk - 1) * rpb + f * qrows
        for i in range(prows):
            gather_row(gbase + i // LANES, i % LANES, other, f, i).start()
            scatter_row(sbase + i // LANES, i % LANES, other, f, i).start()
        ffn_rows(0)

    assert nparts == 2
    @pl.when(jnp.logical_and(active, cnt_ref[blk] > prows))
    def _():
        ffn_rows(1)

    @pl.when(jnp.logical_and(active, f == nparts - 1))
    def _():
        @pl.when(blk >= 1)
        def _():
            wait_scatter(slot)
        for part in range(nparts):
            _store_slabs(ybuf.at[slot, part], acc[part])

    @pl.when(jnp.logical_and(f == 0, blk == na))
    def _():
        wait_gather(slot)
        wait_scatter(slot)
        scatter_block(blk - 1, other)

    @pl.when(jnp.logical_and(jnp.logical_and(f == nparts - 1, blk == nblk - 1), active))
    def _():
        wait_gather(other)
        wait_scatter(other)
        scatter_block(blk, slot)


def _moe(block_e, n_active, block_cnt, src_tab, dst_tab, h, wg, wu, wd, n_out, tm, tf):
    n = h.shape[0] // SLAB
    ne, d, fe = wg.shape
    assert d == SLAB * LANES
    nf = fe // tf
    nblk = dst_tab.shape[0] * LANES // tm - 1
    assert tm % (nf * LANES) == 0

    def wsel(blk, f, na):
        return jnp.where(blk < na[0], f, nf - 1)

    grid_spec = pltpu.PrefetchScalarGridSpec(
        num_scalar_prefetch=5,
        grid=(nblk, nf),
        in_specs=[
            pl.BlockSpec(memory_space=pl.ANY),
            pl.BlockSpec((1, d, tf), lambda blk, f, be, na, cnt, src, dst: (be[blk], 0, wsel(blk, f, na))),
            pl.BlockSpec((1, d, tf), lambda blk, f, be, na, cnt, src, dst: (be[blk], 0, wsel(blk, f, na))),
            pl.BlockSpec((1, tf, d), lambda blk, f, be, na, cnt, src, dst: (be[blk], wsel(blk, f, na), 0)),
        ],
        out_specs=pl.BlockSpec(memory_space=pl.ANY),
        scratch_shapes=[pltpu.VMEM((2, nf, tm // nf * SLAB, LANES), F32), pltpu.VMEM((tm, d), BF16),
                        pltpu.VMEM((nf, tm // nf, d), F32), pltpu.VMEM((2, nf, tm // nf * SLAB, LANES), F32),
                        pltpu.SemaphoreType.DMA((2,)), pltpu.SemaphoreType.DMA((2,))],
    )
    return pl.pallas_call(
        functools.partial(_moe_kernel, n_assign=n * TOP_K, n_spare_blocks=ne),
        out_shape=jax.ShapeDtypeStruct((n_out * SLAB, LANES), F32),
        grid_spec=grid_spec,
        compiler_params=_cparams(("arbitrary", "arbitrary")),
        name="moe_experts",
    )(block_e, n_active, block_cnt, src_tab, dst_tab, h, wg, wu, wd)


def _combine_kernel(y0_ref, y1_ref, x_ref, gates_ref, mod_ref, g_ref, o_ref):
    tm, d = x_ref.shape
    gate = mod_ref[0][5:6]
    gt = gates_ref[...]
    g0 = gt[:, 0:1]
    g1 = gt[:, 1:2]
    ssq = jnp.zeros((tm, 1), F32)
    for s in range(SLAB):
        cols = slice(s * LANES, (s + 1) * LANES)
        xs = x_ref[:, cols] + gate[:, cols] * (g0 * _load_slab_cols(y0_ref, s, tm)
                                               + g1 * _load_slab_cols(y1_ref, s, tm))
        o_ref[:, cols] = xs
        ssq = ssq + jnp.sum(xs * xs, axis=-1, keepdims=True)
    o_ref[...] = o_ref[...] * lax.rsqrt(ssq * (1.0 / d) + EPS) * g_ref[...]


def _combine(y, x2, gates_t, mod, final_g, bsz, tm):
    n, d = x2.shape
    tiles_per_b = n // bsz // tm
    nt = n // tm
    row = lambda i: (i, 0)
    return pl.pallas_call(
        _combine_kernel,
        out_shape=jax.ShapeDtypeStruct((n, d), F32),
        grid=(nt,),
        in_specs=[
            pl.BlockSpec((tm * SLAB, LANES), row),
            pl.BlockSpec((tm * SLAB, LANES), lambda i: (nt + i, 0)),
            pl.BlockSpec((tm, d), row),
            pl.BlockSpec((tm, TOP_K), row),
            pl.BlockSpec((1, MOD_ROWS, d), lambda i: (i // tiles_per_b, 0, 0)),
            pl.BlockSpec((1, d), lambda i: (0, 0)),
        ],
        out_specs=pl.BlockSpec((tm, d), row),
        compiler_params=_cparams(("arbitrary",)),
        name="moe_combine_final_norm",
    )(y, y, x2, gates_t, mod, final_g)


def _routing(idx, ne, tm):
    n = idx.shape[1]
    a_tot = n * TOP_K
    flat_e = idx.T.reshape(-1)
    sizes = jnp.sum((flat_e[:, None] == jnp.arange(ne, dtype=jnp.int32)[None, :]).astype(jnp.int32), axis=0)
    padded = (sizes + tm - 1) // tm * tm
    pad_end = jnp.cumsum(padded)
    pad_start = pad_end - padded
    p = a_tot + ne * tm
    nblk = p // tm
    n_active = (pad_end[-1] // tm).astype(jnp.int32)
    blk_start = jnp.arange(nblk, dtype=jnp.int32) * tm
    block_e = jnp.sum((blk_start[:, None] >= pad_end[None, :]).astype(jnp.int32), axis=1)
    last_e = jnp.sum((jnp.maximum(pad_end[-1] - 1, 0) >= pad_end).astype(jnp.int32))
    block_e = jnp.where(jnp.arange(nblk) < n_active, block_e, last_e).astype(jnp.int32)
    fill_end = jnp.cumsum(padded - sizes)
    filler = jnp.arange(ne * tm, dtype=jnp.int32)
    filler_e = jnp.sum((filler[:, None] >= fill_end[None, :]).astype(jnp.int32), axis=1)
    keys = jnp.concatenate([flat_e * 2, filler_e * 2 + 1])
    payload = jnp.concatenate([jnp.arange(a_tot, dtype=jnp.int32), -1 - filler])
    _, slot_src = lax.sort((keys, payload), num_keys=1)
    valid = slot_src >= 0
    src_tok = jnp.where(valid, slot_src // TOP_K, 0).astype(jnp.int32)
    dst_row = jnp.where(valid, (slot_src % TOP_K) * n + slot_src // TOP_K, a_tot - 1 - slot_src).astype(jnp.int32)
    src_tab = jnp.pad(src_tok, (0, tm)).reshape(-1, LANES)
    dst_tab = jnp.concatenate([dst_row, p + jnp.arange(tm, dtype=jnp.int32)]).reshape(-1, LANES)
    block_cnt = jnp.clip(sizes[block_e] - (blk_start - pad_start[block_e]), 0, tm).astype(jnp.int32)
    return block_e, n_active.reshape(1), block_cnt, src_tab, dst_tab, p + tm


def _rope_layout(w):
    rows, d = w.shape
    heads_per_group = GROUP // HEAD_DIM
    w5 = w.reshape(rows, d // GROUP, heads_per_group, 2, HALF)
    return w5.transpose(0, 1, 3, 2, 4).reshape(rows, d)


def _mod_rows(mod_l, bsz, d):
    m = mod_l[:bsz].reshape(bsz, 6, d)
    return jnp.pad(m, ((0, 0), (0, MOD_ROWS - 6), (0, 0)))


@jax.jit
def kernel(x, c, positions, w_ada, b_ada, norm_mix_g, norm_ffn_g, attn_w_qkv, attn_w_o, lam_q1, lam_k1,
           lam_q2, lam_k2, attn_subln_g, conv_w_pw1, conv_b_pw1, conv_w_dw, conv_b_dw, conv_ln_g,
           conv_ln_b, conv_w_pw2, conv_b_pw2, ffn_w_gate, ffn_w_up, ffn_w_down, moe_w_router,
           moe_w_gate, moe_w_up, moe_w_down, final_g):
    bsz, t, d = x.shape
    n = bsz * t
    depth = w_ada.shape[0]
    assert depth == 2 and d % GROUP == 0 and bsz <= MOD_ROWS
    tm = min(ROW_TILE, t)
    tk = min(KEY_CHUNK, t)
    x2 = x.reshape(n, d)

    c8 = jnp.pad(c, ((0, MOD_ROWS - bsz), (0, 0)))
    mod = _ada(c8, w_ada, b_ada)
    mod0 = _mod_rows(mod[0], bsz, d)
    mod1 = _mod_rows(mod[1], bsz, d)

    inv_freq = ROPE_THETA ** (-jnp.arange(HALF, dtype=F32) / HALF)
    w_qkv = attn_w_qkv[0]
    wqt = _rope_layout(w_qkv[:, :d]).T.astype(BF16)
    wk = _rope_layout(w_qkv[:, d:2 * d]).astype(BF16)
    wvt = w_qkv[:, 2 * d:].T.astype(BF16)
    qt, k, vt, (ffn_wg, ffn_wu, ffn_wd, w_o, w_pw1, w_pw2) = _qkv(
        x2, mod0, norm_mix_g[0:1], positions.reshape(n // tm, 1, tm), inv_freq.reshape(HALF, 1),
        wqt, wk, wvt, bsz, tm, tk,
        [ffn_w_gate[0], ffn_w_up[0], ffn_w_down[0], attn_w_o[0], conv_w_pw1[0], conv_w_pw2[0]])
    lam_rows = jnp.concatenate([lam_q1[0:1], lam_k1[0:1], lam_q2[0:1], lam_k2[0:1]], axis=0)
    lambda_init = 0.8 - 0.6 * math.exp(-0.3 * 0)
    ne, _, fe = moe_w_gate.shape[1:]
    o, (moe_wg, moe_wu, moe_wd) = _attention(
        qt, k, vt, lam_rows, attn_subln_g[0].reshape(V_DIM, 1), tm, tk, lambda_init,
        [moe_w_gate[0].reshape(ne * d, fe), moe_w_up[0].reshape(ne * d, fe), moe_w_down[0].reshape(ne * fe, d)])
    x2 = _ffn(o, w_o, x2, mod0, norm_ffn_g[0:1], ffn_wg, ffn_wu, ffn_wd, bsz, tm, tf=FFN_CHUNK)

    x2, h, idx, gates = _conformer(x2, mod1, norm_mix_g[1:2], w_pw1, conv_b_pw1[0:1], conv_w_dw[0],
                                   conv_b_dw[0:1], conv_ln_g[0:1], conv_ln_b[0:1], w_pw2, conv_b_pw2[0:1],
                                   norm_ffn_g[1:2], moe_w_router[0].T, bsz, tm)
    tm_e = EXPERT_BLOCK_TILES * tm
    block_e, n_active, block_cnt, src_tab, dst_tab, n_out = _routing(idx, ne, tm_e)
    tf_e = fe // EXPERT_BLOCK_TILES
    y = _moe(block_e, n_active, block_cnt, src_tab, dst_tab, h, moe_wg.reshape(ne, d, fe),
             moe_wu.reshape(ne, d, fe), moe_wd.reshape(ne, fe, d), n_out, tm_e, tf_e)
    out = _combine(y, x2, gates.T, mod1, final_g.reshape(1, d), bsz, tm)
    return out.reshape(bsz, t, d)
```

```python
import functools
import math

import jax
import jax.numpy as jnp
from jax import lax
from jax.experimental import pallas as pl
from jax.experimental.pallas import tpu as pltpu

EPS = 1e-6
ROPE_THETA = 10000.0
HEAD_DIM = 64
V_DIM = 128
V_ROWS = V_DIM + 16
HALF = HEAD_DIM // 2
LANES = 128
GROUP = 4 * HEAD_DIM
CONV_WIDTH = 31
CONV_HALO = 32
TOP_K = 2
NEG = -1e30
VMEM_LIMIT = 56 * 1024 * 1024

ROW_TILE = 512
KEY_CHUNK = 256
FFN_CHUNK = 256
EXPERT_BLOCK_TILES = 2
ADA_COL_TILE = 1536
CONV_ROW_CHUNK = 64
MOD_ROWS = 8

BF16 = jnp.bfloat16
F32 = jnp.float32


def _cparams(sem):
    return pltpu.CompilerParams(dimension_semantics=sem, vmem_limit_bytes=VMEM_LIMIT)


def _dot(a, b):
    return jnp.dot(a, b, preferred_element_type=F32)


def _dot_nt(a, b):
    return lax.dot_general(a, b, (((1,), (1,)), ((), ())), preferred_element_type=F32)


def _norm_mod(x, g, shift, scale):
    y = x * lax.rsqrt(jnp.mean(x * x, axis=-1, keepdims=True) + EPS)
    return (y * g) * (1.0 + scale) + shift


def _sigmoid(x):
    return 1.0 / (1.0 + jnp.exp(-x))


SLAB = 8


def _store_slabs(ref, val):
    rows = val.shape[0]
    for s in range(SLAB):
        ref[pl.ds(s, rows, stride=SLAB), :] = val[:, s * LANES:(s + 1) * LANES]


def _load_slab_cols(ref, s, rows):
    return ref[pl.ds(s, rows, stride=SLAB), :]


BF16_SUBLANES = 16


def _cast_plan(w2d, nsteps, step_of):
    rows, cols = w2d.shape
    share = 1
    while rows % (nsteps // share) or (rows // (nsteps // share)) % BF16_SUBLANES:
        share *= 2
        assert share <= nsteps, (rows, nsteps)
    rb = rows // (nsteps // share)
    imap = lambda *ids: (step_of(*ids) // share, 0)
    return (pl.BlockSpec((rb, cols), imap), pl.BlockSpec((rb, cols), imap),
            jax.ShapeDtypeStruct((rows, cols), BF16))


def _cast_blocks(refs):
    k = len(refs) // 2
    for src, dst in zip(refs[:k], refs[k:]):
        dst[...] = src[...].astype(BF16)


def _ada_kernel(c_ref, w_ref, b_ref, o_ref):
    c = c_ref[...]
    ca = (c * _sigmoid(c)).astype(BF16)
    o_ref[0] = _dot(ca, w_ref[0].astype(BF16)) + b_ref[0]


def _ada(c8, w_ada, b_ada):
    depth, d, d6 = w_ada.shape
    tn = ADA_COL_TILE
    return pl.pallas_call(
        _ada_kernel,
        out_shape=jax.ShapeDtypeStruct((depth, MOD_ROWS, d6), F32),
        grid=(depth, d6 // tn),
        in_specs=[
            pl.BlockSpec((MOD_ROWS, d), lambda l, j: (0, 0)),
            pl.BlockSpec((1, d, tn), lambda l, j: (l, 0, j)),
            pl.BlockSpec((1, 1, tn), lambda l, j: (l, 0, j)),
        ],
        out_specs=pl.BlockSpec((1, MOD_ROWS, tn), lambda l, j: (l, 0, j)),
        compiler_params=_cparams(("arbitrary", "arbitrary")),
        name="ada_mod",
    )(c8, w_ada, b_ada.reshape(depth, 1, d6))


def _qkv_kernel(x_ref, mod_ref, g_ref, pos_ref, freq_ref, wqt_ref, wk_ref, wvt_ref, *rest, tk, ncast):
    qt_ref, k_ref, vt_ref = rest[ncast:ncast + 3]
    _cast_blocks(rest[:ncast] + rest[ncast + 3:])
    m = mod_ref[0]
    h = _norm_mod(x_ref[...], g_ref[...], m[0:1], m[1:2]).astype(BF16)
    d = x_ref.shape[1]
    tm = x_ref.shape[0]
    ang = freq_ref[...] * pos_ref[0].astype(F32)
    reps = LANES // HALF
    ct = jnp.concatenate([jnp.cos(ang)] * reps, axis=0)
    st = jnp.concatenate([jnp.sin(ang)] * reps, axis=0)
    c = ct.T
    s = st.T
    for gi in range(d // GROUP):
        y = _dot(h, wk_ref[:, gi * GROUP:(gi + 1) * GROUP])
        a = y[:, :LANES]
        b = y[:, LANES:]
        k_ref[:, gi * GROUP:gi * GROUP + LANES] = (a * c - b * s).astype(BF16)
        k_ref[:, gi * GROUP + LANES:(gi + 1) * GROUP] = (b * c + a * s).astype(BF16)
    mult = HEAD_DIM ** -0.5 * math.log2(math.e)
    for gi in range(d // GROUP):
        y = _dot_nt(wqt_ref[gi * GROUP:(gi + 1) * GROUP, :], h)
        a = y[:LANES]
        b = y[LANES:]
        qt_ref[0, gi * GROUP:gi * GROUP + LANES, :] = ((a * ct - b * st) * mult).astype(BF16)
        qt_ref[0, gi * GROUP + LANES:(gi + 1) * GROUP, :] = ((b * ct + a * st) * mult).astype(BF16)
    ones = jnp.ones((V_ROWS - V_DIM, tk), BF16)
    for gi in range(d // GROUP):
        vt = _dot_nt(wvt_ref[gi * GROUP:(gi + 1) * GROUP, :], h)
        for hh in range(GROUP // V_DIM):
            for ci in range(tm // tk):
                vt_ref[0, ci, 2 * gi + hh, 0:V_DIM, :] = (
                    vt[hh * V_DIM:(hh + 1) * V_DIM, ci * tk:(ci + 1) * tk].astype(BF16))
                vt_ref[0, ci, 2 * gi + hh, V_DIM:V_ROWS, :] = ones


def _qkv(x2, mod, g, pos_row, freq_col, wqt, wk, wvt, bsz, tm, tk, cast_weights):
    n, d = x2.shape
    t = n // bsz
    tiles_per_b = t // tm
    nh = d // V_DIM
    row = lambda i: (i, 0)
    const = lambda i: (0, 0)
    plans = [_cast_plan(w, n // tm, lambda i: i) for w in cast_weights]
    outs = pl.pallas_call(
        functools.partial(_qkv_kernel, tk=tk, ncast=len(plans)),
        out_shape=(jax.ShapeDtypeStruct((bsz, d, t), BF16), jax.ShapeDtypeStruct((n, d), BF16),
                   jax.ShapeDtypeStruct((bsz, t // tk, nh, V_ROWS, tk), BF16)) + tuple(p[2] for p in plans),
        grid=(n // tm,),
        in_specs=[
            pl.BlockSpec((tm, d), row),
            pl.BlockSpec((1, MOD_ROWS, d), lambda i: (i // tiles_per_b, 0, 0)),
            pl.BlockSpec((1, d), const),
            pl.BlockSpec((1, 1, tm), lambda i: (i, 0, 0)),
            pl.BlockSpec((HALF, 1), const),
            pl.BlockSpec((d, d), const),
            pl.BlockSpec((d, d), const),
            pl.BlockSpec((d, d), const),
        ] + [p[0] for p in plans],
        out_specs=(pl.BlockSpec((1, d, tm), lambda i: (i // tiles_per_b, 0, i % tiles_per_b)),
                   pl.BlockSpec((tm, d), row),
                   pl.BlockSpec((1, tm // tk, nh, V_ROWS, tk),
                                lambda i: (i // tiles_per_b, i % tiles_per_b, 0, 0, 0)))
        + tuple(p[1] for p in plans),
        compiler_params=_cparams(("arbitrary",)),
        name="qkv_rope",
    )(x2, mod, g, pos_row, freq_col, wqt, wk, wvt, *cast_weights)
    return outs[0], outs[1], outs[2], outs[3:]


def _attn_kernel(qt_ref, k_ref, vt_ref, lam_ref, g_ref, *rest, tk, lambda_init, ncast):
    o_ref = rest[ncast]
    s_a, s_b, mc_a, mc_b, acc = rest[2 * ncast + 1:]
    _cast_blocks(rest[:ncast] + rest[ncast + 1:2 * ncast + 1])
    tq = qt_ref.shape[2]
    nh = acc.shape[0]
    qi = pl.program_id(2)
    n_full = qi * (tq // tk)
    row = lax.broadcasted_iota(jnp.int32, (GROUP, 1), 0)
    head = (row % LANES) // HALF
    qf = qt_ref[0].astype(F32)
    qm = [jnp.where(head == hd, qf, 0.0).astype(BF16) for hd in range(nh)]

    def scores(kj, s_ref, mc_ref, masked, lo=0):
        kc = k_ref[pl.ds(pl.multiple_of(kj * tk, tk), tk), :]
        for hd in range(nh):
            s = _dot(kc, qm[hd][:, lo:])
            if masked:
                kidx = kj * tk + lax.broadcasted_iota(jnp.int32, s.shape, 0)
                qidx = qi * tq + lo + lax.broadcasted_iota(jnp.int32, s.shape, 1)
                s = jnp.where(kidx <= qidx, s, NEG)
            s_ref[hd, :, lo:] = s
            mc_ref[hd, :, lo:] = jnp.max(s, axis=0, keepdims=True)

    def softmax_pv(kj, s_ref, mc_ref, ms, first, lo=0):
        out = []
        for hd in range(nh):
            vc = vt_ref[0, kj, hd // 2]
            s = s_ref[hd, :, lo:]
            mc = mc_ref[hd, :, lo:]
            m_new = mc if first else jnp.maximum(ms[hd][:, lo:], mc)
            pv = _dot(vc, jnp.exp2(s - m_new).astype(BF16))
            if first:
                acc[hd] = pv
            else:
                acc[hd, :, lo:] = jnp.exp2(ms[hd][:, lo:] - m_new) * acc[hd, :, lo:] + pv
            out.append(m_new if lo == 0 else jnp.concatenate([ms[hd][:, :lo], m_new], axis=1))
        return tuple(out)

    assert tq == 2 * tk
    scores(n_full, s_a, mc_a, True)
    scores(n_full + 1, s_b, mc_b, True, lo=tk)
    ms = softmax_pv(n_full, s_a, mc_a, None, True)
    scores(0, s_a, mc_a, False)
    ms = softmax_pv(n_full + 1, s_b, mc_b, ms, False, lo=tk)

    def body(jj, ms):
        j = 2 * jj
        scores(j + 1, s_b, mc_b, False)
        ms = softmax_pv(j, s_a, mc_a, ms, False)
        scores(j + 2, s_a, mc_a, False)
        return softmax_pv(j + 1, s_b, mc_b, ms, False)

    npairs = n_full // 2 - 1
    ms = lax.fori_loop(0, npairs // 2, lambda t, ms: body(2 * t + 1, body(2 * t, ms)), ms)
    ms = lax.cond(jnp.logical_and(npairs > 0, npairs % 2 == 1),
                  lambda ms: body(npairs - 1, ms), lambda ms: ms, ms)

    @pl.when(n_full > 0)
    def _():
        scores(n_full - 1, s_b, mc_b, False)
        ms2 = softmax_pv(n_full - 2, s_a, mc_a, ms, False)
        softmax_pv(n_full - 1, s_b, mc_b, ms2, False)

    lp = lam_ref[...]
    lam = (jnp.exp(jnp.sum(lp[0:1] * lp[1:2], axis=-1, keepdims=True))
           - jnp.exp(jnp.sum(lp[2:3] * lp[3:4], axis=-1, keepdims=True)) + lambda_init)
    for pr in range(nh // 2):
        a1 = acc[2 * pr]
        a2 = acc[2 * pr + 1]
        o = a1[:V_DIM] / a1[V_DIM:V_DIM + 1] - lam * (a2[:V_DIM] / a2[V_DIM:V_DIM + 1])
        o = o * lax.rsqrt(jnp.mean(o * o, axis=0, keepdims=True) + EPS)
        o = (o * g_ref[...]) * (1.0 - lambda_init)
        o_ref[:, pr * V_DIM:(pr + 1) * V_DIM] = o.T.astype(BF16)


def _attention(qt, k, vt, lam_rows, g_col, tq, tk, lambda_init, cast_weights):
    bsz, d, t = qt.shape
    n = bsz * t
    nq = t // tq
    nh = GROUP // HEAD_DIM
    ng = d // GROUP
    plans = [_cast_plan(w, bsz * ng * nq, lambda b, g, i: (b * ng + g) * nq + i) for w in cast_weights]
    outs = pl.pallas_call(
        functools.partial(_attn_kernel, tk=tk, lambda_init=lambda_init, ncast=len(plans)),
        out_shape=(jax.ShapeDtypeStruct((n, d), BF16),) + tuple(pln[2] for pln in plans),
        grid=(bsz, ng, nq),
        in_specs=[
            pl.BlockSpec((1, GROUP, tq), lambda b, g, i: (b, g, i)),
            pl.BlockSpec((t, GROUP), lambda b, g, i: (b, g)),
            pl.BlockSpec((1, t // tk, nh // 2, V_ROWS, tk), lambda b, g, i: (b, 0, g, 0, 0)),
            pl.BlockSpec((4, HEAD_DIM), lambda b, g, i: (0, 0)),
            pl.BlockSpec((V_DIM, 1), lambda b, g, i: (0, 0)),
        ] + [pln[0] for pln in plans],
        out_specs=(pl.BlockSpec((tq, GROUP), lambda b, g, i: (b * nq + i, g)),) + tuple(pln[1] for pln in plans),
        scratch_shapes=[pltpu.VMEM((nh, tk, tq), F32), pltpu.VMEM((nh, tk, tq), F32),
                        pltpu.VMEM((nh, 1, tq), F32), pltpu.VMEM((nh, 1, tq), F32),
                        pltpu.VMEM((nh, V_ROWS, tq), F32)],
        compiler_params=_cparams(("arbitrary", "arbitrary", "arbitrary")),
        name="diff_attn",
    )(qt, k, vt, lam_rows, g_col, *cast_weights)
    return outs[0], outs[1:]


def _ffn_kernel(a_ref, wo_ref, x_ref, mod_ref, g_ref, wg_ref, wu_ref, wd_ref, o_ref, *, tf):
    m = mod_ref[0]
    x = x_ref[...] + m[2:3] * _dot(a_ref[...], wo_ref[...])
    h = _norm_mod(x, g_ref[...], m[3:4], m[4:5]).astype(BF16)
    f = wg_ref.shape[1]
    acc = None
    for ci in range(f // tf):
        sl = slice(ci * tf, (ci + 1) * tf)
        gt = _dot(h, wg_ref[:, sl])
        up = _dot(h, wu_ref[:, sl])
        a = (gt * _sigmoid(gt) * up).astype(BF16)
        part = _dot(a, wd_ref[sl, :])
        acc = part if acc is None else acc + part
    o_ref[...] = x + m[5:6] * acc


def _ffn(a, wo, x2, mod, g, wg, wu, wd, bsz, tm, tf):
    n, d = x2.shape
    f = wg.shape[1]
    tiles_per_b = n // bsz // tm
    row = lambda i: (i, 0)
    const = lambda i: (0, 0)
    return pl.pallas_call(
        functools.partial(_ffn_kernel, tf=tf),
        out_shape=jax.ShapeDtypeStruct((n, d), F32),
        grid=(n // tm,),
        in_specs=[
            pl.BlockSpec((tm, d), row),
            pl.BlockSpec((d, d), const),
            pl.BlockSpec((tm, d), row),
            pl.BlockSpec((1, MOD_ROWS, d), lambda i: (i // tiles_per_b, 0, 0)),
            pl.BlockSpec((1, d), const),
            pl.BlockSpec((d, f), const),
            pl.BlockSpec((d, f), const),
            pl.BlockSpec((f, d), const),
        ],
        out_specs=pl.BlockSpec((tm, d), row),
        compiler_params=_cparams(("arbitrary",)),
        name="oproj_dense_swiglu",
    )(a, wo, x2, mod, g, wg, wu, wd)


def _route_top2(wr_t, h, idx_ref, gate_ref):
    logits = lax.dot_general(wr_t, h, (((1,), (1,)), ((), ())),
                             preferred_element_type=F32, precision=lax.Precision.HIGHEST)
    ne = logits.shape[0]
    eidx = lax.broadcasted_iota(jnp.int32, logits.shape, 0)
    v1 = jnp.max(logits, axis=0, keepdims=True)
    i1 = jnp.min(jnp.where(logits == v1, eidx, ne), axis=0, keepdims=True)
    rest = jnp.where(eidx == i1, -jnp.inf, logits)
    v2 = jnp.max(rest, axis=0, keepdims=True)
    i2 = jnp.min(jnp.where(rest == v2, eidx, ne), axis=0, keepdims=True)
    e2 = jnp.exp(v2 - v1)
    den = 1.0 + e2
    idx_ref[0:1, :] = i1
    idx_ref[1:2, :] = i2
    gate_ref[0:1, :] = 1.0 / den
    gate_ref[1:2, :] = e2 / den


def _conformer_kernel(x_ref, mod_ref, gm_ref, w1_ref, b1_ref, wdw_ref, bdw_ref, lg_ref, lb_ref, w2_ref, b2_ref,
                      gf_ref, wr_ref, x_out_ref, h_ref, idx_ref, gate_ref, ext, vbuf, *, tiles_per_b, rb):
    tm, d = x_ref.shape
    nchunks = d // LANES
    first = (pl.program_id(0) % tiles_per_b) == 0

    @pl.when(first)
    def _():
        ext[:, 0:CONV_HALO, :] = jnp.zeros((nchunks, CONV_HALO, LANES), F32)

    @pl.when(jnp.logical_not(first))
    def _():
        ext[:, 0:CONV_HALO, :] = ext[:, tm:tm + CONV_HALO, :]

    m = mod_ref[0]
    x = x_ref[...]
    h = _norm_mod(x, gm_ref[...], m[0:1], m[1:2]).astype(BF16)
    off = CONV_HALO - (CONV_WIDTH - 1)
    tn = 2 * LANES
    for ci in range(d // tn):
        a = _dot(h, w1_ref[:, ci * tn:(ci + 1) * tn]) + b1_ref[:, ci * tn:(ci + 1) * tn]
        gt = _dot(h, w1_ref[:, d + ci * tn:d + (ci + 1) * tn]) + b1_ref[:, d + ci * tn:d + (ci + 1) * tn]
        u = a * _sigmoid(gt)
        for half in range(tn // LANES):
            c = ci * (tn // LANES) + half
            cols = slice(c * LANES, (c + 1) * LANES)
            ext[c, CONV_HALO:, :] = u[:, half * LANES:(half + 1) * LANES]
            for r in range(tm // rb):
                acc = jnp.broadcast_to(bdw_ref[:, cols], (rb, LANES))
                for kk in range(CONV_WIDTH):
                    acc = acc + wdw_ref[kk:kk + 1, cols] * ext[c, r * rb + off + kk:r * rb + off + kk + rb, :]
                vbuf[r * rb:(r + 1) * rb, cols] = acc
    v = vbuf[...]
    mu = jnp.mean(v, axis=-1, keepdims=True)
    vc = v - mu
    var = jnp.mean(vc * vc, axis=-1, keepdims=True)
    y = vc * lax.rsqrt(var + EPS) * lg_ref[...] + lb_ref[...]
    y = (y * _sigmoid(y)).astype(BF16)
    x = x + m[2:3] * (_dot(y, w2_ref[...]) + b2_ref[...])
    x_out_ref[...] = x
    hf = _norm_mod(x, gf_ref[...], m[3:4], m[4:5])
    _store_slabs(h_ref, hf)
    _route_top2(wr_ref[...], hf, idx_ref, gate_ref)


def _conformer(x2, mod, gm, w1, b1, wdw, bdw, ln_g, ln_b, w2, b2, gf, wr_t, bsz, tm):
    n, d = x2.shape
    ne = wr_t.shape[0]
    tiles_per_b = n // bsz // tm
    row = lambda i: (i, 0)
    const = lambda i: (0, 0)
    vec = pl.BlockSpec((1, d), const)
    return pl.pallas_call(
        functools.partial(_conformer_kernel, tiles_per_b=tiles_per_b, rb=CONV_ROW_CHUNK),
        out_shape=(jax.ShapeDtypeStruct((n, d), F32), jax.ShapeDtypeStruct((n * SLAB, LANES), F32),
                   jax.ShapeDtypeStruct((TOP_K, n), jnp.int32), jax.ShapeDtypeStruct((TOP_K, n), F32)),
        grid=(n // tm,),
        in_specs=[
            pl.BlockSpec((tm, d), row),
            pl.BlockSpec((1, MOD_ROWS, d), lambda i: (i // tiles_per_b, 0, 0)),
            vec,
            pl.BlockSpec((d, 2 * d), const),
            pl.BlockSpec((1, 2 * d), const),
            pl.BlockSpec((CONV_WIDTH, d), const),
            vec, vec, vec,
            pl.BlockSpec((d, d), const),
            vec, vec,
            pl.BlockSpec((ne, d), const),
        ],
        out_specs=(pl.BlockSpec((tm, d), row), pl.BlockSpec((tm * SLAB, LANES), row),
                   pl.BlockSpec((TOP_K, tm), lambda i: (0, i)), pl.BlockSpec((TOP_K, tm), lambda i: (0, i))),
        scratch_shapes=[pltpu.VMEM((d // LANES, tm + CONV_HALO, LANES), F32), pltpu.VMEM((tm, d), F32)],
        compiler_params=_cparams(("arbitrary",)),
        name="conformer_router",
    )(x2, mod, gm, w1, b1, wdw, bdw, ln_g, ln_b, w2, b2, gf, wr_t)


def _moe_kernel(be_ref, na_ref, cnt_ref, src_ref, dst_ref, h_hbm, wg_ref, wu_ref, wd_ref, y_hbm,
                xbuf, xb16, acc, ybuf, gsem, ssem, *, n_assign, n_spare_blocks):
    tm = xb16.shape[0]
    nparts, prows = acc.shape[0:2]
    nblk = pl.num_programs(0)
    rpb = tm // LANES
    blk = pl.program_id(0)
    f = pl.program_id(1)
    na = na_ref[0]
    active = blk < na
    slot = blk % 2
    other = 1 - slot

    def slab(row):
        start = row * SLAB
        return pl.ds(start if isinstance(start, int) else pl.multiple_of(start, SLAB), SLAB)

    def gather_row(trow, col, buf, part, r):
        return pltpu.make_async_copy(h_hbm.at[slab(src_ref[trow, col])], xbuf.at[buf, part, slab(r)],
                                     gsem.at[buf])

    def scatter_row(trow, col, buf, part, r):
        return pltpu.make_async_copy(ybuf.at[buf, part, slab(r)], y_hbm.at[slab(dst_ref[trow, col])],
                                     ssem.at[buf])

    def wait_gather(buf):
        pltpu.make_async_copy(xbuf.at[buf], xbuf.at[buf], gsem.at[buf]).wait()

    def wait_scatter(buf):
        pltpu.make_async_copy(ybuf.at[buf], ybuf.at[buf], ssem.at[buf]).wait()

    def scatter_block(b, buf):
        def body(i, carry):
            scatter_row(b * rpb + i // LANES, i % LANES, buf, i // prows, i % prows).start()
            return carry
        lax.fori_loop(0, tm, body, 0)
        wait_scatter(buf)

    @pl.when(jnp.logical_and(blk == 0, f == 0))
    def _():
        def body(i, carry):
            gather_row(i // LANES, i % LANES, 0, i // prows, i % prows).start()
            return carry
        lax.fori_loop(0, tm, body, 0)
        ybuf[1] = jnp.zeros(ybuf.shape[1:], F32)
        spare_fill = [pltpu.make_async_copy(
            ybuf.at[1, 0], y_hbm.at[pl.ds((n_assign + j * prows) * SLAB, prows * SLAB)], ssem.at[1])
            for j in range(n_spare_blocks * nparts)]
        for cp in spare_fill:
            cp.start()
        for cp in spare_fill:
            cp.wait()

    @pl.when(jnp.logical_and(active, f == 0))
    def _():
        wait_gather(slot)
        for part in range(nparts):
            for s in range(SLAB):
                xb16[part * prows:(part + 1) * prows, s * LANES:(s + 1) * LANES] = (
                    _load_slab_cols(xbuf.at[slot, part], s, prows).astype(BF16))
        acc[...] = jnp.zeros(acc.shape, F32)

    def ffn_rows(part):
        xb = xb16[part * prows:(part + 1) * prows]
        gt = _dot(xb, wg_ref[0])
        up = _dot(xb, wu_ref[0])
        acc[part] += _dot((gt * _sigmoid(gt) * up).astype(BF16), wd_ref[0])

    @pl.when(active)
    def _():
        qrows = prows // LANES
        gbase = (blk + 1) * rpb + f * qrows
        sbase = jnp.where(blk == 0, nblk, blk - 1) * rpb + f * qrows
        for i in range(prows):
            gather_row(gbase + i // LANES, i % LANES, other, f, i).start()
            scatter_row(sbase + i // LANES, i % LANES, other, f, i).start()
        ffn_rows(0)

    assert nparts == 2
    @pl.when(jnp.logical_and(active, cnt_ref[blk] > prows))
    def _():
        ffn_rows(1)

    @pl.when(jnp.logical_and(active, f == nparts - 1))
    def _():
        @pl.when(blk >= 1)
        def _():
            wait_scatter(slot)
        for part in range(nparts):
            _store_slabs(ybuf.at[slot, part], acc[part])

    @pl.when(jnp.logical_and(f == 0, blk == na))
    def _():
        wait_gather(slot)
        wait_scatter(slot)
        scatter_block(blk - 1, other)

    @pl.when(jnp.logical_and(jnp.logical_and(f == nparts - 1, blk == nblk - 1), active))
    def _():
        wait_gather(other)
        wait_scatter(other)
        scatter_block(blk, slot)


def _moe(block_e, n_active, block_cnt, src_tab, dst_tab, h, wg, wu, wd, n_out, tm, tf):
    n = h.shape[0] // SLAB
    ne, d, fe = wg.shape
    assert d == SLAB * LANES
    nf = fe // tf
    nblk = dst_tab.shape[0] * LANES // tm - 1
    assert tm % (nf * LANES) == 0

    def wsel(blk, f, na):
        return jnp.where(blk < na[0], f, nf - 1)

    grid_spec = pltpu.PrefetchScalarGridSpec(
        num_scalar_prefetch=5,
        grid=(nblk, nf),
        in_specs=[
            pl.BlockSpec(memory_space=pl.ANY),
            pl.BlockSpec((1, d, tf), lambda blk, f, be, na, cnt, src, dst: (be[blk], 0, wsel(blk, f, na))),
            pl.BlockSpec((1, d, tf), lambda blk, f, be, na, cnt, src, dst: (be[blk], 0, wsel(blk, f, na))),
            pl.BlockSpec((1, tf, d), lambda blk, f, be, na, cnt, src, dst: (be[blk], wsel(blk, f, na), 0)),
        ],
        out_specs=pl.BlockSpec(memory_space=pl.ANY),
        scratch_shapes=[pltpu.VMEM((2, nf, tm // nf * SLAB, LANES), F32), pltpu.VMEM((tm, d), BF16),
                        pltpu.VMEM((nf, tm // nf, d), F32), pltpu.VMEM((2, nf, tm // nf * SLAB, LANES), F32),
                        pltpu.SemaphoreType.DMA((2,)), pltpu.SemaphoreType.DMA((2,))],
    )
    return pl.pallas_call(
        functools.partial(_moe_kernel, n_assign=n * TOP_K, n_spare_blocks=ne),
        out_shape=jax.ShapeDtypeStruct((n_out * SLAB, LANES), F32),
        grid_spec=grid_spec,
        compiler_params=_cparams(("arbitrary", "arbitrary")),
        name="moe_experts",
    )(block_e, n_active, block_cnt, src_tab, dst_tab, h, wg, wu, wd)


def _combine_kernel(y0_ref, y1_ref, x_ref, gates_ref, mod_ref, g_ref, o_ref):
    tm, d = x_ref.shape
    gate = mod_ref[0][5:6]
    gt = gates_ref[...]
    g0 = gt[:, 0:1]
    g1 = gt[:, 1:2]
    ssq = jnp.zeros((tm, 1), F32)
    for s in range(SLAB):
        cols = slice(s * LANES, (s + 1) * LANES)
        xs = x_ref[:, cols] + gate[:, cols] * (g0 * _load_slab_cols(y0_ref, s, tm)
                                               + g1 * _load_slab_cols(y1_ref, s, tm))
        o_ref[:, cols] = xs
        ssq = ssq + jnp.sum(xs * xs, axis=-1, keepdims=True)
    o_ref[...] = o_ref[...] * lax.rsqrt(ssq * (1.0 / d) + EPS) * g_ref[...]


def _combine(y, x2, gates_t, mod, final_g, bsz, tm):
    n, d = x2.shape
    tiles_per_b = n // bsz // tm
    nt = n // tm
    row = lambda i: (i, 0)
    return pl.pallas_call(
        _combine_kernel,
        out_shape=jax.ShapeDtypeStruct((n, d), F32),
        grid=(nt,),
        in_specs=[
            pl.BlockSpec((tm * SLAB, LANES), row),
            pl.BlockSpec((tm * SLAB, LANES), lambda i: (nt + i, 0)),
            pl.BlockSpec((tm, d), row),
            pl.BlockSpec((tm, TOP_K), row),
            pl.BlockSpec((1, MOD_ROWS, d), lambda i: (i // tiles_per_b, 0, 0)),
            pl.BlockSpec((1, d), lambda i: (0, 0)),
        ],
        out_specs=pl.BlockSpec((tm, d), row),
        compiler_params=_cparams(("arbitrary",)),
        name="moe_combine_final_norm",
    )(y, y, x2, gates_t, mod, final_g)


def _routing(idx, ne, tm):
    n = idx.shape[1]
    a_tot = n * TOP_K
    flat_e = idx.T.reshape(-1)
    sizes = jnp.sum((flat_e[:, None] == jnp.arange(ne, dtype=jnp.int32)[None, :]).astype(jnp.int32), axis=0)
    padded = (sizes + tm - 1) // tm * tm
    pad_end = jnp.cumsum(padded)
    pad_start = pad_end - padded
    p = a_tot + ne * tm
    nblk = p // tm
    n_active = (pad_end[-1] // tm).astype(jnp.int32)
    blk_start = jnp.arange(nblk, dtype=jnp.int32) * tm
    block_e = jnp.sum((blk_start[:, None] >= pad_end[None, :]).astype(jnp.int32), axis=1)
    last_e = jnp.sum((jnp.maximum(pad_end[-1] - 1, 0) >= pad_end).astype(jnp.int32))
    block_e = jnp.where(jnp.arange(nblk) < n_active, block_e, last_e).astype(jnp.int32)
    id_bits = 16
    assert a_tot <= 1 << id_bits and ne * tm <= 1 << id_bits
    fill_end = jnp.cumsum(padded - sizes)
    filler = jnp.arange(ne * tm, dtype=jnp.int32)
    filler_e = jnp.sum((filler[:, None] >= fill_end[None, :]).astype(jnp.int32), axis=1)
    words = jnp.concatenate([(flat_e * 2 << id_bits) + jnp.arange(a_tot, dtype=jnp.int32),
                             ((filler_e * 2 + 1) << id_bits) + filler])
    slots = jnp.sort(words)
    ident = slots & ((1 << id_bits) - 1)
    valid = ((slots >> id_bits) & 1) == 0
    src_tok = jnp.where(valid, ident // TOP_K, 0).astype(jnp.int32)
    dst_row = jnp.where(valid, (ident % TOP_K) * n + ident // TOP_K, a_tot + ident).astype(jnp.int32)
    src_tab = jnp.pad(src_tok, (0, tm)).reshape(-1, LANES)
    dst_tab = jnp.concatenate([dst_row, p + jnp.arange(tm, dtype=jnp.int32)]).reshape(-1, LANES)
    block_cnt = jnp.clip(sizes[block_e] - (blk_start - pad_start[block_e]), 0, tm).astype(jnp.int32)
    return block_e, n_active.reshape(1), block_cnt, src_tab, dst_tab, p + tm


def _rope_layout(w):
    rows, d = w.shape
    heads_per_group = GROUP // HEAD_DIM
    w5 = w.reshape(rows, d // GROUP, heads_per_group, 2, HALF)
    return w5.transpose(0, 1, 3, 2, 4).reshape(rows, d)


def _mod_rows(mod_l, bsz, d):
    m = mod_l[:bsz].reshape(bsz, 6, d)
    return jnp.pad(m, ((0, 0), (0, MOD_ROWS - 6), (0, 0)))


@jax.jit
def kernel(x, c, positions, w_ada, b_ada, norm_mix_g, norm_ffn_g, attn_w_qkv, attn_w_o, lam_q1, lam_k1,
           lam_q2, lam_k2, attn_subln_g, conv_w_pw1, conv_b_pw1, conv_w_dw, conv_b_dw, conv_ln_g,
           conv_ln_b, conv_w_pw2, conv_b_pw2, ffn_w_gate, ffn_w_up, ffn_w_down, moe_w_router,
           moe_w_gate, moe_w_up, moe_w_down, final_g):
    bsz, t, d = x.shape
    n = bsz * t
    depth = w_ada.shape[0]
    assert depth == 2 and d % GROUP == 0 and bsz <= MOD_ROWS
    tm = min(ROW_TILE, t)
    tk = min(KEY_CHUNK, t)
    x2 = x.reshape(n, d)

    c8 = jnp.pad(c, ((0, MOD_ROWS - bsz), (0, 0)))
    mod = _ada(c8, w_ada, b_ada)
    mod0 = _mod_rows(mod[0], bsz, d)
    mod1 = _mod_rows(mod[1], bsz, d)

    inv_freq = ROPE_THETA ** (-jnp.arange(HALF, dtype=F32) / HALF)
    w_qkv = attn_w_qkv[0]
    wqt = _rope_layout(w_qkv[:, :d]).T.astype(BF16)
    wk = _rope_layout(w_qkv[:, d:2 * d]).astype(BF16)
    wvt = w_qkv[:, 2 * d:].T.astype(BF16)
    qt, k, vt, (ffn_wg, ffn_wu, ffn_wd, w_o, w_pw1, w_pw2) = _qkv(
        x2, mod0, norm_mix_g[0:1], positions.reshape(n // tm, 1, tm), inv_freq.reshape(HALF, 1),
        wqt, wk, wvt, bsz, tm, tk,
        [ffn_w_gate[0], ffn_w_up[0], ffn_w_down[0], attn_w_o[0], conv_w_pw1[0], conv_w_pw2[0]])
    lam_rows = jnp.concatenate([lam_q1[0:1], lam_k1[0:1], lam_q2[0:1], lam_k2[0:1]], axis=0)
    lambda_init = 0.8 - 0.6 * math.exp(-0.3 * 0)
    ne, _, fe = moe_w_gate.shape[1:]
    o, (moe_wg, moe_wu, moe_wd) = _attention(
        qt, k, vt, lam_rows, attn_subln_g[0].reshape(V_DIM, 1), tm, tk, lambda_init,
        [moe_w_gate[0].reshape(ne * d, fe), moe_w_up[0].reshape(ne * d, fe), moe_w_down[0].reshape(ne * fe, d)])
    x2 = _ffn(o, w_o, x2, mod0, norm_ffn_g[0:1], ffn_wg, ffn_wu, ffn_wd, bsz, tm, tf=FFN_CHUNK)

    x2, h, idx, gates = _conformer(x2, mod1, norm_mix_g[1:2], w_pw1, conv_b_pw1[0:1], conv_w_dw[0],
                                   conv_b_dw[0:1], conv_ln_g[0:1], conv_ln_b[0:1], w_pw2, conv_b_pw2[0:1],
                                   norm_ffn_g[1:2], moe_w_router[0].T, bsz, tm)
    tm_e = EXPERT_BLOCK_TILES * tm
    block_e, n_active, block_cnt, src_tab, dst_tab, n_out = _routing(idx, ne, tm_e)
    tf_e = fe // EXPERT_BLOCK_TILES
    y = _moe(block_e, n_active, block_cnt, src_tab, dst_tab, h, moe_wg.reshape(ne, d, fe),
             moe_wu.reshape(ne, d, fe), moe_wd.reshape(ne, fe, d), n_out, tm_e, tf_e)
    out = _combine(y, x2, gates.T, mod1, final_g.reshape(1, d), bsz, tm)
    return out.reshape(bsz, t, d)
```

```python
import functools
import math

import jax
import jax.numpy as jnp
from jax import lax
from jax.experimental import pallas as pl
from jax.experimental.pallas import tpu as pltpu

EPS = 1e-6
ROPE_THETA = 10000.0
HEAD_DIM = 64
V_DIM = 128
V_ROWS = V_DIM + 16
HALF = HEAD_DIM // 2
LANES = 128
GROUP = 4 * HEAD_DIM
CONV_WIDTH = 31
CONV_HALO = 32
TOP_K = 2
NEG = -1e30
VMEM_LIMIT = 56 * 1024 * 1024

ROW_TILE = 512
KEY_CHUNK = 256
FFN_CHUNK = 256
EXPERT_BLOCK_TILES = 2
ADA_COL_TILE = 1536
CONV_ROW_CHUNK = 64
MOD_ROWS = 8

BF16 = jnp.bfloat16
F32 = jnp.float32


def _cparams(sem):
    return pltpu.CompilerParams(dimension_semantics=sem, vmem_limit_bytes=VMEM_LIMIT)


def _dot(a, b):
    return jnp.dot(a, b, preferred_element_type=F32)


def _dot_nt(a, b):
    return lax.dot_general(a, b, (((1,), (1,)), ((), ())), preferred_element_type=F32)


def _norm_mod(x, g, shift, scale):
    y = x * lax.rsqrt(jnp.mean(x * x, axis=-1, keepdims=True) + EPS)
    return (y * g) * (1.0 + scale) + shift


def _sigmoid(x):
    return 1.0 / (1.0 + jnp.exp(-x))


SLAB = 8


def _store_slabs(ref, val):
    rows = val.shape[0]
    for s in range(SLAB):
        ref[pl.ds(s, rows, stride=SLAB), :] = val[:, s * LANES:(s + 1) * LANES]


def _load_slab_cols(ref, s, rows):
    return ref[pl.ds(s, rows, stride=SLAB), :]


BF16_SUBLANES = 16


def _cast_plan(w2d, nsteps, step_of):
    rows, cols = w2d.shape
    share = 1
    while rows % (nsteps // share) or (rows // (nsteps // share)) % BF16_SUBLANES:
        share *= 2
        assert share <= nsteps, (rows, nsteps)
    rb = rows // (nsteps // share)
    imap = lambda *ids: (step_of(*ids) // share, 0)
    return (pl.BlockSpec((rb, cols), imap), pl.BlockSpec((rb, cols), imap),
            jax.ShapeDtypeStruct((rows, cols), BF16))


def _cast_blocks(refs):
    k = len(refs) // 2
    for src, dst in zip(refs[:k], refs[k:]):
        dst[...] = src[...].astype(BF16)


def _ada_kernel(c_ref, w_ref, b_ref, o_ref):
    c = c_ref[...]
    ca = (c * _sigmoid(c)).astype(BF16)
    o_ref[0] = _dot(ca, w_ref[0].astype(BF16)) + b_ref[0]


def _ada(c8, w_ada, b_ada):
    depth, d, d6 = w_ada.shape
    tn = ADA_COL_TILE
    return pl.pallas_call(
        _ada_kernel,
        out_shape=jax.ShapeDtypeStruct((depth, MOD_ROWS, d6), F32),
        grid=(depth, d6 // tn),
        in_specs=[
            pl.BlockSpec((MOD_ROWS, d), lambda l, j: (0, 0)),
            pl.BlockSpec((1, d, tn), lambda l, j: (l, 0, j)),
            pl.BlockSpec((1, 1, tn), lambda l, j: (l, 0, j)),
        ],
        out_specs=pl.BlockSpec((1, MOD_ROWS, tn), lambda l, j: (l, 0, j)),
        compiler_params=_cparams(("arbitrary", "arbitrary")),
        name="ada_mod",
    )(c8, w_ada, b_ada.reshape(depth, 1, d6))


def _qkv_kernel(x_ref, mod_ref, g_ref, pos_ref, freq_ref, wqt_ref, wk_ref, wvt_ref, *rest, tk, ncast):
    qt_ref, k_ref, vt_ref = rest[ncast:ncast + 3]
    _cast_blocks(rest[:ncast] + rest[ncast + 3:])
    m = mod_ref[0]
    h = _norm_mod(x_ref[...], g_ref[...], m[0:1], m[1:2]).astype(BF16)
    d = x_ref.shape[1]
    tm = x_ref.shape[0]
    ang = freq_ref[...] * pos_ref[0].astype(F32)
    reps = LANES // HALF
    ct = jnp.concatenate([jnp.cos(ang)] * reps, axis=0)
    st = jnp.concatenate([jnp.sin(ang)] * reps, axis=0)
    c = ct.T
    s = st.T
    for gi in range(d // GROUP):
        y = _dot(h, wk_ref[:, gi * GROUP:(gi + 1) * GROUP])
        a = y[:, :LANES]
        b = y[:, LANES:]
        k_ref[:, gi * GROUP:gi * GROUP + LANES] = (a * c - b * s).astype(BF16)
        k_ref[:, gi * GROUP + LANES:(gi + 1) * GROUP] = (b * c + a * s).astype(BF16)
    mult = HEAD_DIM ** -0.5 * math.log2(math.e)
    for gi in range(d // GROUP):
        y = _dot_nt(wqt_ref[gi * GROUP:(gi + 1) * GROUP, :], h)
        a = y[:LANES]
        b = y[LANES:]
        qt_ref[0, gi * GROUP:gi * GROUP + LANES, :] = ((a * ct - b * st) * mult).astype(BF16)
        qt_ref[0, gi * GROUP + LANES:(gi + 1) * GROUP, :] = ((b * ct + a * st) * mult).astype(BF16)
    ones = jnp.ones((V_ROWS - V_DIM, tk), BF16)
    for gi in range(d // GROUP):
        vt = _dot_nt(wvt_ref[gi * GROUP:(gi + 1) * GROUP, :], h)
        for hh in range(GROUP // V_DIM):
            for ci in range(tm // tk):
                vt_ref[0, ci, 2 * gi + hh, 0:V_DIM, :] = (
                    vt[hh * V_DIM:(hh + 1) * V_DIM, ci * tk:(ci + 1) * tk].astype(BF16))
                vt_ref[0, ci, 2 * gi + hh, V_DIM:V_ROWS, :] = ones


def _qkv(x2, mod, g, pos_row, freq_col, wqt, wk, wvt, bsz, tm, tk, cast_weights):
    n, d = x2.shape
    t = n // bsz
    tiles_per_b = t // tm
    nh = d // V_DIM
    row = lambda i: (i, 0)
    const = lambda i: (0, 0)
    plans = [_cast_plan(w, n // tm, lambda i: i) for w in cast_weights]
    outs = pl.pallas_call(
        functools.partial(_qkv_kernel, tk=tk, ncast=len(plans)),
        out_shape=(jax.ShapeDtypeStruct((bsz, d, t), BF16), jax.ShapeDtypeStruct((n, d), BF16),
                   jax.ShapeDtypeStruct((bsz, t // tk, nh, V_ROWS, tk), BF16)) + tuple(p[2] for p in plans),
        grid=(n // tm,),
        in_specs=[
            pl.BlockSpec((tm, d), row),
            pl.BlockSpec((1, MOD_ROWS, d), lambda i: (i // tiles_per_b, 0, 0)),
            pl.BlockSpec((1, d), const),
            pl.BlockSpec((1, 1, tm), lambda i: (i, 0, 0)),
            pl.BlockSpec((HALF, 1), const),
            pl.BlockSpec((d, d), const),
            pl.BlockSpec((d, d), const),
            pl.BlockSpec((d, d), const),
        ] + [p[0] for p in plans],
        out_specs=(pl.BlockSpec((1, d, tm), lambda i: (i // tiles_per_b, 0, i % tiles_per_b)),
                   pl.BlockSpec((tm, d), row),
                   pl.BlockSpec((1, tm // tk, nh, V_ROWS, tk),
                                lambda i: (i // tiles_per_b, i % tiles_per_b, 0, 0, 0)))
        + tuple(p[1] for p in plans),
        compiler_params=_cparams(("arbitrary",)),
        name="qkv_rope",
    )(x2, mod, g, pos_row, freq_col, wqt, wk, wvt, *cast_weights)
    return outs[0], outs[1], outs[2], outs[3:]


def _attn_kernel(qt_ref, k_ref, vt_ref, lam_ref, g_ref, *rest, tk, lambda_init, ncast):
    o_ref = rest[ncast]
    s_a, s_b, mc_a, mc_b, acc = rest[2 * ncast + 1:]
    _cast_blocks(rest[:ncast] + rest[ncast + 1:2 * ncast + 1])
    tq = qt_ref.shape[2]
    nh = acc.shape[0]
    qi = pl.program_id(2)
    n_full = qi * (tq // tk)
    row = lax.broadcasted_iota(jnp.int32, (GROUP, 1), 0)
    head = (row % LANES) // HALF
    qf = qt_ref[0].astype(F32)
    qm = [jnp.where(head == hd, qf, 0.0).astype(BF16) for hd in range(nh)]

    def scores(kj, s_ref, mc_ref, masked, lo=0):
        kc = k_ref[pl.ds(pl.multiple_of(kj * tk, tk), tk), :]
        for hd in range(nh):
            s = _dot(kc, qm[hd][:, lo:])
            if masked:
                kidx = kj * tk + lax.broadcasted_iota(jnp.int32, s.shape, 0)
                qidx = qi * tq + lo + lax.broadcasted_iota(jnp.int32, s.shape, 1)
                s = jnp.where(kidx <= qidx, s, NEG)
            s_ref[hd, :, lo:] = s
            mc_ref[hd, :, lo:] = jnp.max(s, axis=0, keepdims=True)

    def softmax_pv(kj, s_ref, mc_ref, ms, first, lo=0):
        out = []
        for hd in range(nh):
            vc = vt_ref[0, kj, hd // 2]
            s = s_ref[hd, :, lo:]
            mc = mc_ref[hd, :, lo:]
            m_new = mc if first else jnp.maximum(ms[hd][:, lo:], mc)
            pv = _dot(vc, jnp.exp2(s - m_new).astype(BF16))
            if first:
                acc[hd] = pv
            else:
                acc[hd, :, lo:] = jnp.exp2(ms[hd][:, lo:] - m_new) * acc[hd, :, lo:] + pv
            out.append(m_new if lo == 0 else jnp.concatenate([ms[hd][:, :lo], m_new], axis=1))
        return tuple(out)

    assert tq == 2 * tk
    scores(n_full, s_a, mc_a, True)
    scores(n_full + 1, s_b, mc_b, True, lo=tk)
    ms = softmax_pv(n_full, s_a, mc_a, None, True)
    scores(0, s_a, mc_a, False)
    ms = softmax_pv(n_full + 1, s_b, mc_b, ms, False, lo=tk)

    def body(jj, ms):
        j = 2 * jj
        scores(j + 1, s_b, mc_b, False)
        ms = softmax_pv(j, s_a, mc_a, ms, False)
        scores(j + 2, s_a, mc_a, False)
        return softmax_pv(j + 1, s_b, mc_b, ms, False)

    npairs = n_full // 2 - 1
    ms = lax.fori_loop(0, npairs // 2, lambda t, ms: body(2 * t + 1, body(2 * t, ms)), ms)
    ms = lax.cond(jnp.logical_and(npairs > 0, npairs % 2 == 1),
                  lambda ms: body(npairs - 1, ms), lambda ms: ms, ms)

    @pl.when(n_full > 0)
    def _():
        scores(n_full - 1, s_b, mc_b, False)
        ms2 = softmax_pv(n_full - 2, s_a, mc_a, ms, False)
        softmax_pv(n_full - 1, s_b, mc_b, ms2, False)

    lp = lam_ref[...]
    lam = (jnp.exp(jnp.sum(lp[0:1] * lp[1:2], axis=-1, keepdims=True))
           - jnp.exp(jnp.sum(lp[2:3] * lp[3:4], axis=-1, keepdims=True)) + lambda_init)
    for pr in range(nh // 2):
        a1 = acc[2 * pr]
        a2 = acc[2 * pr + 1]
        o = a1[:V_DIM] / a1[V_DIM:V_DIM + 1] - lam * (a2[:V_DIM] / a2[V_DIM:V_DIM + 1])
        o = o * lax.rsqrt(jnp.mean(o * o, axis=0, keepdims=True) + EPS)
        o = (o * g_ref[...]) * (1.0 - lambda_init)
        o_ref[:, pr * V_DIM:(pr + 1) * V_DIM] = o.T.astype(BF16)


def _attention(qt, k, vt, lam_rows, g_col, tq, tk, lambda_init, cast_weights):
    bsz, d, t = qt.shape
    n = bsz * t
    nq = t // tq
    nh = GROUP // HEAD_DIM
    ng = d // GROUP
    plans = [_cast_plan(w, bsz * ng * nq, lambda b, g, i: (b * ng + g) * nq + i) for w in cast_weights]
    outs = pl.pallas_call(
        functools.partial(_attn_kernel, tk=tk, lambda_init=lambda_init, ncast=len(plans)),
        out_shape=(jax.ShapeDtypeStruct((n, d), BF16),) + tuple(pln[2] for pln in plans),
        grid=(bsz, ng, nq),
        in_specs=[
            pl.BlockSpec((1, GROUP, tq), lambda b, g, i: (b, g, i)),
            pl.BlockSpec((t, GROUP), lambda b, g, i: (b, g)),
            pl.BlockSpec((1, t // tk, nh // 2, V_ROWS, tk), lambda b, g, i: (b, 0, g, 0, 0)),
            pl.BlockSpec((4, HEAD_DIM), lambda b, g, i: (0, 0)),
            pl.BlockSpec((V_DIM, 1), lambda b, g, i: (0, 0)),
        ] + [pln[0] for pln in plans],
        out_specs=(pl.BlockSpec((tq, GROUP), lambda b, g, i: (b * nq + i, g)),) + tuple(pln[1] for pln in plans),
        scratch_shapes=[pltpu.VMEM((nh, tk, tq), F32), pltpu.VMEM((nh, tk, tq), F32),
                        pltpu.VMEM((nh, 1, tq), F32), pltpu.VMEM((nh, 1, tq), F32),
                        pltpu.VMEM((nh, V_ROWS, tq), F32)],
        compiler_params=_cparams(("arbitrary", "arbitrary", "arbitrary")),
        name="diff_attn",
    )(qt, k, vt, lam_rows, g_col, *cast_weights)
    return outs[0], outs[1:]


def _ffn_kernel(a_ref, wo_ref, x_ref, mod_ref, g_ref, wg_ref, wu_ref, wd_ref, o_ref, act, *, tf):
    m = mod_ref[0]
    x = x_ref[...] + m[2:3] * _dot(a_ref[...], wo_ref[...])
    h = _norm_mod(x, g_ref[...], m[3:4], m[4:5]).astype(BF16)
    f = wg_ref.shape[1]
    for ci in range(f // tf):
        sl = slice(ci * tf, (ci + 1) * tf)
        gt = _dot(h, wg_ref[:, sl])
        up = _dot(h, wu_ref[:, sl])
        act[:, sl] = (gt * _sigmoid(gt) * up).astype(BF16)
    o_ref[...] = x + m[5:6] * _dot(act[...], wd_ref[...])


def _ffn(a, wo, x2, mod, g, wg, wu, wd, bsz, tm, tf):
    n, d = x2.shape
    f = wg.shape[1]
    tiles_per_b = n // bsz // tm
    row = lambda i: (i, 0)
    const = lambda i: (0, 0)
    return pl.pallas_call(
        functools.partial(_ffn_kernel, tf=tf),
        out_shape=jax.ShapeDtypeStruct((n, d), F32),
        grid=(n // tm,),
        in_specs=[
            pl.BlockSpec((tm, d), row),
            pl.BlockSpec((d, d), const),
            pl.BlockSpec((tm, d), row),
            pl.BlockSpec((1, MOD_ROWS, d), lambda i: (i // tiles_per_b, 0, 0)),
            pl.BlockSpec((1, d), const),
            pl.BlockSpec((d, f), const),
            pl.BlockSpec((d, f), const),
            pl.BlockSpec((f, d), const),
        ],
        out_specs=pl.BlockSpec((tm, d), row),
        scratch_shapes=[pltpu.VMEM((tm, f), BF16)],
        compiler_params=_cparams(("arbitrary",)),
        name="oproj_dense_swiglu",
    )(a, wo, x2, mod, g, wg, wu, wd)


def _route_top2(wr_t, h, idx_ref, gate_ref):
    logits = lax.dot_general(wr_t, h, (((1,), (1,)), ((), ())),
                             preferred_element_type=F32, precision=lax.Precision.HIGHEST)
    ne = logits.shape[0]
    eidx = lax.broadcasted_iota(jnp.int32, logits.shape, 0)
    v1 = jnp.max(logits, axis=0, keepdims=True)
    i1 = jnp.min(jnp.where(logits == v1, eidx, ne), axis=0, keepdims=True)
    rest = jnp.where(eidx == i1, -jnp.inf, logits)
    v2 = jnp.max(rest, axis=0, keepdims=True)
    i2 = jnp.min(jnp.where(rest == v2, eidx, ne), axis=0, keepdims=True)
    e2 = jnp.exp(v2 - v1)
    den = 1.0 + e2
    idx_ref[0:1, :] = i1
    idx_ref[1:2, :] = i2
    gate_ref[0:1, :] = 1.0 / den
    gate_ref[1:2, :] = e2 / den


def _conformer_kernel(x_ref, mod_ref, gm_ref, w1_ref, b1_ref, wdw_ref, bdw_ref, lg_ref, lb_ref, w2_ref, b2_ref,
                      gf_ref, wr_ref, x_out_ref, h_ref, idx_ref, gate_ref, ext, vbuf, *, tiles_per_b, rb):
    tm, d = x_ref.shape
    nchunks = d // LANES
    first = (pl.program_id(0) % tiles_per_b) == 0

    @pl.when(first)
    def _():
        ext[:, 0:CONV_HALO, :] = jnp.zeros((nchunks, CONV_HALO, LANES), F32)

    @pl.when(jnp.logical_not(first))
    def _():
        ext[:, 0:CONV_HALO, :] = ext[:, tm:tm + CONV_HALO, :]

    m = mod_ref[0]
    x = x_ref[...]
    h = _norm_mod(x, gm_ref[...], m[0:1], m[1:2]).astype(BF16)
    off = CONV_HALO - (CONV_WIDTH - 1)
    tn = 2 * LANES
    for ci in range(d // tn):
        a = _dot(h, w1_ref[:, ci * tn:(ci + 1) * tn]) + b1_ref[:, ci * tn:(ci + 1) * tn]
        gt = _dot(h, w1_ref[:, d + ci * tn:d + (ci + 1) * tn]) + b1_ref[:, d + ci * tn:d + (ci + 1) * tn]
        u = a * _sigmoid(gt)
        for half in range(tn // LANES):
            c = ci * (tn // LANES) + half
            cols = slice(c * LANES, (c + 1) * LANES)
            ext[c, CONV_HALO:, :] = u[:, half * LANES:(half + 1) * LANES]
            for r in range(tm // rb):
                acc = jnp.broadcast_to(bdw_ref[:, cols], (rb, LANES))
                for kk in range(CONV_WIDTH):
                    acc = acc + wdw_ref[kk:kk + 1, cols] * ext[c, r * rb + off + kk:r * rb + off + kk + rb, :]
                vbuf[r * rb:(r + 1) * rb, cols] = acc
    v = vbuf[...]
    mu = jnp.mean(v, axis=-1, keepdims=True)
    vc = v - mu
    var = jnp.mean(vc * vc, axis=-1, keepdims=True)
    y = vc * lax.rsqrt(var + EPS) * lg_ref[...] + lb_ref[...]
    y = (y * _sigmoid(y)).astype(BF16)
    x = x + m[2:3] * (_dot(y, w2_ref[...]) + b2_ref[...])
    x_out_ref[...] = x
    hf = _norm_mod(x, gf_ref[...], m[3:4], m[4:5])
    _store_slabs(h_ref, hf)
    _route_top2(wr_ref[...], hf, idx_ref, gate_ref)


def _conformer(x2, mod, gm, w1, b1, wdw, bdw, ln_g, ln_b, w2, b2, gf, wr_t, bsz, tm):
    n, d = x2.shape
    ne = wr_t.shape[0]
    tiles_per_b = n // bsz // tm
    row = lambda i: (i, 0)
    const = lambda i: (0, 0)
    vec = pl.BlockSpec((1, d), const)
    return pl.pallas_call(
        functools.partial(_conformer_kernel, tiles_per_b=tiles_per_b, rb=CONV_ROW_CHUNK),
        out_shape=(jax.ShapeDtypeStruct((n, d), F32), jax.ShapeDtypeStruct((n * SLAB, LANES), F32),
                   jax.ShapeDtypeStruct((TOP_K, n), jnp.int32), jax.ShapeDtypeStruct((TOP_K, n), F32)),
        grid=(n // tm,),
        in_specs=[
            pl.BlockSpec((tm, d), row),
            pl.BlockSpec((1, MOD_ROWS, d), lambda i: (i // tiles_per_b, 0, 0)),
            vec,
            pl.BlockSpec((d, 2 * d), const),
            pl.BlockSpec((1, 2 * d), const),
            pl.BlockSpec((CONV_WIDTH, d), const),
            vec, vec, vec,
            pl.BlockSpec((d, d), const),
            vec, vec,
            pl.BlockSpec((ne, d), const),
        ],
        out_specs=(pl.BlockSpec((tm, d), row), pl.BlockSpec((tm * SLAB, LANES), row),
                   pl.BlockSpec((TOP_K, tm), lambda i: (0, i)), pl.BlockSpec((TOP_K, tm), lambda i: (0, i))),
        scratch_shapes=[pltpu.VMEM((d // LANES, tm + CONV_HALO, LANES), F32), pltpu.VMEM((tm, d), F32)],
        compiler_params=_cparams(("arbitrary",)),
        name="conformer_router",
    )(x2, mod, gm, w1, b1, wdw, bdw, ln_g, ln_b, w2, b2, gf, wr_t)


def _moe_kernel(be_ref, na_ref, cnt_ref, src_ref, dst_ref, h_hbm, wg_ref, wu_ref, wd_ref, y_hbm,
                xbuf, xb16, acc, ybuf, gsem, ssem, *, n_assign, n_spare_blocks):
    tm = xb16.shape[0]
    nparts, prows = acc.shape[0:2]
    nblk = pl.num_programs(0)
    rpb = tm // LANES
    blk = pl.program_id(0)
    f = pl.program_id(1)
    na = na_ref[0]
    active = blk < na
    slot = blk % 2
    other = 1 - slot

    def slab(row):
        start = row * SLAB
        return pl.ds(start if isinstance(start, int) else pl.multiple_of(start, SLAB), SLAB)

    def gather_row(trow, col, buf, part, r):
        return pltpu.make_async_copy(h_hbm.at[slab(src_ref[trow, col])], xbuf.at[buf, part, slab(r)],
                                     gsem.at[buf])

    def scatter_row(trow, col, buf, part, r):
        return pltpu.make_async_copy(ybuf.at[buf, part, slab(r)], y_hbm.at[slab(dst_ref[trow, col])],
                                     ssem.at[buf])

    def wait_gather(buf):
        pltpu.make_async_copy(xbuf.at[buf], xbuf.at[buf], gsem.at[buf]).wait()

    def wait_scatter(buf):
        pltpu.make_async_copy(ybuf.at[buf], ybuf.at[buf], ssem.at[buf]).wait()

    def scatter_block(b, buf):
        def body(i, carry):
            scatter_row(b * rpb + i // LANES, i % LANES, buf, i // prows, i % prows).start()
            return carry
        lax.fori_loop(0, tm, body, 0)
        wait_scatter(buf)

    @pl.when(jnp.logical_and(blk == 0, f == 0))
    def _():
        def body(i, carry):
            gather_row(i // LANES, i % LANES, 0, i // prows, i % prows).start()
            return carry
        lax.fori_loop(0, tm, body, 0)
        ybuf[1] = jnp.zeros(ybuf.shape[1:], F32)
        spare_fill = [pltpu.make_async_copy(
            ybuf.at[1, 0], y_hbm.at[pl.ds((n_assign + j * prows) * SLAB, prows * SLAB)], ssem.at[1])
            for j in range(n_spare_blocks * nparts)]
        for cp in spare_fill:
            cp.start()
        for cp in spare_fill:
            cp.wait()

    @pl.when(jnp.logical_and(active, f == 0))
    def _():
        wait_gather(slot)
        for part in range(nparts):
            for s in range(SLAB):
                xb16[part * prows:(part + 1) * prows, s * LANES:(s + 1) * LANES] = (
                    _load_slab_cols(xbuf.at[slot, part], s, prows).astype(BF16))
        acc[...] = jnp.zeros(acc.shape, F32)

    def ffn_rows(part):
        xb = xb16[part * prows:(part + 1) * prows]
        gt = _dot(xb, wg_ref[0])
        up = _dot(xb, wu_ref[0])
        acc[part] += _dot((gt * _sigmoid(gt) * up).astype(BF16), wd_ref[0])

    @pl.when(active)
    def _():
        qrows = prows // LANES
        gbase = (blk + 1) * rpb + f * qrows
        sbase = jnp.where(blk == 0, nblk, blk - 1) * rpb + f * qrows
        for i in range(prows):
            gather_row(gbase + i // LANES, i % LANES, other, f, i).start()
            scatter_row(sbase + i // LANES, i % LANES, other, f, i).start()
        ffn_rows(0)

    assert nparts == 2
    @pl.when(jnp.logical_and(active, cnt_ref[blk] > prows))
    def _():
        ffn_rows(1)

    @pl.when(jnp.logical_and(active, f == nparts - 1))
    def _():
        @pl.when(blk >= 1)
        def _():
            wait_scatter(slot)
        for part in range(nparts):
            _store_slabs(ybuf.at[slot, part], acc[part])

    @pl.when(jnp.logical_and(f == 0, blk == na))
    def _():
        wait_gather(slot)
        wait_scatter(slot)
        scatter_block(blk - 1, other)

    @pl.when(jnp.logical_and(jnp.logical_and(f == nparts - 1, blk == nblk - 1), active))
    def _():
        wait_gather(other)
        wait_scatter(other)
        scatter_block(blk, slot)


def _moe(block_e, n_active, block_cnt, src_tab, dst_tab, h, wg, wu, wd, n_out, tm, tf):
    n = h.shape[0] // SLAB
    ne, d, fe = wg.shape
    assert d == SLAB * LANES
    nf = fe // tf
    nblk = dst_tab.shape[0] * LANES // tm - 1
    assert tm % (nf * LANES) == 0

    def wsel(blk, f, na):
        return jnp.where(blk < na[0], f, nf - 1)

    grid_spec = pltpu.PrefetchScalarGridSpec(
        num_scalar_prefetch=5,
        grid=(nblk, nf),
        in_specs=[
            pl.BlockSpec(memory_space=pl.ANY),
            pl.BlockSpec((1, d, tf), lambda blk, f, be, na, cnt, src, dst: (be[blk], 0, wsel(blk, f, na))),
            pl.BlockSpec((1, d, tf), lambda blk, f, be, na, cnt, src, dst: (be[blk], 0, wsel(blk, f, na))),
            pl.BlockSpec((1, tf, d), lambda blk, f, be, na, cnt, src, dst: (be[blk], wsel(blk, f, na), 0)),
        ],
        out_specs=pl.BlockSpec(memory_space=pl.ANY),
        scratch_shapes=[pltpu.VMEM((2, nf, tm // nf * SLAB, LANES), F32), pltpu.VMEM((tm, d), BF16),
                        pltpu.VMEM((nf, tm // nf, d), F32), pltpu.VMEM((2, nf, tm // nf * SLAB, LANES), F32),
                        pltpu.SemaphoreType.DMA((2,)), pltpu.SemaphoreType.DMA((2,))],
    )
    return pl.pallas_call(
        functools.partial(_moe_kernel, n_assign=n * TOP_K, n_spare_blocks=ne),
        out_shape=jax.ShapeDtypeStruct((n_out * SLAB, LANES), F32),
        grid_spec=grid_spec,
        compiler_params=_cparams(("arbitrary", "arbitrary")),
        name="moe_experts",
    )(block_e, n_active, block_cnt, src_tab, dst_tab, h, wg, wu, wd)


def _combine_kernel(y0_ref, y1_ref, x_ref, gates_ref, mod_ref, g_ref, o_ref):
    tm, d = x_ref.shape
    gate = mod_ref[0][5:6]
    gt = gates_ref[...]
    g0 = gt[:, 0:1]
    g1 = gt[:, 1:2]
    ssq = jnp.zeros((tm, 1), F32)
    for s in range(SLAB):
        cols = slice(s * LANES, (s + 1) * LANES)
        xs = x_ref[:, cols] + gate[:, cols] * (g0 * _load_slab_cols(y0_ref, s, tm)
                                               + g1 * _load_slab_cols(y1_ref, s, tm))
        o_ref[:, cols] = xs
        ssq = ssq + jnp.sum(xs * xs, axis=-1, keepdims=True)
    o_ref[...] = o_ref[...] * lax.rsqrt(ssq * (1.0 / d) + EPS) * g_ref[...]


def _combine(y, x2, gates_t, mod, final_g, bsz, tm):
    n, d = x2.shape
    tiles_per_b = n // bsz // tm
    nt = n // tm
    row = lambda i: (i, 0)
    return pl.pallas_call(
        _combine_kernel,
        out_shape=jax.ShapeDtypeStruct((n, d), F32),
        grid=(nt,),
        in_specs=[
            pl.BlockSpec((tm * SLAB, LANES), row),
            pl.BlockSpec((tm * SLAB, LANES), lambda i: (nt + i, 0)),
            pl.BlockSpec((tm, d), row),
            pl.BlockSpec((tm, TOP_K), row),
            pl.BlockSpec((1, MOD_ROWS, d), lambda i: (i // tiles_per_b, 0, 0)),
            pl.BlockSpec((1, d), lambda i: (0, 0)),
        ],
        out_specs=pl.BlockSpec((tm, d), row),
        compiler_params=_cparams(("arbitrary",)),
        name="moe_combine_final_norm",
    )(y, y, x2, gates_t, mod, final_g)


def _routing(idx, ne, tm):
    n = idx.shape[1]
    a_tot = n * TOP_K
    flat_e = idx.T.reshape(-1)
    sizes = jnp.sum((flat_e[:, None] == jnp.arange(ne, dtype=jnp.int32)[None, :]).astype(jnp.int32), axis=0)
    padded = (sizes + tm - 1) // tm * tm
    pad_end = jnp.cumsum(padded)
    pad_start = pad_end - padded
    p = a_tot + ne * tm
    nblk = p // tm
    n_active = (pad_end[-1] // tm).astype(jnp.int32)
    blk_start = jnp.arange(nblk, dtype=jnp.int32) * tm
    block_e = jnp.sum((blk_start[:, None] >= pad_end[None, :]).astype(jnp.int32), axis=1)
    last_e = jnp.sum((jnp.maximum(pad_end[-1] - 1, 0) >= pad_end).astype(jnp.int32))
    block_e = jnp.where(jnp.arange(nblk) < n_active, block_e, last_e).astype(jnp.int32)
    id_bits = 16
    assert a_tot <= 1 << id_bits and ne * tm <= 1 << id_bits
    fill_end = jnp.cumsum(padded - sizes)
    filler = jnp.arange(ne * tm, dtype=jnp.int32)
    filler_e = jnp.sum((filler[:, None] >= fill_end[None, :]).astype(jnp.int32), axis=1)
    words = jnp.concatenate([(flat_e * 2 << id_bits) + jnp.arange(a_tot, dtype=jnp.int32),
                             ((filler_e * 2 + 1) << id_bits) + filler])
    slots = jnp.sort(words)
    ident = slots & ((1 << id_bits) - 1)
    valid = ((slots >> id_bits) & 1) == 0
    src_tok = jnp.where(valid, ident // TOP_K, 0).astype(jnp.int32)
    dst_row = jnp.where(valid, (ident % TOP_K) * n + ident // TOP_K, a_tot + ident).astype(jnp.int32)
    src_tab = jnp.pad(src_tok, (0, tm)).reshape(-1, LANES)
    dst_tab = jnp.concatenate([dst_row, p + jnp.arange(tm, dtype=jnp.int32)]).reshape(-1, LANES)
    block_cnt = jnp.clip(sizes[block_e] - (blk_start - pad_start[block_e]), 0, tm).astype(jnp.int32)
    return block_e, n_active.reshape(1), block_cnt, src_tab, dst_tab, p + tm


def _rope_layout(w):
    rows, d = w.shape
    heads_per_group = GROUP // HEAD_DIM
    w5 = w.reshape(rows, d // GROUP, heads_per_group, 2, HALF)
    return w5.transpose(0, 1, 3, 2, 4).reshape(rows, d)


def _mod_rows(mod_l, bsz, d):
    m = mod_l[:bsz].reshape(bsz, 6, d)
    return jnp.pad(m, ((0, 0), (0, MOD_ROWS - 6), (0, 0)))


@jax.jit
def kernel(x, c, positions, w_ada, b_ada, norm_mix_g, norm_ffn_g, attn_w_qkv, attn_w_o, lam_q1, lam_k1,
           lam_q2, lam_k2, attn_subln_g, conv_w_pw1, conv_b_pw1, conv_w_dw, conv_b_dw, conv_ln_g,
           conv_ln_b, conv_w_pw2, conv_b_pw2, ffn_w_gate, ffn_w_up, ffn_w_down, moe_w_router,
           moe_w_gate, moe_w_up, moe_w_down, final_g):
    bsz, t, d = x.shape
    n = bsz * t
    depth = w_ada.shape[0]
    assert depth == 2 and d % GROUP == 0 and bsz <= MOD_ROWS
    tm = min(ROW_TILE, t)
    tk = min(KEY_CHUNK, t)
    x2 = x.reshape(n, d)

    c8 = jnp.pad(c, ((0, MOD_ROWS - bsz), (0, 0)))
    mod = _ada(c8, w_ada, b_ada)
    mod0 = _mod_rows(mod[0], bsz, d)
    mod1 = _mod_rows(mod[1], bsz, d)

    inv_freq = ROPE_THETA ** (-jnp.arange(HALF, dtype=F32) / HALF)
    w_qkv = attn_w_qkv[0]
    wqt = _rope_layout(w_qkv[:, :d]).T.astype(BF16)
    wk = _rope_layout(w_qkv[:, d:2 * d]).astype(BF16)
    wvt = w_qkv[:, 2 * d:].T.astype(BF16)
    qt, k, vt, (ffn_wg, ffn_wu, ffn_wd, w_o, w_pw1, w_pw2) = _qkv(
        x2, mod0, norm_mix_g[0:1], positions.reshape(n // tm, 1, tm), inv_freq.reshape(HALF, 1),
        wqt, wk, wvt, bsz, tm, tk,
        [ffn_w_gate[0], ffn_w_up[0], ffn_w_down[0], attn_w_o[0], conv_w_pw1[0], conv_w_pw2[0]])
    lam_rows = jnp.concatenate([lam_q1[0:1], lam_k1[0:1], lam_q2[0:1], lam_k2[0:1]], axis=0)
    lambda_init = 0.8 - 0.6 * math.exp(-0.3 * 0)
    ne, _, fe = moe_w_gate.shape[1:]
    o, (moe_wg, moe_wu, moe_wd) = _attention(
        qt, k, vt, lam_rows, attn_subln_g[0].reshape(V_DIM, 1), tm, tk, lambda_init,
        [moe_w_gate[0].reshape(ne * d, fe), moe_w_up[0].reshape(ne * d, fe), moe_w_down[0].reshape(ne * fe, d)])
    x2 = _ffn(o, w_o, x2, mod0, norm_ffn_g[0:1], ffn_wg, ffn_wu, ffn_wd, bsz, tm, tf=FFN_CHUNK)

    x2, h, idx, gates = _conformer(x2, mod1, norm_mix_g[1:2], w_pw1, conv_b_pw1[0:1], conv_w_dw[0],
                                   conv_b_dw[0:1], conv_ln_g[0:1], conv_ln_b[0:1], w_pw2, conv_b_pw2[0:1],
                                   norm_ffn_g[1:2], moe_w_router[0].T, bsz, tm)
    tm_e = EXPERT_BLOCK_TILES * tm
    block_e, n_active, block_cnt, src_tab, dst_tab, n_out = _routing(idx, ne, tm_e)
    tf_e = fe // EXPERT_BLOCK_TILES
    y = _moe(block_e, n_active, block_cnt, src_tab, dst_tab, h, moe_wg.reshape(ne, d, fe),
             moe_wu.reshape(ne, d, fe), moe_wd.reshape(ne, fe, d), n_out, tm_e, tf_e)
    out = _combine(y, x2, gates.T, mod1, final_g.reshape(1, d), bsz, tm)
    return out.reshape(bsz, t, d)
```
